```python
import jax, jax.numpy as jnp
from jax import lax
import numpy as np

D_MODEL = 1024
BATCH = 16
SEQ = 2048
DEPTH = 1

CHUNK = 64
D_MIX = D_MODEL
D_CONV = D_MIX // 2
CONV_GROUPS = 8
CONV_GROUP_DIM = D_CONV // CONV_GROUPS
CONV_WIDTH = 3
D_RWKV = D_MIX - D_CONV
RWKV_HEAD_DIM = 64
RWKV_HEADS = D_RWKV // RWKV_HEAD_DIM
DECAY_LORA = 64
AAA_LORA = 64
GATE_LORA = 128
N_EXPERTS = 32
TOP_K = 4
D_EXPERT = D_MODEL
SWIGLU_LIMIT = 7.0
SWIGLU_ALPHA = 1.702
NORM_EPS = 1e-5
GN_EPS = 64e-5
MOE_BLOCK = 256

COLS_CONV = 3 * D_CONV
COLS_RWKV = 3 * D_RWKV + DECAY_LORA + AAA_LORA + GATE_LORA
D_IN_PROJ = COLS_CONV + COLS_RWKV

kernel_name = 'hymba_conv_rwkv7_moe_adaln_block'


def rms_norm(x, gain):
    xf = x.astype(jnp.float32)
    y = xf * lax.rsqrt(jnp.mean(xf * xf, axis=-1, keepdims=True) + NORM_EPS)
    return (y * gain.astype(jnp.float32)).astype(x.dtype)


def modulate(h, shift, scale):
    return h * (1 + scale[:, None, :]) + shift[:, None, :]


def causal_shift(p):
    return jnp.pad(p, ((0, 0), (1, 0), (0, 0)))[:, :-1]


def short_conv_mixer(p_conv, conv_w, conv_gn):
    b_gate, c_gate, h = jnp.split(p_conv, 3, axis=-1)
    u = c_gate * h
    y = lax.conv_general_dilated(
        u, conv_w[:, None, :].astype(u.dtype), window_strides=(1,),
        padding=[(CONV_WIDTH - 1, 0)], dimension_numbers=('NWC', 'WIO', 'NWC'),
        feature_group_count=D_CONV)
    y = (b_gate * y).astype(jnp.float32)
    bn, t, _ = y.shape
    yg = y.reshape(bn, t, CONV_GROUPS, CONV_GROUP_DIM)
    yg = yg * lax.rsqrt(jnp.mean(yg * yg, axis=-1, keepdims=True) + NORM_EPS)
    return (yg.reshape(bn, t, D_CONV) * conv_gn.astype(jnp.float32)).astype(p_conv.dtype)


def rwkv7_mixer(p_rwkv, mu, w0, w_up, a0, a_up, g_up, k_k, k_a, r_k, ln_w, ln_b):
    p = p_rwkv + (causal_shift(p_rwkv) - p_rwkv) * mu
    s1, s2, s3 = D_RWKV, 2 * D_RWKV, 3 * D_RWKV
    r, k, v, dw, da, dg = jnp.split(p, [s1, s2, s3, s3 + DECAY_LORA, s3 + DECAY_LORA + AAA_LORA], axis=-1)
    w_log = -jax.nn.softplus(-(w0 + jnp.tanh(dw) @ w_up)) - 0.5
    decay = jnp.exp(-jnp.exp(w_log.astype(jnp.float32)))
    a = jax.nn.sigmoid(a0 + da @ a_up)
    g = (jax.nn.sigmoid(dg) @ g_up).astype(jnp.float32)
    bn, t, _ = r.shape

    def heads(z):
        return z.reshape(bn, t, RWKV_HEADS, RWKV_HEAD_DIM).astype(jnp.float32)

    r, k, v, a, decay = heads(r), heads(k), heads(v), heads(a), heads(decay)
    hn = (RWKV_HEADS, RWKV_HEAD_DIM)
    kk = k * k_k.reshape(hn).astype(jnp.float32)
    kk = kk / jnp.maximum(jnp.sqrt(jnp.sum(kk * kk, axis=-1, keepdims=True)), 1e-12)
    k = k * (1 + (a - 1) * k_a.reshape(hn).astype(jnp.float32))

    def step(state, inp):
        r_t, w_t, k_t, v_t, kk_t, a_t = inp
        sa = jnp.einsum('bhij,bhj->bhi', state, -kk_t)
        state = (state * w_t[:, :, None, :]
                 + sa[..., :, None] * (kk_t * a_t)[..., None, :]
                 + v_t[..., :, None] * k_t[..., None, :])
        return state, jnp.einsum('bhij,bhj->bhi', state, r_t)

    xs = tuple(jnp.moveaxis(z, 1, 0) for z in (r, decay, k, v, kk, a))
    state0 = jnp.zeros((bn, RWKV_HEADS, RWKV_HEAD_DIM, RWKV_HEAD_DIM), jnp.float32)
    _, o = lax.scan(step, state0, xs)
    o = jnp.moveaxis(o, 0, 1)
    mean = jnp.mean(o, axis=-1, keepdims=True)
    var = jnp.mean(jnp.square(o - mean), axis=-1, keepdims=True)
    o = (o - mean) * lax.rsqrt(var + GN_EPS) * ln_w.reshape(hn).astype(jnp.float32) + ln_b.reshape(hn).astype(jnp.float32)
    o = o + jnp.sum(r * k * r_k.reshape(hn).astype(jnp.float32), axis=-1, keepdims=True) * v
    return (o.reshape(bn, t, D_RWKV) * g).astype(p_rwkv.dtype)


def moe_ffn(h, router_w, router_b, w1, b1, w2, b2):
    bn, t, d = h.shape
    xf = h.reshape(-1, d)
    n_tok = xf.shape[0]
    n_rows = n_tok * TOP_K
    logits = (xf @ router_w + router_b).astype(jnp.float32)
    top_logits, top_idx = lax.top_k(logits, TOP_K)
    gates = jax.nn.softmax(top_logits, axis=-1)
    flat_e = top_idx.reshape(-1)
    flat_tok = jnp.repeat(jnp.arange(n_tok, dtype=jnp.int32), TOP_K)
    flat_g = gates.reshape(-1)
    order = jnp.argsort(flat_e)
    e_sorted, tok_sorted, g_sorted = flat_e[order], flat_tok[order], flat_g[order]
    counts = jnp.bincount(flat_e, length=N_EXPERTS)
    padded = (counts + MOE_BLOCK - 1) // MOE_BLOCK * MOE_BLOCK
    pad_end = jnp.cumsum(padded)
    pad_start = pad_end - padded
    start = jnp.cumsum(counts) - counts
    dest = pad_start[e_sorted] + jnp.arange(n_rows, dtype=jnp.int32) - start[e_sorted]
    n_blocks = -(-n_rows // MOE_BLOCK) + N_EXPERTS
    m_pad = n_blocks * MOE_BLOCK
    row_tok = jnp.zeros((m_pad,), jnp.int32).at[dest].set(tok_sorted)
    row_gate = jnp.zeros((m_pad,), jnp.float32).at[dest].set(g_sorted)
    block_expert = jnp.minimum(
        jnp.searchsorted(pad_end, jnp.arange(n_blocks, dtype=jnp.int32) * MOE_BLOCK, side='right'),
        N_EXPERTS - 1)

    def expert_block(args):
        e, toks, gts = args
        xb = xf[toks]
        hb = xb @ w1[e] + b1[e]
        x_glu = jnp.minimum(hb[:, ::2], SWIGLU_LIMIT)
        x_lin = jnp.clip(hb[:, 1::2], -SWIGLU_LIMIT, SWIGLU_LIMIT)
        act = x_glu * jax.nn.sigmoid(SWIGLU_ALPHA * x_glu) * (x_lin + 1)
        yb = act @ w2[e] + b2[e]
        return yb * gts[:, None].astype(yb.dtype)

    y_rows = lax.map(expert_block, (block_expert,
                                    row_tok.reshape(n_blocks, MOE_BLOCK),
                                    row_gate.reshape(n_blocks, MOE_BLOCK)))
    y = jax.ops.segment_sum(y_rows.reshape(m_pad, d), row_tok, num_segments=n_tok)
    return y.reshape(bn, t, d)


def setup_inputs(seed: int = 0) -> dict:
    key = jax.random.key(seed)
    ks = jax.random.split(key, 30)
    L = DEPTH

    def nrm(k, shape, s):
        return jax.random.normal(k, shape, jnp.float32) * s

    return {
        'x': nrm(ks[0], (BATCH, SEQ, D_MODEL), 1.0),
        'c': nrm(ks[1], (BATCH, D_MODEL), 1.0),
        'ada_w': nrm(ks[2], (L, D_MODEL, 6 * D_MODEL), 0.5 * D_MODEL ** -0.5),
        'ada_b': nrm(ks[3], (L, 6 * D_MODEL), 0.01),
        'norm1_g': 1.0 + nrm(ks[4], (L, D_MODEL), 0.05),
        'w_in': nrm(ks[5], (L, D_MODEL, D_IN_PROJ), D_MODEL ** -0.5),
        'conv_w': nrm(ks[6], (L, CONV_WIDTH, D_CONV), CONV_WIDTH ** -0.5),
        'conv_gn': 1.0 + nrm(ks[7], (L, D_CONV), 0.05),
        'rwkv_mu': jax.random.uniform(ks[8], (L, COLS_RWKV), jnp.float32),
        'rwkv_w0': jax.random.uniform(ks[9], (L, D_RWKV), jnp.float32, -6.0, 1.0),
        'rwkv_w_up': nrm(ks[10], (L, DECAY_LORA, D_RWKV), 0.5 * DECAY_LORA ** -0.5),
        'rwkv_a0': nrm(ks[11], (L, D_RWKV), 0.1),
        'rwkv_a_up': nrm(ks[12], (L, AAA_LORA, D_RWKV), AAA_LORA ** -0.5),
        'rwkv_g_up': nrm(ks[13], (L, GATE_LORA, D_RWKV), GATE_LORA ** -0.5),
        'rwkv_k_k': 0.85 + nrm(ks[14], (L, D_RWKV), 0.05),
        'rwkv_k_a': 1.0 + nrm(ks[15], (L, D_RWKV), 0.05),
        'rwkv_r_k': nrm(ks[16], (L, D_RWKV), 0.1),
        'rwkv_ln_w': 1.0 + nrm(ks[17], (L, D_RWKV), 0.05),
        'rwkv_ln_b': nrm(ks[18], (L, D_RWKV), 0.01),
        'w_out': nrm(ks[19], (L, D_MIX, D_MODEL), D_MIX ** -0.5),
        'norm2_g': 1.0 + nrm(ks[20], (L, D_MODEL), 0.05),
        'router_w': nrm(ks[21], (L, D_MODEL, N_EXPERTS), D_MODEL ** -0.5),
        'router_b': nrm(ks[22], (L, N_EXPERTS), 0.01),
        'exp_w1': nrm(ks[23], (L, N_EXPERTS, D_MODEL, 2 * D_EXPERT), D_MODEL ** -0.5),
        'exp_b1': nrm(ks[24], (L, N_EXPERTS, 2 * D_EXPERT), 0.01),
        'exp_w2': nrm(ks[25], (L, N_EXPERTS, D_EXPERT, D_MODEL), D_EXPERT ** -0.5),
        'exp_b2': nrm(ks[26], (L, N_EXPERTS, D_MODEL), 0.01),
        'final_g': 1.0 + nrm(ks[27], (D_MODEL,), 0.05),
    }


def reference(x, c, ada_w, ada_b, norm1_g, w_in, conv_w, conv_gn, rwkv_mu, rwkv_w0, rwkv_w_up,
              rwkv_a0, rwkv_a_up, rwkv_g_up, rwkv_k_k, rwkv_k_a, rwkv_r_k, rwkv_ln_w, rwkv_ln_b,
              w_out, norm2_g, router_w, router_b, exp_w1, exp_b1, exp_w2, exp_b2, final_g):
    h = x
    c_act = jax.nn.silu(c)
    for l in range(DEPTH):
        mod = c_act @ ada_w[l] + ada_b[l]
        sh1, sc1, gt1, sh2, sc2, gt2 = jnp.split(mod, 6, axis=-1)
        u = modulate(rms_norm(h, norm1_g[l]), sh1, sc1)
        proj = u @ w_in[l]
        y_conv = short_conv_mixer(proj[..., :COLS_CONV], conv_w[l], conv_gn[l])
        y_rwkv = rwkv7_mixer(proj[..., COLS_CONV:], rwkv_mu[l], rwkv_w0[l], rwkv_w_up[l],
                             rwkv_a0[l], rwkv_a_up[l], rwkv_g_up[l], rwkv_k_k[l], rwkv_k_a[l],
                             rwkv_r_k[l], rwkv_ln_w[l], rwkv_ln_b[l])
        mix = jnp.concatenate([y_conv, y_rwkv], axis=-1) @ w_out[l]
        h = h + gt1[:, None, :] * mix
        u2 = modulate(rms_norm(h, norm2_g[l]), sh2, sc2)
        h = h + gt2[:, None, :] * moe_ffn(u2, router_w[l], router_b[l], exp_w1[l], exp_b1[l],
                                           exp_w2[l], exp_b2[l])
    return rms_norm(h, final_g)
```

```python
import functools

import jax
import jax.numpy as jnp
from jax import lax
from jax.experimental import pallas as pl
from jax.experimental.pallas import tpu as pltpu

F32 = jnp.float32
BF16 = jnp.bfloat16
I32 = jnp.int32

D_MODEL = 1024
D_CONV = 512
CONV_GROUP_DIM = 64
D_RWKV = 512
HEAD_DIM = 64
N_HEADS = D_RWKV // HEAD_DIM
DECAY_LORA = 64
AAA_LORA = 64
GATE_LORA = 128
COLS_CONV = 3 * D_CONV
COLS_RWKV = 3 * D_RWKV + DECAY_LORA + AAA_LORA + GATE_LORA
N_EXPERTS = 32
TOP_K = 4
D_EXPERT = D_MODEL
SWIGLU_LIMIT = 7.0
SWIGLU_ALPHA = 1.702
NORM_EPS = 1e-5
GN_EPS = 64e-5

LANES = 128
SUBLANES = 8
CHUNK = 64
ROW_BLOCK = 256
TOKEN_TILE = 512
MOVE_TILE = 256
VMEM_LIMIT = 48 * 1024 * 1024


def _cparams(*sem):
    return pltpu.CompilerParams(dimension_semantics=sem, vmem_limit_bytes=VMEM_LIMIT)


def _mm(a, b):
    return jnp.dot(a.astype(BF16), b.astype(BF16), preferred_element_type=F32)


def _mm_nt(a, b):
    return lax.dot_general(a.astype(BF16), b.astype(BF16), (((1,), (1,)), ((), ())),
                           preferred_element_type=F32)


def _mm_tn(a, b):
    return lax.dot_general(a.astype(BF16), b.astype(BF16), (((0,), (0,)), ((), ())),
                           preferred_element_type=F32)


def _split2(x):
    hi = x.astype(BF16)
    lo = (x - hi.astype(F32)).astype(BF16)
    return hi, lo


def _split3(x):
    hi = x.astype(BF16)
    r = x - hi.astype(F32)
    mid = r.astype(BF16)
    lo = (r - mid.astype(F32)).astype(BF16)
    return hi, mid, lo


def _sigmoid(x):
    return 1.0 / (1.0 + jnp.exp(-x))


def _rms(x):
    return x * lax.rsqrt(jnp.mean(x * x, axis=-1, keepdims=True) + NORM_EPS)


def _ada_body(c_ref, w_ref, b_ref, o_ref):
    c = c_ref[...]
    ca = c * _sigmoid(c)
    ah, al = _split2(ca)
    wh, wl = _split2(w_ref[...])
    acc = jnp.dot(ah, wh, preferred_element_type=F32)
    acc += jnp.dot(ah, wl, preferred_element_type=F32)
    acc += jnp.dot(al, wh, preferred_element_type=F32)
    o_ref[...] = acc + b_ref[...]


def _ada_mod(c, ada_w, ada_b):
    bsz = c.shape[0]
    n = ada_w.shape[1]
    tn = 1024
    return pl.pallas_call(
        _ada_body,
        name="ada_mod",
        grid=(n // tn,),
        in_specs=[pl.BlockSpec((bsz, D_MODEL), lambda j: (0, 0)),
                  pl.BlockSpec((D_MODEL, tn), lambda j: (0, j)),
                  pl.BlockSpec((1, tn), lambda j: (0, j))],
        out_specs=pl.BlockSpec((bsz, tn), lambda j: (0, j)),
        out_shape=jax.ShapeDtypeStruct((bsz, n), F32),
        compiler_params=_cparams("parallel"),
    )(c, ada_w, ada_b.reshape(1, n))


def _inproj_body(x_ref, g_ref, sh_ref, sc_ref, w_ref, oc_ref, or_ref):
    u = _rms(x_ref[...]) * g_ref[...]
    u = u * (1.0 + sc_ref[0, 0]) + sh_ref[0, 0]
    acc = jnp.dot(u.astype(BF16), w_ref[...], preferred_element_type=F32)
    oc_ref[...] = acc[:, :COLS_CONV]
    or_ref[...] = acc[:, COLS_CONV:]


def _in_proj(xf, g1, mod6, w_in_bf, tiles_per_batch):
    n_tok = xf.shape[0]
    tm = TOKEN_TILE
    n_in = w_in_bf.shape[1]
    return pl.pallas_call(
        _inproj_body,
        name="in_proj",
        grid=(n_tok // tm,),
        in_specs=[pl.BlockSpec((tm, D_MODEL), lambda i: (i, 0)),
                  pl.BlockSpec((1, D_MODEL), lambda i: (0, 0)),
                  pl.BlockSpec((1, 1, 1, D_MODEL), lambda i: (i // tiles_per_batch, 0, 0, 0)),
                  pl.BlockSpec((1, 1, 1, D_MODEL), lambda i: (i // tiles_per_batch, 1, 0, 0)),
                  pl.BlockSpec((D_MODEL, n_in), lambda i: (0, 0))],
        out_specs=[pl.BlockSpec((tm, COLS_CONV), lambda i: (i, 0)),
                   pl.BlockSpec((tm, COLS_RWKV), lambda i: (i, 0))],
        out_shape=[jax.ShapeDtypeStruct((n_tok, COLS_CONV), F32),
                   jax.ShapeDtypeStruct((n_tok, COLS_RWKV), F32)],
        compiler_params=_cparams("parallel"),
    )(xf, g1, mod6, mod6, w_in_bf)


def _conv_body(p_ref, cw_ref, gn_ref, o_ref, carry_ref):
    @pl.when(pl.program_id(1) == 0)
    def _():
        carry_ref[...] = jnp.zeros_like(carry_ref)

    p = p_ref[0]
    tc = p.shape[0]
    b_gate = p[:, :D_CONV]
    u = p[:, D_CONV:2 * D_CONV] * p[:, 2 * D_CONV:]
    ext = jnp.concatenate([carry_ref[...], u], axis=0)
    u1 = pltpu.roll(ext, 1, axis=0)[SUBLANES:]
    u2 = pltpu.roll(ext, 2, axis=0)[SUBLANES:]
    carry_ref[...] = u[tc - SUBLANES:, :]
    cw = cw_ref[...]
    y = b_gate * (cw[0:1] * u2 + cw[1:2] * u1 + cw[2:3] * u)
    outs = []
    for g in range(D_CONV // CONV_GROUP_DIM):
        outs.append(_rms(y[:, g * CONV_GROUP_DIM:(g + 1) * CONV_GROUP_DIM]))
    o_ref[0] = jnp.concatenate(outs, axis=-1) * gn_ref[...]


def _conv_mixer(pc, conv_w, conv_gn, bsz, seq):
    tc = TOKEN_TILE
    return pl.pallas_call(
        _conv_body,
        name="conv_mixer",
        grid=(bsz, seq // tc),
        in_specs=[pl.BlockSpec((1, tc, COLS_CONV), lambda b, i: (b, i, 0)),
                  pl.BlockSpec((3, D_CONV), lambda b, i: (0, 0)),
                  pl.BlockSpec((1, D_CONV), lambda b, i: (0, 0))],
        out_specs=pl.BlockSpec((1, tc, D_CONV), lambda b, i: (b, i, 0)),
        out_shape=jax.ShapeDtypeStruct((bsz, seq, D_CONV), F32),
        scratch_shapes=[pltpu.VMEM((SUBLANES, D_CONV), F32)],
        compiler_params=_cparams("parallel", "arbitrary"),
    )(pc.reshape(bsz, seq, COLS_CONV), conv_w, conv_gn)


def _rwkv_body(p_ref, mu_ref, w0_ref, wup_ref, a0_ref, aup_ref, gup_ref, kkw_ref, kaw_ref,
               rkw_ref, lnw_ref, lnb_ref, o_ref, state_ref, prev_ref):
    @pl.when(pl.program_id(1) == 0)
    def _():
        state_ref[...] = jnp.zeros_like(state_ref)
        prev_ref[...] = jnp.zeros_like(prev_ref)

    C = CHUNK
    N = HEAD_DIM
    P = p_ref[0]
    row1 = lax.broadcasted_iota(I32, (C, 1), 0)
    Pprev = jnp.where(row1 == 0, prev_ref[0:1, :], pltpu.roll(P, 1, axis=0))
    prev_ref[...] = jnp.broadcast_to(P[C - 1:C, :], prev_ref.shape)
    p = P + (Pprev - P) * mu_ref[...]

    s1, s2, s3 = D_RWKV, 2 * D_RWKV, 3 * D_RWKV
    r = p[:, :s1]
    k = p[:, s1:s2]
    v = p[:, s2:s3]
    dw = p[:, s3:s3 + DECAY_LORA]
    da = p[:, s3 + DECAY_LORA:s3 + DECAY_LORA + AAA_LORA]
    dg = p[:, s3 + DECAY_LORA + AAA_LORA:]

    z = -(w0_ref[...] + _mm(jnp.tanh(dw), wup_ref[...]))
    softplus = jnp.maximum(z, 0.0) + jnp.log(1.0 + jnp.exp(-jnp.abs(z)))
    logw = -jnp.exp(-softplus - 0.5)
    a = _sigmoid(a0_ref[...] + _mm(da, aup_ref[...]))
    g = _mm(_sigmoid(dg), gup_ref[...])
    kk = k * kkw_ref[...]
    kmod = k * (1.0 + (a - 1.0) * kaw_ref[...])

    ri = lax.broadcasted_iota(I32, (C, C), 0)
    ci = lax.broadcasted_iota(I32, (C, C), 1)
    strict = ri > ci
    incl = ri >= ci
    eye = jnp.where(ri == ci, 1.0, 0.0).astype(F32)
    tri = jnp.where(incl, 1.0, 0.0).astype(BF16)
    cum = sum(jnp.dot(tri, part, preferred_element_type=F32) for part in _split3(logw))
    gam = jnp.exp(cum)
    gam_prev = jnp.exp(cum - logw)
    ginv = jnp.exp(-cum)

    outs = []
    for h in range(N_HEADS):
        sl = slice(h * N, (h + 1) * N)
        kkh = kk[:, sl]
        kkh = kkh / jnp.maximum(jnp.sqrt(jnp.sum(kkh * kkh, axis=-1, keepdims=True)), 1e-12)
        rh, kh, vh = r[:, sl], kmod[:, sl], v[:, sl]
        At = -kkh * gam_prev[:, sl]
        Bt = kkh * a[:, sl] * ginv[:, sl]
        Kt = kh * ginv[:, sl]
        Rt = rh * gam[:, sl]
        BK = jnp.concatenate([Bt, Kt], axis=0)
        M = _mm_nt(jnp.concatenate([At, Rt], axis=0), BK)
        Lab = jnp.where(strict, M[:C, :C], 0.0)
        Lak = jnp.where(strict, M[:C, C:], 0.0)
        Trb = jnp.where(incl, M[C:, :C], 0.0)
        Trk = jnp.where(incl, M[C:, C:], 0.0)
        T = eye + Lab
        Lp = Lab
        for _ in range(5):
            Lp = _mm(Lp, Lp)
            T = T + _mm(T, Lp)
        X2 = _mm(Lak, vh)
        W = _mm(T, jnp.concatenate([At, X2], axis=1))
        S0 = state_ref[h]
        U = _mm_nt(W[:, :N], S0) + W[:, N:]
        UV = jnp.concatenate([U, vh], axis=0)
        O = _mm_nt(Rt, S0) + _mm(jnp.concatenate([Trb, Trk], axis=1), UV)
        state_ref[h] = (S0 + _mm_tn(UV, BK)) * gam[C - 1:C, sl]

        mean = jnp.mean(O, axis=-1, keepdims=True)
        cen = O - mean
        var = jnp.mean(cen * cen, axis=-1, keepdims=True)
        on = cen * lax.rsqrt(var + GN_EPS) * lnw_ref[:, sl] + lnb_ref[:, sl]
        bonus = jnp.sum(rh * kh * rkw_ref[:, sl], axis=-1, keepdims=True)
        outs.append((on + bonus * vh) * g[:, sl])
    o_ref[0] = jnp.concatenate(outs, axis=-1)


def _rwkv_mixer(pr, mu, w0, w_up, a0, a_up, g_up, k_k, k_a, r_k, ln_w, ln_b, bsz, seq):
    C = CHUNK
    row = lambda n: pl.BlockSpec((1, n), lambda b, i: (0, 0))
    full = lambda m, n: pl.BlockSpec((m, n), lambda b, i: (0, 0))
    return pl.pallas_call(
        _rwkv_body,
        name="rwkv_mixer",
        grid=(bsz, seq // C),
        in_specs=[pl.BlockSpec((1, C, COLS_RWKV), lambda b, i: (b, i, 0)),
                  row(COLS_RWKV), row(D_RWKV), full(DECAY_LORA, D_RWKV), row(D_RWKV),
                  full(AAA_LORA, D_RWKV), full(GATE_LORA, D_RWKV), row(D_RWKV), row(D_RWKV),
                  row(D_RWKV), row(D_RWKV), row(D_RWKV)],
        out_specs=pl.BlockSpec((1, C, D_RWKV), lambda b, i: (b, i, 0)),
        out_shape=jax.ShapeDtypeStruct((bsz, seq, D_RWKV), F32),
        scratch_shapes=[pltpu.VMEM((N_HEADS, HEAD_DIM, HEAD_DIM), F32),
                        pltpu.VMEM((SUBLANES, COLS_RWKV), F32)],
        compiler_params=_cparams("parallel", "arbitrary"),
    )(pr.reshape(bsz, seq, COLS_RWKV), mu.reshape(1, -1), w0.reshape(1, -1), w_up,
      a0.reshape(1, -1), a_up, g_up, k_k.reshape(1, -1), k_a.reshape(1, -1), r_k.reshape(1, -1),
      ln_w.reshape(1, -1), ln_b.reshape(1, -1))


def _outproj_body(yc_ref, yr_ref, x_ref, wo_ref, gt_ref, g2_ref, sh_ref, sc_ref, rwh_ref, rwl_ref,
                  rb_ref, h_ref, u_ref, gate_ref, idx_ref, cnt_ref):
    @pl.when(pl.program_id(0) == 0)
    def _():
        cnt_ref[...] = jnp.zeros_like(cnt_ref)

    mix = _mm(yc_ref[...], wo_ref[:D_CONV, :]) + _mm(yr_ref[...], wo_ref[D_CONV:, :])
    h = x_ref[...] + gt_ref[0, 0] * mix
    h_ref[...] = h
    u2 = _rms(h) * g2_ref[...]
    u2 = u2 * (1.0 + sc_ref[0, 0]) + sh_ref[0, 0]
    u_ref[...] = u2

    uh, ul = _split2(u2)
    logits = (jnp.dot(uh, rwh_ref[...], preferred_element_type=F32)
              + jnp.dot(uh, rwl_ref[...], preferred_element_type=F32)
              + jnp.dot(ul, rwh_ref[...], preferred_element_type=F32)) + rb_ref[...]
    tm = logits.shape[0]
    lane = lax.broadcasted_iota(I32, (tm, LANES), 1)
    neg = jnp.float32(-jnp.inf)
    l = jnp.where(lane < N_EXPERTS, logits, neg)
    tops, idxs = [], []
    for _ in range(TOP_K):
        m = jnp.max(l, axis=-1, keepdims=True)
        ix = jnp.min(jnp.where(l == m, lane, LANES), axis=-1, keepdims=True)
        tops.append(m)
        idxs.append(ix)
        l = jnp.where(lane == ix, neg, l)
    es = [jnp.exp(t - tops[0]) for t in tops]
    den = es[0] + es[1] + es[2] + es[3]
    gate_tile = jnp.zeros((tm, LANES), F32)
    idx_tile = jnp.zeros((tm, LANES), I32)
    sel = jnp.zeros((tm, LANES), F32)
    for kq in range(TOP_K):
        gate_tile = jnp.where(lane == kq, es[kq] / den, gate_tile)
        idx_tile = jnp.where(lane == kq, idxs[kq], idx_tile)
        sel = sel + jnp.where(lane == idxs[kq], 1.0, 0.0)
    gate_ref[...] = gate_tile
    idx_ref[...] = idx_tile
    part = sel[0:SUBLANES]
    for q in range(1, tm // SUBLANES):
        part = part + sel[q * SUBLANES:(q + 1) * SUBLANES]
    cnt_ref[...] += part


def _out_proj_router(yc, yr, xf, w_out_bf, mod6, g2, rw_hi, rw_lo, rb, tiles_per_batch):
    n_tok = xf.shape[0]
    tm = TOKEN_TILE
    tok = lambda n: pl.BlockSpec((tm, n), lambda i: (i, 0))
    modspec = lambda which: pl.BlockSpec((1, 1, 1, D_MODEL),
                                         lambda i: (i // tiles_per_batch, which, 0, 0))
    const = lambda m, n: pl.BlockSpec((m, n), lambda i: (0, 0))
    return pl.pallas_call(
        _outproj_body,
        name="out_proj_router",
        grid=(n_tok // tm,),
        in_specs=[tok(D_CONV), tok(D_RWKV), tok(D_MODEL), const(D_MODEL, D_MODEL),
                  modspec(2), const(1, D_MODEL), modspec(3), modspec(4),
                  const(D_MODEL, LANES), const(D_MODEL, LANES), const(1, LANES)],
        out_specs=[tok(D_MODEL), tok(D_MODEL), tok(LANES), tok(LANES), const(SUBLANES, LANES)],
        out_shape=[jax.ShapeDtypeStruct((n_tok, D_MODEL), F32),
                   jax.ShapeDtypeStruct((n_tok, D_MODEL), F32),
                   jax.ShapeDtypeStruct((n_tok, LANES), F32),
                   jax.ShapeDtypeStruct((n_tok, LANES), I32),
                   jax.ShapeDtypeStruct((SUBLANES, LANES), F32)],
        compiler_params=_cparams("arbitrary"),
    )(yc, yr, xf, w_out_bf, mod6, g2, mod6, mod6, rw_hi, rw_lo, rb)


def _assign_body(idx_ref, base_ref, dest_ref, carry_ref, tri_ref):
    tm = idx_ref.shape[0]

    @pl.when(pl.program_id(0) == 0)
    def _():
        carry_ref[...] = jnp.zeros_like(carry_ref)
        rr = lax.broadcasted_iota(I32, (tm, tm), 0)
        cc = lax.broadcasted_iota(I32, (tm, tm), 1)
        tri_ref[...] = jnp.where(rr > cc, 1.0, 0.0).astype(BF16)

    idx = idx_ref[...]
    lane = lax.broadcasted_iota(I32, (tm, LANES), 1)
    masks = [lane == idx[:, kq:kq + 1] for kq in range(TOP_K)]
    sel = jnp.zeros((tm, LANES), F32)
    for mk in masks:
        sel = sel + jnp.where(mk, 1.0, 0.0)
    prefix = jnp.dot(tri_ref[...], sel.astype(BF16), preferred_element_type=F32)
    first_row = base_ref[0:1, :] + carry_ref[0:1, :] + prefix
    dest_tile = jnp.zeros((tm, LANES), F32)
    for kq in range(TOP_K):
        dk = jnp.sum(jnp.where(masks[kq], first_row, 0.0), axis=-1, keepdims=True)
        dest_tile = jnp.where(lane == kq, dk, dest_tile)
    dest_ref[...] = jnp.transpose(dest_tile)[0:SUBLANES, :].astype(I32)
    carry_ref[...] += jnp.broadcast_to(jnp.sum(sel, axis=0, keepdims=True), carry_ref.shape)


def _assign_rows(idx_tile, base):
    n_tok = idx_tile.shape[0]
    tm = TOKEN_TILE
    return pl.pallas_call(
        _assign_body,
        name="assign_rows",
        grid=(n_tok // tm,),
        in_specs=[pl.BlockSpec((tm, LANES), lambda i: (i, 0)),
                  pl.BlockSpec((SUBLANES, LANES), lambda i: (0, 0))],
        out_specs=pl.BlockSpec((SUBLANES, tm), lambda i: (0, i)),
        out_shape=jax.ShapeDtypeStruct((SUBLANES, n_tok), I32),
        scratch_shapes=[pltpu.VMEM((SUBLANES, LANES), F32), pltpu.VMEM((tm, tm), BF16)],
        compiler_params=_cparams("arbitrary"),
    )(idx_tile, base)


def _dispatch_body(zrow_ref, u_ref, dest_ref, xs_ref, dsm_ref, zbuf_ref, sem_s, sem_z, sem_r):
    tm = u_ref.shape[0]

    @pl.when(pl.program_id(0) == 0)
    def _():
        zbuf_ref[...] = jnp.zeros_like(zbuf_ref)
        for phase in range(2):
            for e in range(2 * N_EXPERTS):
                zr = zrow_ref[e]

                @pl.when(zr >= 0)
                def _():
                    cp = pltpu.make_async_copy(
                        zbuf_ref, xs_ref.at[pl.ds(pl.multiple_of(zr, ROW_BLOCK), ROW_BLOCK)], sem_z)
                    if phase == 0:
                        cp.start()
                    else:
                        cp.wait()

    cp = pltpu.make_async_copy(dest_ref, dsm_ref, sem_s)
    cp.start()
    cp.wait()

    def issue(t, carry):
        for kq in range(TOP_K):
            d = dsm_ref[kq, t]
            pltpu.make_async_copy(u_ref.at[pl.ds(t, 1)], xs_ref.at[pl.ds(d, 1)], sem_r).start()
        return carry

    lax.fori_loop(0, tm, issue, 0)
    for kq in range(TOP_K):
        pltpu.make_async_copy(u_ref, xs_ref.at[pl.ds(0, tm)], sem_r).wait()


def _dispatch(u2, dest, zrow, m_pad):
    n_tok = u2.shape[0]
    tm = MOVE_TILE
    grid_spec = pltpu.PrefetchScalarGridSpec(
        num_scalar_prefetch=1,
        grid=(n_tok // tm,),
        in_specs=[pl.BlockSpec((tm, D_MODEL), lambda i, z: (i, 0)),
                  pl.BlockSpec((SUBLANES, tm), lambda i, z: (0, i))],
        out_specs=pl.BlockSpec(memory_space=pl.ANY),
        scratch_shapes=[pltpu.SMEM((SUBLANES, tm), I32),
                        pltpu.VMEM((ROW_BLOCK, D_MODEL), F32),
                        pltpu.SemaphoreType.DMA, pltpu.SemaphoreType.DMA, pltpu.SemaphoreType.DMA],
    )
    return pl.pallas_call(
        _dispatch_body,
        name="dispatch_rows",
        grid_spec=grid_spec,
        out_shape=jax.ShapeDtypeStruct((m_pad, D_MODEL), F32),
        compiler_params=_cparams("arbitrary"),
    )(zrow, u2, dest)


def _expert_body(be_ref, nu_ref, x_ref, wg_ref, wl_ref, w2_ref, bg_ref, bl_ref, b2_ref, y_ref):
    @pl.when(pl.program_id(0) >= nu_ref[0])
    def _():
        y_ref[...] = jnp.zeros_like(y_ref)

    @pl.when(pl.program_id(0) < nu_ref[0])
    def _():
        x = x_ref[...].astype(BF16)
        hg = jnp.dot(x, wg_ref[0], preferred_element_type=F32) + bg_ref[0]
        hl = jnp.dot(x, wl_ref[0], preferred_element_type=F32) + bl_ref[0]
        xg = jnp.minimum(hg, SWIGLU_LIMIT)
        xl = jnp.clip(hl, -SWIGLU_LIMIT, SWIGLU_LIMIT)
        act = xg * _sigmoid(SWIGLU_ALPHA * xg) * (xl + 1.0)
        y_ref[...] = jnp.dot(act.astype(BF16), w2_ref[0], preferred_element_type=F32) + b2_ref[0]


def _experts(xs, block_expert, n_used, w1g, w1l, w2, b1g, b1l, b2):
    m_pad = xs.shape[0]
    bm = ROW_BLOCK
    rows = lambda i, be, nu: (jnp.minimum(i, nu[0] - 1), 0)
    wsel = lambda i, be, nu: (be[i], 0, 0)
    grid_spec = pltpu.PrefetchScalarGridSpec(
        num_scalar_prefetch=2,
        grid=(m_pad // bm,),
        in_specs=[pl.BlockSpec((bm, D_MODEL), rows),
                  pl.BlockSpec((1, D_MODEL, D_EXPERT), wsel),
                  pl.BlockSpec((1, D_MODEL, D_EXPERT), wsel),
                  pl.BlockSpec((1, D_EXPERT, D_MODEL), wsel),
                  pl.BlockSpec((1, 1, D_EXPERT), wsel),
                  pl.BlockSpec((1, 1, D_EXPERT), wsel),
                  pl.BlockSpec((1, 1, D_MODEL), wsel)],
        out_specs=pl.BlockSpec((bm, D_MODEL), lambda i, be, nu: (i, 0)),
    )
    return pl.pallas_call(
        _expert_body,
        name="expert_mlp",
        grid_spec=grid_spec,
        out_shape=jax.ShapeDtypeStruct((m_pad, D_MODEL), F32),
        compiler_params=_cparams("arbitrary"),
    )(block_expert, n_used, xs, w1g, w1l, w2, b1g, b1l, b2)


def _combine_body(y_ref, dest_ref, gate_ref, h_ref, gt_ref, fg_ref, o_ref, dsm_ref, buf_ref,
                  sem_s, sem_r):
    tm = h_ref.shape[0]
    cp = pltpu.make_async_copy(dest_ref, dsm_ref, sem_s)
    cp.start()
    cp.wait()

    def issue(t, carry):
        for kq in range(TOP_K):
            d = dsm_ref[kq, t]
            pltpu.make_async_copy(y_ref.at[pl.ds(d, 1)], buf_ref.at[kq, pl.ds(t, 1)], sem_r).start()
        return carry

    lax.fori_loop(0, tm, issue, 0)
    for kq in range(TOP_K):
        pltpu.make_async_copy(y_ref.at[pl.ds(0, tm)], buf_ref.at[kq], sem_r).wait()

    gate = gate_ref[...]
    moe = gate[:, 0:1] * buf_ref[0]
    for kq in range(1, TOP_K):
        moe = moe + gate[:, kq:kq + 1] * buf_ref[kq]
    h = h_ref[...] + gt_ref[0, 0] * moe
    o_ref[...] = _rms(h) * fg_ref[...]


def _combine(y_rows, dest, gate_tile, h, mod6, final_g, tiles_per_batch):
    n_tok = h.shape[0]
    tm = MOVE_TILE
    return pl.pallas_call(
        _combine_body,
        name="combine_rows",
        grid=(n_tok // tm,),
        in_specs=[pl.BlockSpec(memory_space=pl.ANY),
                  pl.BlockSpec((SUBLANES, tm), lambda i: (0, i)),
                  pl.BlockSpec((tm, LANES), lambda i: (i, 0)),
                  pl.BlockSpec((tm, D_MODEL), lambda i: (i, 0)),
                  pl.BlockSpec((1, 1, 1, D_MODEL), lambda i: (i // tiles_per_batch, 5, 0, 0)),
                  pl.BlockSpec((1, D_MODEL), lambda i: (0, 0))],
        out_specs=pl.BlockSpec((tm, D_MODEL), lambda i: (i, 0)),
        out_shape=jax.ShapeDtypeStruct((n_tok, D_MODEL), F32),
        scratch_shapes=[pltpu.SMEM((SUBLANES, tm), I32),
                        pltpu.VMEM((TOP_K, tm, D_MODEL), F32),
                        pltpu.SemaphoreType.DMA, pltpu.SemaphoreType.DMA],
        compiler_params=_cparams("arbitrary"),
    )(y_rows, dest, gate_tile, h, mod6, final_g)


def kernel(x, c, ada_w, ada_b, norm1_g, w_in, conv_w, conv_gn, rwkv_mu, rwkv_w0, rwkv_w_up,
           rwkv_a0, rwkv_a_up, rwkv_g_up, rwkv_k_k, rwkv_k_a, rwkv_r_k, rwkv_ln_w, rwkv_ln_b,
           w_out, norm2_g, router_w, router_b, exp_w1, exp_b1, exp_w2, exp_b2, final_g):
    bsz, seq, _ = x.shape
    depth = ada_w.shape[0]
    assert depth == 1, "the final norm is fused into the last layer's combine step"
    n_tok = bsz * seq
    assert seq % TOKEN_TILE == 0 and seq % CHUNK == 0 and n_tok % MOVE_TILE == 0
    n_rows = n_tok * TOP_K
    assert n_rows % ROW_BLOCK == 0
    n_blocks = n_rows // ROW_BLOCK + N_EXPERTS
    m_pad = n_blocks * ROW_BLOCK

    h = x.reshape(n_tok, D_MODEL)
    out = h
    for l in range(depth):
        mod6 = _ada_mod(c, ada_w[l], ada_b[l]).reshape(bsz, 6, 1, D_MODEL)
        pc, pr = _in_proj(h, norm1_g[l].reshape(1, -1), mod6, w_in[l].astype(BF16),
                          seq // TOKEN_TILE)
        yc = _conv_mixer(pc, conv_w[l], conv_gn[l].reshape(1, -1), bsz, seq)
        yr = _rwkv_mixer(pr, rwkv_mu[l], rwkv_w0[l], rwkv_w_up[l], rwkv_a0[l], rwkv_a_up[l],
                         rwkv_g_up[l], rwkv_k_k[l], rwkv_k_a[l], rwkv_r_k[l], rwkv_ln_w[l],
                         rwkv_ln_b[l], bsz, seq)
        rw = jnp.pad(router_w[l], ((0, 0), (0, LANES - N_EXPERTS)))
        rw_hi = rw.astype(BF16)
        rw_lo = (rw - rw_hi.astype(F32)).astype(BF16)
        rb = jnp.pad(router_b[l], (0, LANES - N_EXPERTS)).reshape(1, LANES)
        h, u2, gate_tile, idx_tile, cnt = _out_proj_router(
            yc.reshape(n_tok, D_CONV), yr.reshape(n_tok, D_RWKV), h, w_out[l].astype(BF16), mod6,
            norm2_g[l].reshape(1, -1), rw_hi, rw_lo, rb, seq // TOKEN_TILE)
        counts = jnp.sum(cnt, axis=0)[:N_EXPERTS].astype(I32)
        padded = (counts + ROW_BLOCK - 1) // ROW_BLOCK * ROW_BLOCK
        pad_end = jnp.cumsum(padded)
        pad_start = pad_end - padded
        n_used = (pad_end[-1] // ROW_BLOCK).astype(I32).reshape(1)
        blk = jnp.minimum(jnp.arange(n_blocks, dtype=I32), n_used[0] - 1) * ROW_BLOCK
        block_expert = jnp.minimum(
            jnp.sum((pad_end[None, :] <= blk[:, None]).astype(I32), axis=1), N_EXPERTS - 1)
        tail = (n_used[0] + jnp.arange(N_EXPERTS, dtype=I32)) * ROW_BLOCK
        zrow = jnp.concatenate([jnp.where(padded > 0, pad_end - ROW_BLOCK, -1),
                                jnp.where(tail < m_pad, tail, -1)]).astype(I32)
        base = jnp.broadcast_to(
            jnp.pad(pad_start.astype(F32), (0, LANES - N_EXPERTS))[None, :], (SUBLANES, LANES))
        dest = _assign_rows(idx_tile, base)
        xs = _dispatch(u2, dest, zrow, m_pad)
        w1 = exp_w1[l]
        y_rows = _experts(xs, block_expert, n_used,
                          w1[:, :, 0::2].astype(BF16), w1[:, :, 1::2].astype(BF16),
                          exp_w2[l].astype(BF16),
                          exp_b1[l][:, None, 0::2], exp_b1[l][:, None, 1::2], exp_b2[l][:, None, :])
        out = _combine(y_rows, dest, gate_tile, h, mod6, final_g.reshape(1, -1), seq // MOVE_TILE)
    return out.reshape(bsz, seq, D_MODEL)
```

```python
import functools

import jax
import jax.numpy as jnp
from jax import lax
from jax.experimental import pallas as pl
from jax.experimental.pallas import tpu as pltpu

F32 = jnp.float32
BF16 = jnp.bfloat16
I32 = jnp.int32

D_MODEL = 1024
D_CONV = 512
CONV_GROUP_DIM = 64
D_RWKV = 512
HEAD_DIM = 64
N_HEADS = D_RWKV // HEAD_DIM
DECAY_LORA = 64
AAA_LORA = 64
GATE_LORA = 128
COLS_CONV = 3 * D_CONV
COLS_RWKV = 3 * D_RWKV + DECAY_LORA + AAA_LORA + GATE_LORA
N_EXPERTS = 32
TOP_K = 4
D_EXPERT = D_MODEL
SWIGLU_LIMIT = 7.0
SWIGLU_ALPHA = 1.702
NORM_EPS = 1e-5
GN_EPS = 64e-5

LANES = 128
SUBLANES = 8
MXU_TILE = 256
CHUNK = 64
RWKV_SEQS_PER_STEP = 2
ROW_BLOCK = 256
TOKEN_TILE = 512
MOVE_TILE = 256
VMEM_LIMIT = 48 * 1024 * 1024


def _cparams(*sem):
    return pltpu.CompilerParams(dimension_semantics=sem, vmem_limit_bytes=VMEM_LIMIT)


def _mm(a, b):
    return jnp.dot(a.astype(BF16), b.astype(BF16), preferred_element_type=F32)


def _mm_nt(a, b):
    return lax.dot_general(a.astype(BF16), b.astype(BF16), (((1,), (1,)), ((), ())),
                           preferred_element_type=F32)


def _mm_tn(a, b):
    return lax.dot_general(a.astype(BF16), b.astype(BF16), (((0,), (0,)), ((), ())),
                           preferred_element_type=F32)


def _split2(x):
    hi = x.astype(BF16)
    lo = (x - hi.astype(F32)).astype(BF16)
    return hi, lo


def _split3(x):
    hi = x.astype(BF16)
    r = x - hi.astype(F32)
    mid = r.astype(BF16)
    lo = (r - mid.astype(F32)).astype(BF16)
    return hi, mid, lo


def _sigmoid(x):
    return 1.0 / (1.0 + jnp.exp(-x))


def _rms(x):
    return x * lax.rsqrt(jnp.mean(x * x, axis=-1, keepdims=True) + NORM_EPS)


def _ada_body(c_ref, w_ref, b_ref, o_ref):
    c = c_ref[...]
    ca = c * _sigmoid(c)
    ah, al = _split2(ca)
    wh, wl = _split2(w_ref[...])
    acc = jnp.dot(ah, wh, preferred_element_type=F32)
    acc += jnp.dot(ah, wl, preferred_element_type=F32)
    acc += jnp.dot(al, wh, preferred_element_type=F32)
    o_ref[...] = acc + b_ref[...]


def _ada_mod(c, ada_w, ada_b):
    bsz = c.shape[0]
    n = ada_w.shape[1]
    tn = 1024
    return pl.pallas_call(
        _ada_body,
        name="ada_mod",
        grid=(n // tn,),
        in_specs=[pl.BlockSpec((bsz, D_MODEL), lambda j: (0, 0)),
                  pl.BlockSpec((D_MODEL, tn), lambda j: (0, j)),
                  pl.BlockSpec((1, tn), lambda j: (0, j))],
        out_specs=pl.BlockSpec((bsz, tn), lambda j: (0, j)),
        out_shape=jax.ShapeDtypeStruct((bsz, n), F32),
        compiler_params=_cparams("parallel"),
    )(c, ada_w, ada_b.reshape(1, n))


def _inproj_body(x_ref, g_ref, sh_ref, sc_ref, w_ref, oc_ref, or_ref):
    u = _rms(x_ref[...]) * g_ref[...]
    u = u * (1.0 + sc_ref[0, 0]) + sh_ref[0, 0]
    acc = jnp.dot(u.astype(BF16), w_ref[...], preferred_element_type=F32)
    oc_ref[...] = acc[:, :COLS_CONV]
    or_ref[...] = acc[:, COLS_CONV:]


def _in_proj(xf, g1, mod6, w_in_bf, tiles_per_batch):
    n_tok = xf.shape[0]
    tm = TOKEN_TILE
    n_in = w_in_bf.shape[1]
    return pl.pallas_call(
        _inproj_body,
        name="in_proj",
        grid=(n_tok // tm,),
        in_specs=[pl.BlockSpec((tm, D_MODEL), lambda i: (i, 0)),
                  pl.BlockSpec((1, D_MODEL), lambda i: (0, 0)),
                  pl.BlockSpec((1, 1, 1, D_MODEL), lambda i: (i // tiles_per_batch, 0, 0, 0)),
                  pl.BlockSpec((1, 1, 1, D_MODEL), lambda i: (i // tiles_per_batch, 1, 0, 0)),
                  pl.BlockSpec((D_MODEL, n_in), lambda i: (0, 0))],
        out_specs=[pl.BlockSpec((tm, COLS_CONV), lambda i: (i, 0)),
                   pl.BlockSpec((tm, COLS_RWKV), lambda i: (i, 0))],
        out_shape=[jax.ShapeDtypeStruct((n_tok, COLS_CONV), F32),
                   jax.ShapeDtypeStruct((n_tok, COLS_RWKV), F32)],
        compiler_params=_cparams("parallel"),
    )(xf, g1, mod6, mod6, w_in_bf)


def _conv_body(p_ref, cw_ref, gn_ref, o_ref, carry_ref):
    @pl.when(pl.program_id(1) == 0)
    def _():
        carry_ref[...] = jnp.zeros_like(carry_ref)

    p = p_ref[0]
    tc = p.shape[0]
    b_gate = p[:, :D_CONV]
    u = p[:, D_CONV:2 * D_CONV] * p[:, 2 * D_CONV:]
    ext = jnp.concatenate([carry_ref[...], u], axis=0)
    u1 = pltpu.roll(ext, 1, axis=0)[SUBLANES:]
    u2 = pltpu.roll(ext, 2, axis=0)[SUBLANES:]
    carry_ref[...] = u[tc - SUBLANES:, :]
    cw = cw_ref[...]
    y = b_gate * (cw[0:1] * u2 + cw[1:2] * u1 + cw[2:3] * u)
    outs = []
    for g in range(D_CONV // CONV_GROUP_DIM):
        outs.append(_rms(y[:, g * CONV_GROUP_DIM:(g + 1) * CONV_GROUP_DIM]))
    o_ref[0] = jnp.concatenate(outs, axis=-1) * gn_ref[...]


def _conv_mixer(pc, conv_w, conv_gn, bsz, seq):
    tc = TOKEN_TILE
    return pl.pallas_call(
        _conv_body,
        name="conv_mixer",
        grid=(bsz, seq // tc),
        in_specs=[pl.BlockSpec((1, tc, COLS_CONV), lambda b, i: (b, i, 0)),
                  pl.BlockSpec((3, D_CONV), lambda b, i: (0, 0)),
                  pl.BlockSpec((1, D_CONV), lambda b, i: (0, 0))],
        out_specs=pl.BlockSpec((1, tc, D_CONV), lambda b, i: (b, i, 0)),
        out_shape=jax.ShapeDtypeStruct((bsz, seq, D_CONV), F32),
        scratch_shapes=[pltpu.VMEM((SUBLANES, D_CONV), F32)],
        compiler_params=_cparams("parallel", "arbitrary"),
    )(pc.reshape(bsz, seq, COLS_CONV), conv_w, conv_gn)


def _rwkv_body(p_ref, mu_ref, w0_ref, wup_ref, a0_ref, aup_ref, gup_ref, kkw_ref, kaw_ref,
               rkw_ref, lnw_ref, lnb_ref, o_ref, state_ref, prev_ref):
    @pl.when(pl.program_id(1) == 0)
    def _():
        state_ref[...] = jnp.zeros_like(state_ref)
        prev_ref[...] = jnp.zeros_like(prev_ref)

    C = CHUNK
    N = HEAD_DIM
    nb = p_ref.shape[0]
    s1, s2, s3 = D_RWKV, 2 * D_RWKV, 3 * D_RWKV
    row1 = lax.broadcasted_iota(I32, (C, 1), 0)
    ri = lax.broadcasted_iota(I32, (C, C), 0)
    ci = lax.broadcasted_iota(I32, (C, C), 1)
    strict = ri > ci
    incl = ri >= ci
    eye = jnp.where(ri == ci, 1.0, 0.0).astype(F32)
    tri = jnp.where(incl, 1.0, 0.0).astype(BF16)

    units = []
    for bb in range(nb):
        P = p_ref[bb]
        Pprev = jnp.where(row1 == 0, prev_ref[bb, 0:1, :], pltpu.roll(P, 1, axis=0))
        prev_ref[bb] = jnp.broadcast_to(P[C - 1:C, :], prev_ref.shape[1:])
        p = P + (Pprev - P) * mu_ref[...]
        r = p[:, :s1]
        k = p[:, s1:s2]
        v = p[:, s2:s3]
        dw = p[:, s3:s3 + DECAY_LORA]
        da = p[:, s3 + DECAY_LORA:s3 + DECAY_LORA + AAA_LORA]
        dg = p[:, s3 + DECAY_LORA + AAA_LORA:]
        z = -(w0_ref[...] + _mm(jnp.tanh(dw), wup_ref[...]))
        softplus = jnp.maximum(z, 0.0) + jnp.log(1.0 + jnp.exp(-jnp.abs(z)))
        logw = -jnp.exp(-softplus - 0.5)
        a = _sigmoid(a0_ref[...] + _mm(da, aup_ref[...]))
        g = _mm(_sigmoid(dg), gup_ref[...])
        kk = k * kkw_ref[...]
        kmod = k * (1.0 + (a - 1.0) * kaw_ref[...])
        cum = sum(jnp.dot(tri, part, preferred_element_type=F32) for part in _split3(logw))
        gam = jnp.exp(cum)
        gam_prev = jnp.exp(cum - logw)
        ginv = jnp.exp(-cum)
        for h in range(N_HEADS):
            sl = slice(h * N, (h + 1) * N)
            kkh = kk[:, sl]
            kkh = kkh / jnp.maximum(jnp.sqrt(jnp.sum(kkh * kkh, axis=-1, keepdims=True)), 1e-12)
            rh, kh, vh = r[:, sl], kmod[:, sl], v[:, sl]
            At = (-kkh * gam_prev[:, sl]).astype(BF16)
            Rt = (rh * gam[:, sl]).astype(BF16)
            BK = jnp.concatenate([kkh * a[:, sl] * ginv[:, sl], kh * ginv[:, sl]],
                                 axis=0).astype(BF16)
            units.append(dict(
                slot=bb * N_HEADS + h, At=At, Rt=Rt, BK=BK, v=vh, vb=vh.astype(BF16),
                AR=jnp.concatenate([At, Rt], axis=0), g=g[:, sl], glast=gam[C - 1:C, sl],
                bonus=jnp.sum(rh * kh * rkw_ref[:, sl], axis=-1, keepdims=True),
                lnw=lnw_ref[:, sl], lnb=lnb_ref[:, sl]))

    for u in units:
        M = _mm_nt(u["AR"], u["BK"])
        lab = jnp.where(strict, M[:C, :C], 0.0)
        u["T"] = eye + lab
        u["Lp"] = lab.astype(BF16)
        u["Lak"] = jnp.where(strict, M[:C, C:], 0.0).astype(BF16)
        u["Tr"] = jnp.concatenate([jnp.where(incl, M[C:, :C], 0.0),
                                   jnp.where(incl, M[C:, C:], 0.0)], axis=1).astype(BF16)
    for u in units:
        u["X2"] = _mm(u["Lak"], u["vb"])
        u["Lp"] = _mm(u["Lp"], u["Lp"]).astype(BF16)
    for _ in range(4):
        for u in units:
            u["T"] = u["T"] + _mm(u["T"], u["Lp"])
            u["Lp"] = _mm(u["Lp"], u["Lp"]).astype(BF16)
    for u in units:
        u["T"] = u["T"] + _mm(u["T"], u["Lp"])
    for u in units:
        u["W"] = _mm(u["T"], jnp.concatenate([u["At"], u["X2"].astype(BF16)], axis=1))
    for u in units:
        u["S0"] = state_ref[u["slot"]]
        u["S0b"] = u["S0"].astype(BF16)
        U = _mm_nt(u["W"][:, :N], u["S0b"]) + u["W"][:, N:]
        u["UV"] = jnp.concatenate([U.astype(BF16), u["vb"]], axis=0)
    for u in units:
        u["O"] = _mm_nt(u["Rt"], u["S0b"]) + _mm(u["Tr"], u["UV"])
        state_ref[u["slot"]] = (u["S0"] + _mm_tn(u["UV"], u["BK"])) * u["glast"]

    for bb in range(nb):
        outs = []
        for u in units[bb * N_HEADS:(bb + 1) * N_HEADS]:
            O = u["O"]
            mean = jnp.mean(O, axis=-1, keepdims=True)
            cen = O - mean
            var = jnp.mean(cen * cen, axis=-1, keepdims=True)
            on = cen * lax.rsqrt(var + GN_EPS) * u["lnw"] + u["lnb"]
            outs.append((on + u["bonus"] * u["v"]) * u["g"])
        o_ref[bb] = jnp.concatenate(outs, axis=-1)


def _rwkv_mixer(pr, mu, w0, w_up, a0, a_up, g_up, k_k, k_a, r_k, ln_w, ln_b, bsz, seq):
    C = CHUNK
    nb = RWKV_SEQS_PER_STEP if bsz % RWKV_SEQS_PER_STEP == 0 else 1
    row = lambda n: pl.BlockSpec((1, n), lambda b, i: (0, 0))
    full = lambda m, n: pl.BlockSpec((m, n), lambda b, i: (0, 0))
    return pl.pallas_call(
        _rwkv_body,
        name="rwkv_mixer",
        grid=(bsz // nb, seq // C),
        in_specs=[pl.BlockSpec((nb, C, COLS_RWKV), lambda b, i: (b, i, 0)),
                  row(COLS_RWKV), row(D_RWKV), full(DECAY_LORA, D_RWKV), row(D_RWKV),
                  full(AAA_LORA, D_RWKV), full(GATE_LORA, D_RWKV), row(D_RWKV), row(D_RWKV),
                  row(D_RWKV), row(D_RWKV), row(D_RWKV)],
        out_specs=pl.BlockSpec((nb, C, D_RWKV), lambda b, i: (b, i, 0)),
        out_shape=jax.ShapeDtypeStruct((bsz, seq, D_RWKV), F32),
        scratch_shapes=[pltpu.VMEM((nb * N_HEADS, HEAD_DIM, HEAD_DIM), F32),
                        pltpu.VMEM((nb, SUBLANES, COLS_RWKV), F32)],
        compiler_params=_cparams("parallel", "arbitrary"),
    )(pr.reshape(bsz, seq, COLS_RWKV), mu.reshape(1, -1), w0.reshape(1, -1), w_up,
      a0.reshape(1, -1), a_up, g_up, k_k.reshape(1, -1), k_a.reshape(1, -1), r_k.reshape(1, -1),
      ln_w.reshape(1, -1), ln_b.reshape(1, -1))


def _outproj_body(yc_ref, yr_ref, x_ref, wo_ref, gt_ref, g2_ref, sh_ref, sc_ref, rwh_ref, rwl_ref,
                  rb_ref, h_ref, u_ref, gate_ref, idx_ref, cnt_ref):
    @pl.when(pl.program_id(0) == 0)
    def _():
        cnt_ref[...] = jnp.zeros_like(cnt_ref)

    mix = _mm(yc_ref[...], wo_ref[:D_CONV, :]) + _mm(yr_ref[...], wo_ref[D_CONV:, :])
    h = x_ref[...] + gt_ref[0, 0] * mix
    h_ref[...] = h
    u2 = _rms(h) * g2_ref[...]
    u2 = u2 * (1.0 + sc_ref[0, 0]) + sh_ref[0, 0]
    u_ref[...] = u2

    uh, ul = _split2(u2)
    logits = (jnp.dot(uh, rwh_ref[...], preferred_element_type=F32)
              + jnp.dot(uh, rwl_ref[...], preferred_element_type=F32)
              + jnp.dot(ul, rwh_ref[...], preferred_element_type=F32)) + rb_ref[...]
    tm = logits.shape[0]
    lane = lax.broadcasted_iota(I32, (tm, LANES), 1)
    neg = jnp.float32(-jnp.inf)
    l = jnp.where(lane < N_EXPERTS, logits, neg)
    tops, idxs = [], []
    for _ in range(TOP_K):
        m = jnp.max(l, axis=-1, keepdims=True)
        ix = jnp.min(jnp.where(l == m, lane, LANES), axis=-1, keepdims=True)
        tops.append(m)
        idxs.append(ix)
        l = jnp.where(lane == ix, neg, l)
    es = [jnp.exp(t - tops[0]) for t in tops]
    den = es[0] + es[1] + es[2] + es[3]
    gate_tile = jnp.zeros((tm, LANES), F32)
    idx_tile = jnp.zeros((tm, LANES), I32)
    sel = jnp.zeros((tm, LANES), F32)
    for kq in range(TOP_K):
        gate_tile = jnp.where(lane == kq, es[kq] / den, gate_tile)
        idx_tile = jnp.where(lane == kq, idxs[kq], idx_tile)
        sel = sel + jnp.where(lane == idxs[kq], 1.0, 0.0)
    gate_ref[...] = gate_tile
    idx_ref[...] = idx_tile
    part = sel[0:SUBLANES]
    for q in range(1, tm // SUBLANES):
        part = part + sel[q * SUBLANES:(q + 1) * SUBLANES]
    cnt_ref[...] += part


def _out_proj_router(yc, yr, xf, w_out_bf, mod6, g2, rw_hi, rw_lo, rb, tiles_per_batch):
    n_tok = xf.shape[0]
    tm = TOKEN_TILE
    tok = lambda n: pl.BlockSpec((tm, n), lambda i: (i, 0))
    modspec = lambda which: pl.BlockSpec((1, 1, 1, D_MODEL),
                                         lambda i: (i // tiles_per_batch, which, 0, 0))
    const = lambda m, n: pl.BlockSpec((m, n), lambda i: (0, 0))
    return pl.pallas_call(
        _outproj_body,
        name="out_proj_router",
        grid=(n_tok // tm,),
        in_specs=[tok(D_CONV), tok(D_RWKV), tok(D_MODEL), const(D_MODEL, D_MODEL),
                  modspec(2), const(1, D_MODEL), modspec(3), modspec(4),
                  const(D_MODEL, LANES), const(D_MODEL, LANES), const(1, LANES)],
        out_specs=[tok(D_MODEL), tok(D_MODEL), tok(LANES), tok(LANES), const(SUBLANES, LANES)],
        out_shape=[jax.ShapeDtypeStruct((n_tok, D_MODEL), F32),
                   jax.ShapeDtypeStruct((n_tok, D_MODEL), F32),
                   jax.ShapeDtypeStruct((n_tok, LANES), F32),
                   jax.ShapeDtypeStruct((n_tok, LANES), I32),
                   jax.ShapeDtypeStruct((SUBLANES, LANES), F32)],
        compiler_params=_cparams("arbitrary"),
    )(yc, yr, xf, w_out_bf, mod6, g2, mod6, mod6, rw_hi, rw_lo, rb)


def _assign_body(idx_ref, base_ref, dest_ref, carry_ref, tri_ref):
    tm = idx_ref.shape[0]

    @pl.when(pl.program_id(0) == 0)
    def _():
        carry_ref[...] = jnp.zeros_like(carry_ref)
        rr = lax.broadcasted_iota(I32, (tm, tm), 0)
        cc = lax.broadcasted_iota(I32, (tm, tm), 1)
        tri_ref[...] = jnp.where(rr > cc, 1.0, 0.0).astype(BF16)

    idx = idx_ref[...]
    lane = lax.broadcasted_iota(I32, (tm, LANES), 1)
    masks = [lane == idx[:, kq:kq + 1] for kq in range(TOP_K)]
    sel = jnp.zeros((tm, LANES), F32)
    for mk in masks:
        sel = sel + jnp.where(mk, 1.0, 0.0)
    prefix = jnp.dot(tri_ref[...], sel.astype(BF16), preferred_element_type=F32)
    first_row = base_ref[0:1, :] + carry_ref[0:1, :] + prefix
    dest_tile = jnp.zeros((tm, LANES), F32)
    for kq in range(TOP_K):
        dk = jnp.sum(jnp.where(masks[kq], first_row, 0.0), axis=-1, keepdims=True)
        dest_tile = jnp.where(lane == kq, dk, dest_tile)
    dest_ref[...] = jnp.transpose(dest_tile)[0:SUBLANES, :].astype(I32)
    carry_ref[...] += jnp.broadcast_to(jnp.sum(sel, axis=0, keepdims=True), carry_ref.shape)


def _assign_rows(idx_tile, base):
    n_tok = idx_tile.shape[0]
    tm = TOKEN_TILE
    return pl.pallas_call(
        _assign_body,
        name="assign_rows",
        grid=(n_tok // tm,),
        in_specs=[pl.BlockSpec((tm, LANES), lambda i: (i, 0)),
                  pl.BlockSpec((SUBLANES, LANES), lambda i: (0, 0))],
        out_specs=pl.BlockSpec((SUBLANES, tm), lambda i: (0, i)),
        out_shape=jax.ShapeDtypeStruct((SUBLANES, n_tok), I32),
        scratch_shapes=[pltpu.VMEM((SUBLANES, LANES), F32), pltpu.VMEM((tm, tm), BF16)],
        compiler_params=_cparams("arbitrary"),
    )(idx_tile, base)


def _dispatch_body(zrow_ref, u_ref, dest_ref, xs_ref, dsm_ref, zbuf_ref, sem_s, sem_z, sem_r):
    tm = u_ref.shape[0]

    @pl.when(pl.program_id(0) == 0)
    def _():
        zbuf_ref[...] = jnp.zeros_like(zbuf_ref)
        for phase in range(2):
            for e in range(2 * N_EXPERTS):
                zr = zrow_ref[e]

                @pl.when(zr >= 0)
                def _():
                    cp = pltpu.make_async_copy(
                        zbuf_ref, xs_ref.at[pl.ds(pl.multiple_of(zr, ROW_BLOCK), ROW_BLOCK)], sem_z)
                    if phase == 0:
                        cp.start()
                    else:
                        cp.wait()

    cp = pltpu.make_async_copy(dest_ref, dsm_ref, sem_s)
    cp.start()
    cp.wait()

    def issue(t, carry):
        for kq in range(TOP_K):
            d = dsm_ref[kq, t]
            pltpu.make_async_copy(u_ref.at[pl.ds(t, 1)], xs_ref.at[pl.ds(d, 1)], sem_r).start()
        return carry

    lax.fori_loop(0, tm, issue, 0)
    for kq in range(TOP_K):
        pltpu.make_async_copy(u_ref, xs_ref.at[pl.ds(0, tm)], sem_r).wait()


def _dispatch(u2, dest, zrow, m_pad):
    n_tok = u2.shape[0]
    tm = MOVE_TILE
    grid_spec = pltpu.PrefetchScalarGridSpec(
        num_scalar_prefetch=1,
        grid=(n_tok // tm,),
        in_specs=[pl.BlockSpec((tm, D_MODEL), lambda i, z: (i, 0)),
                  pl.BlockSpec((SUBLANES, tm), lambda i, z: (0, i))],
        out_specs=pl.BlockSpec(memory_space=pl.ANY),
        scratch_shapes=[pltpu.SMEM((SUBLANES, tm), I32),
                        pltpu.VMEM((ROW_BLOCK, D_MODEL), F32),
                        pltpu.SemaphoreType.DMA, pltpu.SemaphoreType.DMA, pltpu.SemaphoreType.DMA],
    )
    return pl.pallas_call(
        _dispatch_body,
        name="dispatch_rows",
        grid_spec=grid_spec,
        out_shape=jax.ShapeDtypeStruct((m_pad, D_MODEL), F32),
        compiler_params=_cparams("arbitrary"),
    )(zrow, u2, dest)


def _deinterleave_body(w_ref, o_ref):
    w = w_ref[0].astype(BF16)
    half = o_ref.shape[2] // 2
    src = lax.broadcasted_iota(I32, (2 * MXU_TILE, MXU_TILE), 0)
    dst = lax.broadcasted_iota(I32, (2 * MXU_TILE, MXU_TILE), 1)
    pick_even = jnp.where(src == 2 * dst, 1.0, 0.0).astype(BF16)
    pick_odd = jnp.where(src == 2 * dst + 1, 1.0, 0.0).astype(BF16)
    for j in range(half // MXU_TILE):
        slab = w[:, 2 * MXU_TILE * j:2 * MXU_TILE * (j + 1)]
        lo = MXU_TILE * j
        o_ref[0, :, lo:lo + MXU_TILE] = jnp.dot(
            slab, pick_even, preferred_element_type=F32).astype(BF16)
        o_ref[0, :, half + lo:half + lo + MXU_TILE] = jnp.dot(
            slab, pick_odd, preferred_element_type=F32).astype(BF16)


def _deinterleave_w1(w1):
    n_e, d_in, two_f = w1.shape
    tr = 512
    return pl.pallas_call(
        _deinterleave_body,
        name="deinterleave_w1",
        grid=(n_e, d_in // tr),
        in_specs=[pl.BlockSpec((1, tr, two_f), lambda e, r: (e, r, 0))],
        out_specs=pl.BlockSpec((1, tr, two_f), lambda e, r: (e, r, 0)),
        out_shape=jax.ShapeDtypeStruct((n_e, d_in, two_f), BF16),
        compiler_params=_cparams("parallel", "parallel"),
    )(w1)


def _expert_body(be_ref, nu_ref, x_ref, w1_ref, w2_ref, bg_ref, bl_ref, b2_ref, y_ref):
    @pl.when(pl.program_id(0) >= nu_ref[0])
    def _():
        y_ref[...] = jnp.zeros_like(y_ref)

    @pl.when(pl.program_id(0) < nu_ref[0])
    def _():
        x = x_ref[...].astype(BF16)
        hb = jnp.dot(x, w1_ref[0], preferred_element_type=F32)
        hg = hb[:, :D_EXPERT] + bg_ref[0]
        hl = hb[:, D_EXPERT:] + bl_ref[0]
        xg = jnp.minimum(hg, SWIGLU_LIMIT)
        xl = jnp.clip(hl, -SWIGLU_LIMIT, SWIGLU_LIMIT)
        act = xg * _sigmoid(SWIGLU_ALPHA * xg) * (xl + 1.0)
        y_ref[...] = jnp.dot(act.astype(BF16), w2_ref[0], preferred_element_type=F32) + b2_ref[0]


def _experts(xs, block_expert, n_used, w1d, w2, b1g, b1l, b2):
    m_pad = xs.shape[0]
    bm = ROW_BLOCK
    rows = lambda i, be, nu: (jnp.minimum(i, nu[0] - 1), 0)
    wsel = lambda i, be, nu: (be[i], 0, 0)
    grid_spec = pltpu.PrefetchScalarGridSpec(
        num_scalar_prefetch=2,
        grid=(m_pad // bm,),
        in_specs=[pl.BlockSpec((bm, D_MODEL), rows),
                  pl.BlockSpec((1, D_MODEL, 2 * D_EXPERT), wsel),
                  pl.BlockSpec((1, D_EXPERT, D_MODEL), wsel),
                  pl.BlockSpec((1, 1, D_EXPERT), wsel),
                  pl.BlockSpec((1, 1, D_EXPERT), wsel),
                  pl.BlockSpec((1, 1, D_MODEL), wsel)],
        out_specs=pl.BlockSpec((bm, D_MODEL), lambda i, be, nu: (i, 0)),
    )
    return pl.pallas_call(
        _expert_body,
        name="expert_mlp",
        grid_spec=grid_spec,
        out_shape=jax.ShapeDtypeStruct((m_pad, D_MODEL), F32),
        compiler_params=_cparams("arbitrary"),
    )(block_expert, n_used, xs, w1d, w2, b1g, b1l, b2)


def _combine_body(y_ref, dest_ref, gate_ref, h_ref, gt_ref, fg_ref, o_ref, dsm_ref, buf_ref,
                  sem_s, sem_r):
    tm = h_ref.shape[0]
    cp = pltpu.make_async_copy(dest_ref, dsm_ref, sem_s)
    cp.start()
    cp.wait()

    def issue(t, carry):
        for kq in range(TOP_K):
            d = dsm_ref[kq, t]
            pltpu.make_async_copy(y_ref.at[pl.ds(d, 1)], buf_ref.at[kq, pl.ds(t, 1)], sem_r).start()
        return carry

    lax.fori_loop(0, tm, issue, 0)
    for kq in range(TOP_K):
        pltpu.make_async_copy(y_ref.at[pl.ds(0, tm)], buf_ref.at[kq], sem_r).wait()

    gate = gate_ref[...]
    moe = gate[:, 0:1] * buf_ref[0]
    for kq in range(1, TOP_K):
        moe = moe + gate[:, kq:kq + 1] * buf_ref[kq]
    h = h_ref[...] + gt_ref[0, 0] * moe
    o_ref[...] = _rms(h) * fg_ref[...]


def _combine(y_rows, dest, gate_tile, h, mod6, final_g, tiles_per_batch):
    n_tok = h.shape[0]
    tm = MOVE_TILE
    return pl.pallas_call(
        _combine_body,
        name="combine_rows",
        grid=(n_tok // tm,),
        in_specs=[pl.BlockSpec(memory_space=pl.ANY),
                  pl.BlockSpec((SUBLANES, tm), lambda i: (0, i)),
                  pl.BlockSpec((tm, LANES), lambda i: (i, 0)),
                  pl.BlockSpec((tm, D_MODEL), lambda i: (i, 0)),
                  pl.BlockSpec((1, 1, 1, D_MODEL), lambda i: (i // tiles_per_batch, 5, 0, 0)),
                  pl.BlockSpec((1, D_MODEL), lambda i: (0, 0))],
        out_specs=pl.BlockSpec((tm, D_MODEL), lambda i: (i, 0)),
        out_shape=jax.ShapeDtypeStruct((n_tok, D_MODEL), F32),
        scratch_shapes=[pltpu.SMEM((SUBLANES, tm), I32),
                        pltpu.VMEM((TOP_K, tm, D_MODEL), F32),
                        pltpu.SemaphoreType.DMA, pltpu.SemaphoreType.DMA],
        compiler_params=_cparams("arbitrary"),
    )(y_rows, dest, gate_tile, h, mod6, final_g)


def kernel(x, c, ada_w, ada_b, norm1_g, w_in, conv_w, conv_gn, rwkv_mu, rwkv_w0, rwkv_w_up,
           rwkv_a0, rwkv_a_up, rwkv_g_up, rwkv_k_k, rwkv_k_a, rwkv_r_k, rwkv_ln_w, rwkv_ln_b,
           w_out, norm2_g, router_w, router_b, exp_w1, exp_b1, exp_w2, exp_b2, final_g):
    bsz, seq, _ = x.shape
    depth = ada_w.shape[0]
    assert depth == 1, "the final norm is fused into the last layer's combine step"
    n_tok = bsz * seq
    assert seq % TOKEN_TILE == 0 and seq % CHUNK == 0 and n_tok % MOVE_TILE == 0
    n_rows = n_tok * TOP_K
    assert n_rows % ROW_BLOCK == 0
    n_blocks = n_rows // ROW_BLOCK + N_EXPERTS
    m_pad = n_blocks * ROW_BLOCK

    h = x.reshape(n_tok, D_MODEL)
    out = h
    for l in range(depth):
        mod6 = _ada_mod(c, ada_w[l], ada_b[l]).reshape(bsz, 6, 1, D_MODEL)
        pc, pr = _in_proj(h, norm1_g[l].reshape(1, -1), mod6, w_in[l].astype(BF16),
                          seq // TOKEN_TILE)
        yc = _conv_mixer(pc, conv_w[l], conv_gn[l].reshape(1, -1), bsz, seq)
        yr = _rwkv_mixer(pr, rwkv_mu[l], rwkv_w0[l], rwkv_w_up[l], rwkv_a0[l], rwkv_a_up[l],
                         rwkv_g_up[l], rwkv_k_k[l], rwkv_k_a[l], rwkv_r_k[l], rwkv_ln_w[l],
                         rwkv_ln_b[l], bsz, seq)
        rw = jnp.pad(router_w[l], ((0, 0), (0, LANES - N_EXPERTS)))
        rw_hi = rw.astype(BF16)
        rw_lo = (rw - rw_hi.astype(F32)).astype(BF16)
        rb = jnp.pad(router_b[l], (0, LANES - N_EXPERTS)).reshape(1, LANES)
        h, u2, gate_tile, idx_tile, cnt = _out_proj_router(
            yc.reshape(n_tok, D_CONV), yr.reshape(n_tok, D_RWKV), h, w_out[l].astype(BF16), mod6,
            norm2_g[l].reshape(1, -1), rw_hi, rw_lo, rb, seq // TOKEN_TILE)
        counts = jnp.sum(cnt, axis=0)[:N_EXPERTS].astype(I32)
        padded = (counts + ROW_BLOCK - 1) // ROW_BLOCK * ROW_BLOCK
        pad_end = jnp.cumsum(padded)
        pad_start = pad_end - padded
        n_used = (pad_end[-1] // ROW_BLOCK).astype(I32).reshape(1)
        blk = jnp.minimum(jnp.arange(n_blocks, dtype=I32), n_used[0] - 1) * ROW_BLOCK
        block_expert = jnp.minimum(
            jnp.sum((pad_end[None, :] <= blk[:, None]).astype(I32), axis=1), N_EXPERTS - 1)
        tail = (n_used[0] + jnp.arange(N_EXPERTS, dtype=I32)) * ROW_BLOCK
        zrow = jnp.concatenate([jnp.where(padded > 0, pad_end - ROW_BLOCK, -1),
                                jnp.where(tail < m_pad, tail, -1)]).astype(I32)
        base = jnp.broadcast_to(
            jnp.pad(pad_start.astype(F32), (0, LANES - N_EXPERTS))[None, :], (SUBLANES, LANES))
        dest = _assign_rows(idx_tile, base)
        xs = _dispatch(u2, dest, zrow, m_pad)
        y_rows = _experts(xs, block_expert, n_used, _deinterleave_w1(exp_w1[l]),
                          exp_w2[l].astype(BF16),
                          exp_b1[l][:, None, 0::2], exp_b1[l][:, None, 1::2], exp_b2[l][:, None, :])
        out = _combine(y_rows, dest, gate_tile, h, mod6, final_g.reshape(1, -1), seq // MOVE_TILE)
    return out.reshape(bsz, seq, D_MODEL)
```

```python
import functools

import jax
import jax.numpy as jnp
from jax import lax
from jax.experimental import pallas as pl
from jax.experimental.pallas import tpu as pltpu

F32 = jnp.float32
BF16 = jnp.bfloat16
I32 = jnp.int32

D_MODEL = 1024
D_CONV = 512
CONV_GROUP_DIM = 64
D_RWKV = 512
HEAD_DIM = 64
N_HEADS = D_RWKV // HEAD_DIM
DECAY_LORA = 64
AAA_LORA = 64
GATE_LORA = 128
COLS_CONV = 3 * D_CONV
COLS_RWKV = 3 * D_RWKV + DECAY_LORA + AAA_LORA + GATE_LORA
N_EXPERTS = 32
TOP_K = 4
D_EXPERT = D_MODEL
SWIGLU_LIMIT = 7.0
SWIGLU_ALPHA = 1.702
NORM_EPS = 1e-5
GN_EPS = 64e-5

LANES = 128
SUBLANES = 8
MXU_TILE = 256
CHUNK = 64
RWKV_SEQS_PER_STEP = 2
ROW_BLOCK = 256
TOKEN_TILE = 512
MOVE_TILE = 256
COMBINE_TILE = 256
VMEM_LIMIT = 48 * 1024 * 1024


def _cparams(*sem):
    return pltpu.CompilerParams(dimension_semantics=sem, vmem_limit_bytes=VMEM_LIMIT)


def _mm(a, b):
    return jnp.dot(a.astype(BF16), b.astype(BF16), preferred_element_type=F32)


def _mm_nt(a, b):
    return lax.dot_general(a.astype(BF16), b.astype(BF16), (((1,), (1,)), ((), ())),
                           preferred_element_type=F32)


def _mm_tn(a, b):
    return lax.dot_general(a.astype(BF16), b.astype(BF16), (((0,), (0,)), ((), ())),
                           preferred_element_type=F32)


def _split2(x):
    hi = x.astype(BF16)
    lo = (x - hi.astype(F32)).astype(BF16)
    return hi, lo


def _split3(x):
    hi = x.astype(BF16)
    r = x - hi.astype(F32)
    mid = r.astype(BF16)
    lo = (r - mid.astype(F32)).astype(BF16)
    return hi, mid, lo


def _sigmoid(x):
    return 1.0 / (1.0 + jnp.exp(-x))


def _rms(x):
    return x * lax.rsqrt(jnp.mean(x * x, axis=-1, keepdims=True) + NORM_EPS)


def _ada_body(c_ref, w_ref, b_ref, o_ref):
    c = c_ref[...]
    ca = c * _sigmoid(c)
    ah, al = _split2(ca)
    wh, wl = _split2(w_ref[...])
    acc = jnp.dot(ah, wh, preferred_element_type=F32)
    acc += jnp.dot(ah, wl, preferred_element_type=F32)
    acc += jnp.dot(al, wh, preferred_element_type=F32)
    o_ref[...] = acc + b_ref[...]


def _ada_mod(c, ada_w, ada_b):
    bsz = c.shape[0]
    n = ada_w.shape[1]
    tn = 1024
    return pl.pallas_call(
        _ada_body,
        name="ada_mod",
        grid=(n // tn,),
        in_specs=[pl.BlockSpec((bsz, D_MODEL), lambda j: (0, 0)),
                  pl.BlockSpec((D_MODEL, tn), lambda j: (0, j)),
                  pl.BlockSpec((1, tn), lambda j: (0, j))],
        out_specs=pl.BlockSpec((bsz, tn), lambda j: (0, j)),
        out_shape=jax.ShapeDtypeStruct((bsz, n), F32),
        compiler_params=_cparams("parallel"),
    )(c, ada_w, ada_b.reshape(1, n))


def _inproj_body(x_ref, g_ref, sh_ref, sc_ref, w_ref, oc_ref, or_ref):
    u = _rms(x_ref[...]) * g_ref[...]
    u = u * (1.0 + sc_ref[0, 0]) + sh_ref[0, 0]
    acc = jnp.dot(u.astype(BF16), w_ref[...], preferred_element_type=F32)
    oc_ref[...] = acc[:, :COLS_CONV]
    or_ref[...] = acc[:, COLS_CONV:]


def _in_proj(xf, g1, mod6, w_in_bf, tiles_per_batch):
    n_tok = xf.shape[0]
    tm = TOKEN_TILE
    n_in = w_in_bf.shape[1]
    return pl.pallas_call(
        _inproj_body,
        name="in_proj",
        grid=(n_tok // tm,),
        in_specs=[pl.BlockSpec((tm, D_MODEL), lambda i: (i, 0)),
                  pl.BlockSpec((1, D_MODEL), lambda i: (0, 0)),
                  pl.BlockSpec((1, 1, 1, D_MODEL), lambda i: (i // tiles_per_batch, 0, 0, 0)),
                  pl.BlockSpec((1, 1, 1, D_MODEL), lambda i: (i // tiles_per_batch, 1, 0, 0)),
                  pl.BlockSpec((D_MODEL, n_in), lambda i: (0, 0))],
        out_specs=[pl.BlockSpec((tm, COLS_CONV), lambda i: (i, 0)),
                   pl.BlockSpec((tm, COLS_RWKV), lambda i: (i, 0))],
        out_shape=[jax.ShapeDtypeStruct((n_tok, COLS_CONV), F32),
                   jax.ShapeDtypeStruct((n_tok, COLS_RWKV), F32)],
        compiler_params=_cparams("parallel"),
    )(xf, g1, mod6, mod6, w_in_bf)


def _conv_body(p_ref, cw_ref, gn_ref, o_ref, carry_ref):
    @pl.when(pl.program_id(1) == 0)
    def _():
        carry_ref[...] = jnp.zeros_like(carry_ref)

    p = p_ref[0]
    tc = p.shape[0]
    b_gate = p[:, :D_CONV]
    u = p[:, D_CONV:2 * D_CONV] * p[:, 2 * D_CONV:]
    ext = jnp.concatenate([carry_ref[...], u], axis=0)
    u1 = pltpu.roll(ext, 1, axis=0)[SUBLANES:]
    u2 = pltpu.roll(ext, 2, axis=0)[SUBLANES:]
    carry_ref[...] = u[tc - SUBLANES:, :]
    cw = cw_ref[...]
    y = b_gate * (cw[0:1] * u2 + cw[1:2] * u1 + cw[2:3] * u)
    outs = []
    for g in range(D_CONV // CONV_GROUP_DIM):
        outs.append(_rms(y[:, g * CONV_GROUP_DIM:(g + 1) * CONV_GROUP_DIM]))
    o_ref[0] = jnp.concatenate(outs, axis=-1) * gn_ref[...]


def _conv_mixer(pc, conv_w, conv_gn, bsz, seq):
    tc = TOKEN_TILE
    return pl.pallas_call(
        _conv_body,
        name="conv_mixer",
        grid=(bsz, seq // tc),
        in_specs=[pl.BlockSpec((1, tc, COLS_CONV), lambda b, i: (b, i, 0)),
                  pl.BlockSpec((3, D_CONV), lambda b, i: (0, 0)),
                  pl.BlockSpec((1, D_CONV), lambda b, i: (0, 0))],
        out_specs=pl.BlockSpec((1, tc, D_CONV), lambda b, i: (b, i, 0)),
        out_shape=jax.ShapeDtypeStruct((bsz, seq, D_CONV), F32),
        scratch_shapes=[pltpu.VMEM((SUBLANES, D_CONV), F32)],
        compiler_params=_cparams("parallel", "arbitrary"),
    )(pc.reshape(bsz, seq, COLS_CONV), conv_w, conv_gn)


def _rwkv_body(p_ref, mu_ref, w0_ref, wup_ref, a0_ref, aup_ref, gup_ref, kkw_ref, kaw_ref,
               rkw_ref, lnw_ref, lnb_ref, o_ref, state_ref, prev_ref):
    @pl.when(pl.program_id(1) == 0)
    def _():
        state_ref[...] = jnp.zeros_like(state_ref)
        prev_ref[...] = jnp.zeros_like(prev_ref)

    C = CHUNK
    N = HEAD_DIM
    nb = p_ref.shape[0]
    s1, s2, s3 = D_RWKV, 2 * D_RWKV, 3 * D_RWKV
    row1 = lax.broadcasted_iota(I32, (C, 1), 0)
    ri = lax.broadcasted_iota(I32, (C, C), 0)
    ci = lax.broadcasted_iota(I32, (C, C), 1)
    strict = ri > ci
    incl = ri >= ci
    eye = jnp.where(ri == ci, 1.0, 0.0).astype(F32)
    tri = jnp.where(incl, 1.0, 0.0).astype(BF16)

    units = []
    for bb in range(nb):
        P = p_ref[bb]
        Pprev = jnp.where(row1 == 0, prev_ref[bb, 0:1, :], pltpu.roll(P, 1, axis=0))
        prev_ref[bb] = jnp.broadcast_to(P[C - 1:C, :], prev_ref.shape[1:])
        p = P + (Pprev - P) * mu_ref[...]
        r = p[:, :s1]
        k = p[:, s1:s2]
        v = p[:, s2:s3]
        dw = p[:, s3:s3 + DECAY_LORA]
        da = p[:, s3 + DECAY_LORA:s3 + DECAY_LORA + AAA_LORA]
        dg = p[:, s3 + DECAY_LORA + AAA_LORA:]
        z = -(w0_ref[...] + _mm(jnp.tanh(dw), wup_ref[...]))
        softplus = jnp.maximum(z, 0.0) + jnp.log(1.0 + jnp.exp(-jnp.abs(z)))
        logw = -jnp.exp(-softplus - 0.5)
        a = _sigmoid(a0_ref[...] + _mm(da, aup_ref[...]))
        g = _mm(_sigmoid(dg), gup_ref[...])
        kk = k * kkw_ref[...]
        kmod = k * (1.0 + (a - 1.0) * kaw_ref[...])
        cum = sum(jnp.dot(tri, part, preferred_element_type=F32) for part in _split3(logw))
        gam = jnp.exp(cum)
        gam_prev = jnp.exp(cum - logw)
        ginv = jnp.exp(-cum)
        for h in range(N_HEADS):
            sl = slice(h * N, (h + 1) * N)
            kkh = kk[:, sl]
            kkh = kkh / jnp.maximum(jnp.sqrt(jnp.sum(kkh * kkh, axis=-1, keepdims=True)), 1e-12)
            rh, kh, vh = r[:, sl], kmod[:, sl], v[:, sl]
            At = (-kkh * gam_prev[:, sl]).astype(BF16)
            Rt = (rh * gam[:, sl]).astype(BF16)
            BK = jnp.concatenate([kkh * a[:, sl] * ginv[:, sl], kh * ginv[:, sl]],
                                 axis=0).astype(BF16)
            units.append(dict(
                slot=bb * N_HEADS + h, At=At, Rt=Rt, BK=BK, v=vh, vb=vh.astype(BF16),
                AR=jnp.concatenate([At, Rt], axis=0), g=g[:, sl], glast=gam[C - 1:C, sl],
                bonus=jnp.sum(rh * kh * rkw_ref[:, sl], axis=-1, keepdims=True),
                lnw=lnw_ref[:, sl], lnb=lnb_ref[:, sl]))

    for u in units:
        M = _mm_nt(u["AR"], u["BK"])
        lab = jnp.where(strict, M[:C, :C], 0.0)
        u["T"] = eye + lab
        u["Lp"] = lab.astype(BF16)
        u["Lak"] = jnp.where(strict, M[:C, C:], 0.0).astype(BF16)
        u["Tr"] = jnp.concatenate([jnp.where(incl, M[C:, :C], 0.0),
                                   jnp.where(incl, M[C:, C:], 0.0)], axis=1).astype(BF16)
    for u in units:
        u["X2"] = _mm(u["Lak"], u["vb"])
        u["Lp"] = _mm(u["Lp"], u["Lp"]).astype(BF16)
    for _ in range(4):
        for u in units:
            u["T"] = u["T"] + _mm(u["T"], u["Lp"])
            u["Lp"] = _mm(u["Lp"], u["Lp"]).astype(BF16)
    for u in units:
        u["T"] = u["T"] + _mm(u["T"], u["Lp"])
    for u in units:
        u["W"] = _mm(u["T"], jnp.concatenate([u["At"], u["X2"].astype(BF16)], axis=1))
    for u in units:
        u["S0"] = state_ref[u["slot"]]
        u["S0b"] = u["S0"].astype(BF16)
        U = _mm_nt(u["W"][:, :N], u["S0b"]) + u["W"][:, N:]
        u["UV"] = jnp.concatenate([U.astype(BF16), u["vb"]], axis=0)
    for u in units:
        u["O"] = _mm_nt(u["Rt"], u["S0b"]) + _mm(u["Tr"], u["UV"])
        state_ref[u["slot"]] = (u["S0"] + _mm_tn(u["UV"], u["BK"])) * u["glast"]

    for bb in range(nb):
        outs = []
        for u in units[bb * N_HEADS:(bb + 1) * N_HEADS]:
            O = u["O"]
            mean = jnp.mean(O, axis=-1, keepdims=True)
            cen = O - mean
            var = jnp.mean(cen * cen, axis=-1, keepdims=True)
            on = cen * lax.rsqrt(var + GN_EPS) * u["lnw"] + u["lnb"]
            outs.append((on + u["bonus"] * u["v"]) * u["g"])
        o_ref[bb] = jnp.concatenate(outs, axis=-1)


def _rwkv_mixer(pr, mu, w0, w_up, a0, a_up, g_up, k_k, k_a, r_k, ln_w, ln_b, bsz, seq):
    C = CHUNK
    nb = RWKV_SEQS_PER_STEP if bsz % RWKV_SEQS_PER_STEP == 0 else 1
    row = lambda n: pl.BlockSpec((1, n), lambda b, i: (0, 0))
    full = lambda m, n: pl.BlockSpec((m, n), lambda b, i: (0, 0))
    return pl.pallas_call(
        _rwkv_body,
        name="rwkv_mixer",
        grid=(bsz // nb, seq // C),
        in_specs=[pl.BlockSpec((nb, C, COLS_RWKV), lambda b, i: (b, i, 0)),
                  row(COLS_RWKV), row(D_RWKV), full(DECAY_LORA, D_RWKV), row(D_RWKV),
                  full(AAA_LORA, D_RWKV), full(GATE_LORA, D_RWKV), row(D_RWKV), row(D_RWKV),
                  row(D_RWKV), row(D_RWKV), row(D_RWKV)],
        out_specs=pl.BlockSpec((nb, C, D_RWKV), lambda b, i: (b, i, 0)),
        out_shape=jax.ShapeDtypeStruct((bsz, seq, D_RWKV), F32),
        scratch_shapes=[pltpu.VMEM((nb * N_HEADS, HEAD_DIM, HEAD_DIM), F32),
                        pltpu.VMEM((nb, SUBLANES, COLS_RWKV), F32)],
        compiler_params=_cparams("parallel", "arbitrary"),
    )(pr.reshape(bsz, seq, COLS_RWKV), mu.reshape(1, -1), w0.reshape(1, -1), w_up,
      a0.reshape(1, -1), a_up, g_up, k_k.reshape(1, -1), k_a.reshape(1, -1), r_k.reshape(1, -1),
      ln_w.reshape(1, -1), ln_b.reshape(1, -1))


def _outproj_body(yc_ref, yr_ref, x_ref, wo_ref, gt_ref, g2_ref, sh_ref, sc_ref, rwh_ref, rwl_ref,
                  rb_ref, h_ref, u_ref, gate_ref, idx_ref, cnt_ref):
    @pl.when(pl.program_id(0) == 0)
    def _():
        cnt_ref[...] = jnp.zeros_like(cnt_ref)

    mix = _mm(yc_ref[...], wo_ref[:D_CONV, :]) + _mm(yr_ref[...], wo_ref[D_CONV:, :])
    h = x_ref[...] + gt_ref[0, 0] * mix
    h_ref[...] = h
    u2 = _rms(h) * g2_ref[...]
    u2 = u2 * (1.0 + sc_ref[0, 0]) + sh_ref[0, 0]
    u_ref[...] = u2

    uh, ul = _split2(u2)
    logits = (jnp.dot(uh, rwh_ref[...], preferred_element_type=F32)
              + jnp.dot(uh, rwl_ref[...], preferred_element_type=F32)
              + jnp.dot(ul, rwh_ref[...], preferred_element_type=F32)) + rb_ref[...]
    tm = logits.shape[0]
    lane = lax.broadcasted_iota(I32, (tm, LANES), 1)
    neg = jnp.float32(-jnp.inf)
    l = jnp.where(lane < N_EXPERTS, logits, neg)
    tops, idxs = [], []
    for _ in range(TOP_K):
        m = jnp.max(l, axis=-1, keepdims=True)
        ix = jnp.min(jnp.where(l == m, lane, LANES), axis=-1, keepdims=True)
        tops.append(m)
        idxs.append(ix)
        l = jnp.where(lane == ix, neg, l)
    es = [jnp.exp(t - tops[0]) for t in tops]
    den = es[0] + es[1] + es[2] + es[3]
    gate_tile = jnp.zeros((tm, LANES), F32)
    idx_tile = jnp.zeros((tm, LANES), I32)
    sel = jnp.zeros((tm, LANES), F32)
    for kq in range(TOP_K):
        gate_tile = jnp.where(lane == kq, es[kq] / den, gate_tile)
        idx_tile = jnp.where(lane == kq, idxs[kq], idx_tile)
        sel = sel + jnp.where(lane == idxs[kq], 1.0, 0.0)
    gate_ref[...] = gate_tile
    idx_ref[...] = idx_tile
    part = sel[0:SUBLANES]
    for q in range(1, tm // SUBLANES):
        part = part + sel[q * SUBLANES:(q + 1) * SUBLANES]
    cnt_ref[...] += part


def _out_proj_router(yc, yr, xf, w_out_bf, mod6, g2, rw_hi, rw_lo, rb, tiles_per_batch):
    n_tok = xf.shape[0]
    tm = TOKEN_TILE
    tok = lambda n: pl.BlockSpec((tm, n), lambda i: (i, 0))
    modspec = lambda which: pl.BlockSpec((1, 1, 1, D_MODEL),
                                         lambda i: (i // tiles_per_batch, which, 0, 0))
    const = lambda m, n: pl.BlockSpec((m, n), lambda i: (0, 0))
    return pl.pallas_call(
        _outproj_body,
        name="out_proj_router",
        grid=(n_tok // tm,),
        in_specs=[tok(D_CONV), tok(D_RWKV), tok(D_MODEL), const(D_MODEL, D_MODEL),
                  modspec(2), const(1, D_MODEL), modspec(3), modspec(4),
                  const(D_MODEL, LANES), const(D_MODEL, LANES), const(1, LANES)],
        out_specs=[tok(D_MODEL), tok(D_MODEL), tok(LANES), tok(LANES), const(SUBLANES, LANES)],
        out_shape=[jax.ShapeDtypeStruct((n_tok, D_MODEL), F32),
                   jax.ShapeDtypeStruct((n_tok, D_MODEL), F32),
                   jax.ShapeDtypeStruct((n_tok, LANES), F32),
                   jax.ShapeDtypeStruct((n_tok, LANES), I32),
                   jax.ShapeDtypeStruct((SUBLANES, LANES), F32)],
        compiler_params=_cparams("arbitrary"),
    )(yc, yr, xf, w_out_bf, mod6, g2, mod6, mod6, rw_hi, rw_lo, rb)


def _assign_body(idx_ref, base_ref, dest_ref, lpos_ref, tab_ref, carry_ref, tri_ref):
    tm = idx_ref.shape[0]

    @pl.when(pl.program_id(0) == 0)
    def _():
        carry_ref[...] = jnp.zeros_like(carry_ref)
        rr = lax.broadcasted_iota(I32, (tm, tm), 0)
        cc = lax.broadcasted_iota(I32, (tm, tm), 1)
        tri_ref[...] = jnp.where(rr > cc, 1.0, 0.0).astype(BF16)

    idx = idx_ref[...]
    lane = lax.broadcasted_iota(I32, (tm, LANES), 1)
    masks = [lane == idx[:, kq:kq + 1] for kq in range(TOP_K)]
    sel = jnp.zeros((tm, LANES), F32)
    for mk in masks:
        sel = sel + jnp.where(mk, 1.0, 0.0)
    prefix = jnp.dot(tri_ref[...], sel.astype(BF16), preferred_element_type=F32)
    first_row = base_ref[0:1, :] + carry_ref[0:1, :] + prefix
    cnt = jnp.broadcast_to(jnp.sum(sel, axis=0, keepdims=True), carry_ref.shape)
    lane8 = lax.broadcasted_iota(I32, (SUBLANES, LANES), 1)
    inc = cnt
    for sh in (1, 2, 4, 8, 16, 32, 64):
        inc = inc + jnp.where(lane8 >= sh, pltpu.roll(inc, sh, axis=1), 0.0)
    loff = inc - cnt
    local_row = loff[0:1, :] + prefix
    dest_tile = jnp.zeros((tm, LANES), F32)
    lpos_tile = jnp.zeros((tm, LANES), F32)
    for kq in range(TOP_K):
        dk = jnp.sum(jnp.where(masks[kq], first_row, 0.0), axis=-1, keepdims=True)
        dest_tile = jnp.where(lane == kq, dk, dest_tile)
        lk = jnp.sum(jnp.where(masks[kq], local_row, 0.0), axis=-1, keepdims=True)
        lpos_tile = jnp.where(lane == kq, lk, lpos_tile)
    dest_ref[...] = jnp.transpose(dest_tile)[0:SUBLANES, :].astype(I32)
    lpos_ref[...] = lpos_tile
    row8 = lax.broadcasted_iota(I32, (SUBLANES, LANES), 0)
    tab_ref[0] = jnp.where(row8 == 0, cnt, jnp.where(row8 == 1, loff, carry_ref[...]))
    carry_ref[...] += cnt


def _assign_rows(idx_tile, base):
    n_tok = idx_tile.shape[0]
    tm = COMBINE_TILE
    return pl.pallas_call(
        _assign_body,
        name="assign_rows",
        grid=(n_tok // tm,),
        in_specs=[pl.BlockSpec((tm, LANES), lambda i: (i, 0)),
                  pl.BlockSpec((SUBLANES, LANES), lambda i: (0, 0))],
        out_specs=[pl.BlockSpec((SUBLANES, tm), lambda i: (0, i)),
                   pl.BlockSpec((tm, LANES), lambda i: (i, 0)),
                   pl.BlockSpec((1, SUBLANES, LANES), lambda i: (i, 0, 0))],
        out_shape=[jax.ShapeDtypeStruct((SUBLANES, n_tok), I32),
                   jax.ShapeDtypeStruct((n_tok, LANES), F32),
                   jax.ShapeDtypeStruct((n_tok // tm, SUBLANES, LANES), F32)],
        scratch_shapes=[pltpu.VMEM((SUBLANES, LANES), F32), pltpu.VMEM((tm, tm), BF16)],
        compiler_params=_cparams("arbitrary"),
    )(idx_tile, base)


def _dispatch_body(zrow_ref, u_ref, dest_ref, xs_ref, dsm_ref, zbuf_ref, sem_s, sem_z, sem_r):
    tm = u_ref.shape[0]

    @pl.when(pl.program_id(0) == 0)
    def _():
        zbuf_ref[...] = jnp.zeros_like(zbuf_ref)
        for phase in range(2):
            for e in range(2 * N_EXPERTS):
                zr = zrow_ref[e]

                @pl.when(zr >= 0)
                def _():
                    cp = pltpu.make_async_copy(
                        zbuf_ref, xs_ref.at[pl.ds(pl.multiple_of(zr, ROW_BLOCK), ROW_BLOCK)], sem_z)
                    if phase == 0:
                        cp.start()
                    else:
                        cp.wait()

    cp = pltpu.make_async_copy(dest_ref, dsm_ref, sem_s)
    cp.start()
    cp.wait()

    def issue(t, carry):
        for kq in range(TOP_K):
            d = dsm_ref[kq, t]
            pltpu.make_async_copy(u_ref.at[pl.ds(t, 1)], xs_ref.at[pl.ds(d, 1)], sem_r).start()
        return carry

    lax.fori_loop(0, tm, issue, 0)
    for kq in range(TOP_K):
        pltpu.make_async_copy(u_ref, xs_ref.at[pl.ds(0, tm)], sem_r).wait()


def _dispatch(u2, dest, zrow, m_pad):
    n_tok = u2.shape[0]
    tm = MOVE_TILE
    grid_spec = pltpu.PrefetchScalarGridSpec(
        num_scalar_prefetch=1,
        grid=(n_tok // tm,),
        in_specs=[pl.BlockSpec((tm, D_MODEL), lambda i, z: (i, 0)),
                  pl.BlockSpec((SUBLANES, tm), lambda i, z: (0, i))],
        out_specs=pl.BlockSpec(memory_space=pl.ANY),
        scratch_shapes=[pltpu.SMEM((SUBLANES, tm), I32),
                        pltpu.VMEM((ROW_BLOCK, D_MODEL), F32),
                        pltpu.SemaphoreType.DMA, pltpu.SemaphoreType.DMA, pltpu.SemaphoreType.DMA],
    )
    return pl.pallas_call(
        _dispatch_body,
        name="dispatch_rows",
        grid_spec=grid_spec,
        out_shape=jax.ShapeDtypeStruct((m_pad, D_MODEL), F32),
        compiler_params=_cparams("arbitrary"),
    )(zrow, u2, dest)


def _deinterleave_body(w_ref, o_ref):
    w = w_ref[0].astype(BF16)
    half = o_ref.shape[2] // 2
    src = lax.broadcasted_iota(I32, (2 * MXU_TILE, MXU_TILE), 0)
    dst = lax.broadcasted_iota(I32, (2 * MXU_TILE, MXU_TILE), 1)
    pick_even = jnp.where(src == 2 * dst, 1.0, 0.0).astype(BF16)
    pick_odd = jnp.where(src == 2 * dst + 1, 1.0, 0.0).astype(BF16)
    for j in range(half // MXU_TILE):
        slab = w[:, 2 * MXU_TILE * j:2 * MXU_TILE * (j + 1)]
        lo = MXU_TILE * j
        o_ref[0, :, lo:lo + MXU_TILE] = jnp.dot(
            slab, pick_even, preferred_element_type=F32).astype(BF16)
        o_ref[0, :, half + lo:half + lo + MXU_TILE] = jnp.dot(
            slab, pick_odd, preferred_element_type=F32).astype(BF16)


def _deinterleave_w1(w1):
    n_e, d_in, two_f = w1.shape
    tr = 512
    return pl.pallas_call(
        _deinterleave_body,
        name="deinterleave_w1",
        grid=(n_e, d_in // tr),
        in_specs=[pl.BlockSpec((1, tr, two_f), lambda e, r: (e, r, 0))],
        out_specs=pl.BlockSpec((1, tr, two_f), lambda e, r: (e, r, 0)),
        out_shape=jax.ShapeDtypeStruct((n_e, d_in, two_f), BF16),
        compiler_params=_cparams("parallel", "parallel"),
    )(w1)


def _expert_body(be_ref, nu_ref, x_ref, w1_ref, w2_ref, bg_ref, bl_ref, b2_ref, y_ref):
    @pl.when(pl.program_id(0) >= nu_ref[0])
    def _():
        y_ref[...] = jnp.zeros_like(y_ref)

    @pl.when(pl.program_id(0) < nu_ref[0])
    def _():
        x = x_ref[...].astype(BF16)
        hb = jnp.dot(x, w1_ref[0], preferred_element_type=F32)
        hg = hb[:, :D_EXPERT] + bg_ref[0]
        hl = hb[:, D_EXPERT:] + bl_ref[0]
        xg = jnp.minimum(hg, SWIGLU_LIMIT)
        xl = jnp.clip(hl, -SWIGLU_LIMIT, SWIGLU_LIMIT)
        act = xg * _sigmoid(SWIGLU_ALPHA * xg) * (xl + 1.0)
        y = jnp.dot(act.astype(BF16), w2_ref[0], preferred_element_type=F32) + b2_ref[0]
        n_seg = D_MODEL // LANES
        for j in range(n_seg):
            y_ref[pl.ds(j, y.shape[0], stride=n_seg), :] = y[:, j * LANES:(j + 1) * LANES]


def _experts(xs, block_expert, n_used, w1d, w2, b1g, b1l, b2):
    m_pad = xs.shape[0]
    bm = ROW_BLOCK
    rows = lambda i, be, nu: (jnp.minimum(i, nu[0] - 1), 0)
    wsel = lambda i, be, nu: (be[i], 0, 0)
    grid_spec = pltpu.PrefetchScalarGridSpec(
        num_scalar_prefetch=2,
        grid=(m_pad // bm,),
        in_specs=[pl.BlockSpec((bm, D_MODEL), rows),
                  pl.BlockSpec((1, D_MODEL, 2 * D_EXPERT), wsel),
                  pl.BlockSpec((1, D_EXPERT, D_MODEL), wsel),
                  pl.BlockSpec((1, 1, D_EXPERT), wsel),
                  pl.BlockSpec((1, 1, D_EXPERT), wsel),
                  pl.BlockSpec((1, 1, D_MODEL), wsel)],
        out_specs=pl.BlockSpec((bm * (D_MODEL // LANES), LANES), lambda i, be, nu: (i, 0)),
    )
    return pl.pallas_call(
        _expert_body,
        name="expert_mlp",
        grid_spec=grid_spec,
        out_shape=jax.ShapeDtypeStruct((m_pad * (D_MODEL // LANES), LANES), F32),
        compiler_params=_cparams("arbitrary"),
    )(block_expert, n_used, xs, w1d, w2, b1g, b1l, b2)


def _combine_body(rg_ref, rl_ref, rc_ref, y_ref, lpos_ref, gate_ref, h_ref, gt_ref, fg_ref, o_ref,
                  yloc_ref, sem):
    i = pl.program_id(0)
    tm = h_ref.shape[0]
    n_seg = D_MODEL // LANES
    n_loc = yloc_ref.shape[0] // n_seg
    for e in range(N_EXPERTS):
        c = rc_ref[i * N_EXPERTS + e] * n_seg

        @pl.when(c > 0)
        def _():
            src = pl.multiple_of(rg_ref[i * N_EXPERTS + e] * n_seg, n_seg)
            dst = pl.multiple_of(rl_ref[i * N_EXPERTS + e] * n_seg, n_seg)
            pltpu.make_async_copy(y_ref.at[pl.ds(src, c)], yloc_ref.at[pl.ds(dst, c)], sem).start()

    pltpu.make_async_copy(y_ref.at[pl.ds(0, n_loc * n_seg)], yloc_ref, sem).wait()

    y = jnp.concatenate([yloc_ref[pl.ds(j, n_loc, stride=n_seg), :] for j in range(n_seg)],
                        axis=1).astype(BF16)
    q = lax.broadcasted_iota(I32, (tm, n_loc), 1)
    lpos = lpos_ref[...].astype(I32)
    gate = gate_ref[...]
    gmat = jnp.zeros((tm, n_loc), F32)
    for kq in range(TOP_K):
        gmat = jnp.where(q == lpos[:, kq:kq + 1], gate[:, kq:kq + 1], gmat)
    ghi, glo = _split2(gmat)
    moe = (jnp.dot(ghi, y, preferred_element_type=F32)
           + jnp.dot(glo, y, preferred_element_type=F32))
    h = h_ref[...] + gt_ref[0, 0] * moe
    o_ref[...] = _rms(h) * fg_ref[...]


def _combine(y_rows, run_start, run_local, run_count, lpos, gate_tile, h, mod6, final_g,
             tiles_per_batch):
    n_tok = h.shape[0]
    tm = COMBINE_TILE
    grid_spec = pltpu.PrefetchScalarGridSpec(
        num_scalar_prefetch=3,
        grid=(n_tok // tm,),
        in_specs=[pl.BlockSpec(memory_space=pl.ANY),
                  pl.BlockSpec((tm, LANES), lambda i, *_: (i, 0)),
                  pl.BlockSpec((tm, LANES), lambda i, *_: (i, 0)),
                  pl.BlockSpec((tm, D_MODEL), lambda i, *_: (i, 0)),
                  pl.BlockSpec((1, 1, 1, D_MODEL), lambda i, *_: (i // tiles_per_batch, 5, 0, 0)),
                  pl.BlockSpec((1, D_MODEL), lambda i, *_: (0, 0))],
        out_specs=pl.BlockSpec((tm, D_MODEL), lambda i, *_: (i, 0)),
        scratch_shapes=[pltpu.VMEM((tm * TOP_K * (D_MODEL // LANES), LANES), F32),
                        pltpu.SemaphoreType.DMA],
    )
    return pl.pallas_call(
        _combine_body,
        name="combine_rows",
        grid_spec=grid_spec,
        out_shape=jax.ShapeDtypeStruct((n_tok, D_MODEL), F32),
        compiler_params=_cparams("arbitrary"),
    )(run_start, run_local, run_count, y_rows, lpos, gate_tile, h, mod6, final_g)


def kernel(x, c, ada_w, ada_b, norm1_g, w_in, conv_w, conv_gn, rwkv_mu, rwkv_w0, rwkv_w_up,
           rwkv_a0, rwkv_a_up, rwkv_g_up, rwkv_k_k, rwkv_k_a, rwkv_r_k, rwkv_ln_w, rwkv_ln_b,
           w_out, norm2_g, router_w, router_b, exp_w1, exp_b1, exp_w2, exp_b2, final_g):
    bsz, seq, _ = x.shape
    depth = ada_w.shape[0]
    assert depth == 1, "the final norm is fused into the last layer's combine step"
    n_tok = bsz * seq
    assert seq % TOKEN_TILE == 0 and seq % CHUNK == 0 and n_tok % MOVE_TILE == 0
    assert seq % COMBINE_TILE == 0
    n_rows = n_tok * TOP_K
    assert n_rows % ROW_BLOCK == 0
    n_blocks = n_rows // ROW_BLOCK + N_EXPERTS
    m_pad = n_blocks * ROW_BLOCK

    h = x.reshape(n_tok, D_MODEL)
    out = h
    for l in range(depth):
        mod6 = _ada_mod(c, ada_w[l], ada_b[l]).reshape(bsz, 6, 1, D_MODEL)
        pc, pr = _in_proj(h, norm1_g[l].reshape(1, -1), mod6, w_in[l].astype(BF16),
                          seq // TOKEN_TILE)
        yc = _conv_mixer(pc, conv_w[l], conv_gn[l].reshape(1, -1), bsz, seq)
        yr = _rwkv_mixer(pr, rwkv_mu[l], rwkv_w0[l], rwkv_w_up[l], rwkv_a0[l], rwkv_a_up[l],
                         rwkv_g_up[l], rwkv_k_k[l], rwkv_k_a[l], rwkv_r_k[l], rwkv_ln_w[l],
                         rwkv_ln_b[l], bsz, seq)
        rw = jnp.pad(router_w[l], ((0, 0), (0, LANES - N_EXPERTS)))
        rw_hi = rw.astype(BF16)
        rw_lo = (rw - rw_hi.astype(F32)).astype(BF16)
        rb = jnp.pad(router_b[l], (0, LANES - N_EXPERTS)).reshape(1, LANES)
        h, u2, gate_tile, idx_tile, cnt = _out_proj_router(
            yc.reshape(n_tok, D_CONV), yr.reshape(n_tok, D_RWKV), h, w_out[l].astype(BF16), mod6,
            norm2_g[l].reshape(1, -1), rw_hi, rw_lo, rb, seq // TOKEN_TILE)
        counts = jnp.sum(cnt, axis=0)[:N_EXPERTS].astype(I32)
        padded = (counts + ROW_BLOCK - 1) // ROW_BLOCK * ROW_BLOCK
        pad_end = jnp.cumsum(padded)
        pad_start = pad_end - padded
        n_used = (pad_end[-1] // ROW_BLOCK).astype(I32).reshape(1)
        blk = jnp.minimum(jnp.arange(n_blocks, dtype=I32), n_used[0] - 1) * ROW_BLOCK
        block_expert = jnp.minimum(
            jnp.sum((pad_end[None, :] <= blk[:, None]).astype(I32), axis=1), N_EXPERTS - 1)
        tail = (n_used[0] + jnp.arange(N_EXPERTS, dtype=I32)) * ROW_BLOCK
        zrow = jnp.concatenate([jnp.where(padded > 0, pad_end - ROW_BLOCK, -1),
                                jnp.where(tail < m_pad, tail, -1)]).astype(I32)
        base = jnp.broadcast_to(
            jnp.pad(pad_start.astype(F32), (0, LANES - N_EXPERTS))[None, :], (SUBLANES, LANES))
        dest, lpos, tab = _assign_rows(idx_tile, base)
        run_count = tab[:, 0, :N_EXPERTS].astype(I32)
        run_local = tab[:, 1, :N_EXPERTS].astype(I32)
        run_start = pad_start[None, :] + tab[:, 2, :N_EXPERTS].astype(I32)
        xs = _dispatch(u2, dest, zrow, m_pad)
        y_rows = _experts(xs, block_expert, n_used, _deinterleave_w1(exp_w1[l]),
                          exp_w2[l].astype(BF16),
                          exp_b1[l][:, None, 0::2], exp_b1[l][:, None, 1::2], exp_b2[l][:, None, :])
        out = _combine(y_rows, run_start.reshape(-1), run_local.reshape(-1), run_count.reshape(-1),
                       lpos, gate_tile, h, mod6, final_g.reshape(1, -1), seq // COMBINE_TILE)
    return out.reshape(bsz, seq, D_MODEL)
```

```python
import jax
import jax.numpy as jnp
from jax import lax
from jax.experimental import pallas as pl
from jax.experimental.pallas import tpu as pltpu

F32 = jnp.float32
BF16 = jnp.bfloat16
I32 = jnp.int32

D_MODEL = 1024
D_CONV = 512
CONV_GROUP_DIM = 64
D_RWKV = 512
HEAD_DIM = 64
N_HEADS = D_RWKV // HEAD_DIM
DECAY_LORA = 64
AAA_LORA = 64
GATE_LORA = 128
COLS_CONV = 3 * D_CONV
COLS_RWKV = 3 * D_RWKV + DECAY_LORA + AAA_LORA + GATE_LORA
N_EXPERTS = 32
TOP_K = 4
D_EXPERT = D_MODEL
SWIGLU_LIMIT = 7.0
SWIGLU_ALPHA = 1.702
NORM_EPS = 1e-5
GN_EPS = 64e-5

LANES = 128
SUBLANES = 8
MXU_TILE = 256
CHUNK = 64
RWKV_SEQS_PER_STEP = 2
ROW_BLOCK = 256
TOKEN_TILE = 512
COMBINE_TILE = 256
VMEM_LIMIT = 48 * 1024 * 1024


def _cparams(*sem):
    return pltpu.CompilerParams(dimension_semantics=sem, vmem_limit_bytes=VMEM_LIMIT)


def _mm(a, b):
    return jnp.dot(a.astype(BF16), b.astype(BF16), preferred_element_type=F32)


def _mm_nt(a, b):
    return lax.dot_general(a.astype(BF16), b.astype(BF16), (((1,), (1,)), ((), ())),
                           preferred_element_type=F32)


def _mm_tn(a, b):
    return lax.dot_general(a.astype(BF16), b.astype(BF16), (((0,), (0,)), ((), ())),
                           preferred_element_type=F32)


def _split2(x):
    hi = x.astype(BF16)
    lo = (x - hi.astype(F32)).astype(BF16)
    return hi, lo


def _split3(x):
    hi = x.astype(BF16)
    r = x - hi.astype(F32)
    mid = r.astype(BF16)
    lo = (r - mid.astype(F32)).astype(BF16)
    return hi, mid, lo


def _sigmoid(x):
    return 1.0 / (1.0 + jnp.exp(-x))


def _rms(x):
    return x * lax.rsqrt(jnp.mean(x * x, axis=-1, keepdims=True) + NORM_EPS)


def _ada_body(c_ref, w_ref, b_ref, o_ref):
    c = c_ref[...]
    ca = c * _sigmoid(c)
    ah, al = _split2(ca)
    wh, wl = _split2(w_ref[...])
    acc = jnp.dot(ah, wh, preferred_element_type=F32)
    acc += jnp.dot(ah, wl, preferred_element_type=F32)
    acc += jnp.dot(al, wh, preferred_element_type=F32)
    o_ref[...] = acc + b_ref[...]


def _ada_mod(c, ada_w, ada_b):
    bsz = c.shape[0]
    n = ada_w.shape[1]
    tn = 1024
    return pl.pallas_call(
        _ada_body,
        name="ada_mod",
        grid=(n // tn,),
        in_specs=[pl.BlockSpec((bsz, D_MODEL), lambda j: (0, 0)),
                  pl.BlockSpec((D_MODEL, tn), lambda j: (0, j)),
                  pl.BlockSpec((1, tn), lambda j: (0, j))],
        out_specs=pl.BlockSpec((bsz, tn), lambda j: (0, j)),
        out_shape=jax.ShapeDtypeStruct((bsz, n), F32),
        compiler_params=_cparams("parallel"),
    )(c, ada_w, ada_b.reshape(1, n))


def _inproj_body(x_ref, g_ref, sh_ref, sc_ref, w_ref, oc_ref, or_ref):
    u = _rms(x_ref[...]) * g_ref[...]
    u = u * (1.0 + sc_ref[0, 0]) + sh_ref[0, 0]
    acc = jnp.dot(u.astype(BF16), w_ref[...], preferred_element_type=F32)
    oc_ref[...] = acc[:, :COLS_CONV]
    or_ref[...] = acc[:, COLS_CONV:]


def _in_proj(xf, g1, mod6, w_in_bf, tiles_per_batch):
    n_tok = xf.shape[0]
    tm = TOKEN_TILE
    n_in = w_in_bf.shape[1]
    return pl.pallas_call(
        _inproj_body,
        name="in_proj",
        grid=(n_tok // tm,),
        in_specs=[pl.BlockSpec((tm, D_MODEL), lambda i: (i, 0)),
                  pl.BlockSpec((1, D_MODEL), lambda i: (0, 0)),
                  pl.BlockSpec((1, 1, 1, D_MODEL), lambda i: (i // tiles_per_batch, 0, 0, 0)),
                  pl.BlockSpec((1, 1, 1, D_MODEL), lambda i: (i // tiles_per_batch, 1, 0, 0)),
                  pl.BlockSpec((D_MODEL, n_in), lambda i: (0, 0))],
        out_specs=[pl.BlockSpec((tm, COLS_CONV), lambda i: (i, 0)),
                   pl.BlockSpec((tm, COLS_RWKV), lambda i: (i, 0))],
        out_shape=[jax.ShapeDtypeStruct((n_tok, COLS_CONV), F32),
                   jax.ShapeDtypeStruct((n_tok, COLS_RWKV), F32)],
        compiler_params=_cparams("parallel"),
    )(xf, g1, mod6, mod6, w_in_bf)


def _conv_body(p_ref, cw_ref, gn_ref, o_ref, carry_ref):
    @pl.when(pl.program_id(1) == 0)
    def _():
        carry_ref[...] = jnp.zeros_like(carry_ref)

    p = p_ref[0]
    tc = p.shape[0]
    b_gate = p[:, :D_CONV]
    u = p[:, D_CONV:2 * D_CONV] * p[:, 2 * D_CONV:]
    ext = jnp.concatenate([carry_ref[...], u], axis=0)
    u1 = pltpu.roll(ext, 1, axis=0)[SUBLANES:]
    u2 = pltpu.roll(ext, 2, axis=0)[SUBLANES:]
    carry_ref[...] = u[tc - SUBLANES:, :]
    cw = cw_ref[...]
    y = b_gate * (cw[0:1] * u2 + cw[1:2] * u1 + cw[2:3] * u)
    outs = []
    for g in range(D_CONV // CONV_GROUP_DIM):
        outs.append(_rms(y[:, g * CONV_GROUP_DIM:(g + 1) * CONV_GROUP_DIM]))
    o_ref[0] = jnp.concatenate(outs, axis=-1) * gn_ref[...]


def _conv_mixer(pc, conv_w, conv_gn, bsz, seq):
    tc = TOKEN_TILE
    return pl.pallas_call(
        _conv_body,
        name="conv_mixer",
        grid=(bsz, seq // tc),
        in_specs=[pl.BlockSpec((1, tc, COLS_CONV), lambda b, i: (b, i, 0)),
                  pl.BlockSpec((3, D_CONV), lambda b, i: (0, 0)),
                  pl.BlockSpec((1, D_CONV), lambda b, i: (0, 0))],
        out_specs=pl.BlockSpec((1, tc, D_CONV), lambda b, i: (b, i, 0)),
        out_shape=jax.ShapeDtypeStruct((bsz, seq, D_CONV), F32),
        scratch_shapes=[pltpu.VMEM((SUBLANES, D_CONV), F32)],
        compiler_params=_cparams("parallel", "arbitrary"),
    )(pc.reshape(bsz, seq, COLS_CONV), conv_w, conv_gn)


def _rwkv_body(p_ref, mu_ref, w0_ref, wup_ref, a0_ref, aup_ref, gup_ref, kkw_ref, kaw_ref,
               rkw_ref, lnw_ref, lnb_ref, o_ref, state_ref, prev_ref):
    @pl.when(pl.program_id(1) == 0)
    def _():
        state_ref[...] = jnp.zeros_like(state_ref)
        prev_ref[...] = jnp.zeros_like(prev_ref)

    C = CHUNK
    N = HEAD_DIM
    nb = p_ref.shape[0]
    s1, s2, s3 = D_RWKV, 2 * D_RWKV, 3 * D_RWKV
    row1 = lax.broadcasted_iota(I32, (C, 1), 0)
    ri = lax.broadcasted_iota(I32, (C, C), 0)
    ci = lax.broadcasted_iota(I32, (C, C), 1)
    strict = ri > ci
    incl = ri >= ci
    eye = jnp.where(ri == ci, 1.0, 0.0).astype(F32)
    tri = jnp.where(incl, 1.0, 0.0).astype(BF16)

    units = []
    for bb in range(nb):
        P = p_ref[bb]
        Pprev = jnp.where(row1 == 0, prev_ref[bb, 0:1, :], pltpu.roll(P, 1, axis=0))
        prev_ref[bb] = jnp.broadcast_to(P[C - 1:C, :], prev_ref.shape[1:])
        p = P + (Pprev - P) * mu_ref[...]
        r = p[:, :s1]
        k = p[:, s1:s2]
        v = p[:, s2:s3]
        dw = p[:, s3:s3 + DECAY_LORA]
        da = p[:, s3 + DECAY_LORA:s3 + DECAY_LORA + AAA_LORA]
        dg = p[:, s3 + DECAY_LORA + AAA_LORA:]
        z = -(w0_ref[...] + _mm(jnp.tanh(dw), wup_ref[...]))
        softplus = jnp.maximum(z, 0.0) + jnp.log(1.0 + jnp.exp(-jnp.abs(z)))
        logw = -jnp.exp(-softplus - 0.5)
        a = _sigmoid(a0_ref[...] + _mm(da, aup_ref[...]))
        g = _mm(_sigmoid(dg), gup_ref[...])
        kk = k * kkw_ref[...]
        kmod = k * (1.0 + (a - 1.0) * kaw_ref[...])
        cum = sum(jnp.dot(tri, part, preferred_element_type=F32) for part in _split3(logw))
        gam = jnp.exp(cum)
        gam_prev = jnp.exp(cum - logw)
        ginv = jnp.exp(-cum)
        for h in range(N_HEADS):
            sl = slice(h * N, (h + 1) * N)
            kkh = kk[:, sl]
            kkh = kkh / jnp.maximum(jnp.sqrt(jnp.sum(kkh * kkh, axis=-1, keepdims=True)), 1e-12)
            rh, kh, vh = r[:, sl], kmod[:, sl], v[:, sl]
            At = (-kkh * gam_prev[:, sl]).astype(BF16)
            Rt = (rh * gam[:, sl]).astype(BF16)
            BK = jnp.concatenate([kkh * a[:, sl] * ginv[:, sl], kh * ginv[:, sl]],
                                 axis=0).astype(BF16)
            units.append(dict(
                slot=bb * N_HEADS + h, At=At, Rt=Rt, BK=BK, v=vh, vb=vh.astype(BF16),
                AR=jnp.concatenate([At, Rt], axis=0), g=g[:, sl], glast=gam[C - 1:C, sl],
                bonus=jnp.sum(rh * kh * rkw_ref[:, sl], axis=-1, keepdims=True),
                lnw=lnw_ref[:, sl], lnb=lnb_ref[:, sl]))

    for u in units:
        M = _mm_nt(u["AR"], u["BK"])
        lab = jnp.where(strict, M[:C, :C], 0.0)
        u["T"] = eye + lab
        u["Lp"] = lab.astype(BF16)
        u["Lak"] = jnp.where(strict, M[:C, C:], 0.0).astype(BF16)
        u["Tr"] = jnp.concatenate([jnp.where(incl, M[C:, :C], 0.0),
                                   jnp.where(incl, M[C:, C:], 0.0)], axis=1).astype(BF16)
    for u in units:
        u["X2"] = _mm(u["Lak"], u["vb"])
        u["Lp"] = _mm(u["Lp"], u["Lp"]).astype(BF16)
    for _ in range(4):
        for u in units:
            u["T"] = u["T"] + _mm(u["T"], u["Lp"])
            u["Lp"] = _mm(u["Lp"], u["Lp"]).astype(BF16)
    for u in units:
        u["T"] = u["T"] + _mm(u["T"], u["Lp"])
    for u in units:
        u["W"] = _mm(u["T"], jnp.concatenate([u["At"], u["X2"].astype(BF16)], axis=1))
    for u in units:
        u["S0"] = state_ref[u["slot"]]
        u["S0b"] = u["S0"].astype(BF16)
        U = _mm_nt(u["W"][:, :N], u["S0b"]) + u["W"][:, N:]
        u["UV"] = jnp.concatenate([U.astype(BF16), u["vb"]], axis=0)
    for u in units:
        u["O"] = _mm_nt(u["Rt"], u["S0b"]) + _mm(u["Tr"], u["UV"])
        state_ref[u["slot"]] = (u["S0"] + _mm_tn(u["UV"], u["BK"])) * u["glast"]

    for bb in range(nb):
        outs = []
        for u in units[bb * N_HEADS:(bb + 1) * N_HEADS]:
            O = u["O"]
            mean = jnp.mean(O, axis=-1, keepdims=True)
            cen = O - mean
            var = jnp.mean(cen * cen, axis=-1, keepdims=True)
            on = cen * lax.rsqrt(var + GN_EPS) * u["lnw"] + u["lnb"]
            outs.append((on + u["bonus"] * u["v"]) * u["g"])
        o_ref[bb] = jnp.concatenate(outs, axis=-1)


def _rwkv_mixer(pr, mu, w0, w_up, a0, a_up, g_up, k_k, k_a, r_k, ln_w, ln_b, bsz, seq):
    C = CHUNK
    nb = RWKV_SEQS_PER_STEP if bsz % RWKV_SEQS_PER_STEP == 0 else 1
    row = lambda n: pl.BlockSpec((1, n), lambda b, i: (0, 0))
    full = lambda m, n: pl.BlockSpec((m, n), lambda b, i: (0, 0))
    return pl.pallas_call(
        _rwkv_body,
        name="rwkv_mixer",
        grid=(bsz // nb, seq // C),
        in_specs=[pl.BlockSpec((nb, C, COLS_RWKV), lambda b, i: (b, i, 0)),
                  row(COLS_RWKV), row(D_RWKV), full(DECAY_LORA, D_RWKV), row(D_RWKV),
                  full(AAA_LORA, D_RWKV), full(GATE_LORA, D_RWKV), row(D_RWKV), row(D_RWKV),
                  row(D_RWKV), row(D_RWKV), row(D_RWKV)],
        out_specs=pl.BlockSpec((nb, C, D_RWKV), lambda b, i: (b, i, 0)),
        out_shape=jax.ShapeDtypeStruct((bsz, seq, D_RWKV), F32),
        scratch_shapes=[pltpu.VMEM((nb * N_HEADS, HEAD_DIM, HEAD_DIM), F32),
                        pltpu.VMEM((nb, SUBLANES, COLS_RWKV), F32)],
        compiler_params=_cparams("parallel", "arbitrary"),
    )(pr.reshape(bsz, seq, COLS_RWKV), mu.reshape(1, -1), w0.reshape(1, -1), w_up,
      a0.reshape(1, -1), a_up, g_up, k_k.reshape(1, -1), k_a.reshape(1, -1), r_k.reshape(1, -1),
      ln_w.reshape(1, -1), ln_b.reshape(1, -1))


def _outproj_body(yc_ref, yr_ref, x_ref, wo_ref, gt_ref, g2_ref, sh_ref, sc_ref, rwh_ref, rwl_ref,
                  rb_ref, h_ref, u_ref, gate_ref, idx_ref):
    mix =_mm(yc_ref[...], wo_ref[:D_CONV, :]) + _mm(yr_ref[...], wo_ref[D_CONV:, :])
    h = x_ref[...] + gt_ref[0, 0] * mix
    h_ref[...] = h
    u2 = _rms(h) * g2_ref[...]
    u2 = u2 * (1.0 + sc_ref[0, 0]) + sh_ref[0, 0]
    u_ref[...] = u2

    uh, ul = _split2(u2)
    logits = (jnp.dot(uh, rwh_ref[...], preferred_element_type=F32)
              + jnp.dot(uh, rwl_ref[...], preferred_element_type=F32)
              + jnp.dot(ul, rwh_ref[...], preferred_element_type=F32)) + rb_ref[...]
    tm = logits.shape[0]
    lane = lax.broadcasted_iota(I32, (tm, LANES), 1)
    neg = jnp.float32(-jnp.inf)
    l = jnp.where(lane < N_EXPERTS, logits, neg)
    tops, idxs = [], []
    for _ in range(TOP_K):
        m = jnp.max(l, axis=-1, keepdims=True)
        ix = jnp.min(jnp.where(l == m, lane, LANES), axis=-1, keepdims=True)
        tops.append(m)
        idxs.append(ix)
        l = jnp.where(lane == ix, neg, l)
    es = [jnp.exp(t - tops[0]) for t in tops]
    den = es[0] + es[1] + es[2] + es[3]
    gate_tile = jnp.zeros((tm, LANES), F32)
    idx_tile = jnp.zeros((tm, LANES), I32)
    for kq in range(TOP_K):
        gate_tile = jnp.where(lane == kq, es[kq] / den, gate_tile)
        idx_tile = jnp.where(lane == kq, idxs[kq], idx_tile)
    gate_ref[...] = gate_tile
    idx_ref[...] = idx_tile


def _out_proj_router(yc, yr, xf, w_out_bf, mod6, g2, rw_hi, rw_lo, rb, tiles_per_batch):
    n_tok = xf.shape[0]
    tm = TOKEN_TILE
    tok = lambda n: pl.BlockSpec((tm, n), lambda i: (i, 0))
    modspec = lambda which: pl.BlockSpec((1, 1, 1, D_MODEL),
                                         lambda i: (i // tiles_per_batch, which, 0, 0))
    const = lambda m, n: pl.BlockSpec((m, n), lambda i: (0, 0))
    return pl.pallas_call(
        _outproj_body,
        name="out_proj_router",
        grid=(n_tok // tm,),
        in_specs=[tok(D_CONV), tok(D_RWKV), tok(D_MODEL), const(D_MODEL, D_MODEL),
                  modspec(2), const(1, D_MODEL), modspec(3), modspec(4),
                  const(D_MODEL, LANES), const(D_MODEL, LANES), const(1, LANES)],
        out_specs=[tok(D_MODEL), tok(D_MODEL), tok(LANES), tok(LANES)],
        out_shape=[jax.ShapeDtypeStruct((n_tok, D_MODEL), F32),
                   jax.ShapeDtypeStruct((n_tok, D_MODEL), F32),
                   jax.ShapeDtypeStruct((n_tok, LANES), F32),
                   jax.ShapeDtypeStruct((n_tok, LANES), I32)],
        compiler_params=_cparams("parallel"),
    )(yc, yr, xf, w_out_bf, mod6, g2, mod6, mod6, rw_hi, rw_lo, rb)


def _sort_body(idx_ref, u_ref, lpos_ref, tab_ref, xl_ref):
    tm = idx_ref.shape[0]
    rr = lax.broadcasted_iota(I32, (tm, tm), 0)
    cc = lax.broadcasted_iota(I32, (tm, tm), 1)
    tri = jnp.where(rr > cc, 1.0, 0.0).astype(BF16)

    idx = idx_ref[...]
    lane = lax.broadcasted_iota(I32, (tm, LANES), 1)
    masks = [lane == idx[:, kq:kq + 1] for kq in range(TOP_K)]
    sel = jnp.zeros((tm, LANES), F32)
    for mk in masks:
        sel = sel + jnp.where(mk, 1.0, 0.0)
    prefix = jnp.dot(tri, sel.astype(BF16), preferred_element_type=F32)
    cnt = jnp.broadcast_to(jnp.sum(sel, axis=0, keepdims=True), (SUBLANES, LANES))
    lane8 = lax.broadcasted_iota(I32, (SUBLANES, LANES), 1)
    inc = cnt
    for sh in (1, 2, 4, 8, 16, 32, 64):
        inc = inc + jnp.where(lane8 >= sh, pltpu.roll(inc, sh, axis=1), 0.0)
    loff = inc - cnt
    local_row = loff[0:1, :] + prefix
    lpos_tile = jnp.zeros((tm, LANES), F32)
    for kq in range(TOP_K):
        lk = jnp.sum(jnp.where(masks[kq], local_row, 0.0), axis=-1, keepdims=True)
        lpos_tile = jnp.where(lane == kq, lk, lpos_tile)
    lpos_ref[...] = lpos_tile
    row8 = lax.broadcasted_iota(I32, (SUBLANES, LANES), 0)
    tab_ref[0] = jnp.where(row8 == 0, cnt, loff)

    n_loc = tm * TOP_K
    n_seg = D_MODEL // LANES
    lpos_t = jnp.transpose(lpos_tile)[0:SUBLANES, :].astype(I32)
    q = lax.broadcasted_iota(I32, (n_loc, tm), 0)
    pick = jnp.zeros((n_loc, tm), F32)
    for kq in range(TOP_K):
        pick = jnp.where(q == lpos_t[kq:kq + 1, :], 1.0, pick)
    xl = jnp.dot(pick.astype(BF16), u_ref[...].astype(BF16), preferred_element_type=F32)
    for j in range(n_seg):
        xl_ref[pl.ds(j, n_loc, stride=n_seg), :] = xl[:, j * LANES:(j + 1) * LANES]


def _sort_rows(idx_tile, u2):
    n_tok = idx_tile.shape[0]
    tm = COMBINE_TILE
    n_seg = D_MODEL // LANES
    return pl.pallas_call(
        _sort_body,
        name="sort_rows",
        grid=(n_tok // tm,),
        in_specs=[pl.BlockSpec((tm, LANES), lambda i: (i, 0)),
                  pl.BlockSpec((tm, D_MODEL), lambda i: (i, 0))],
        out_specs=[pl.BlockSpec((tm, LANES), lambda i: (i, 0)),
                   pl.BlockSpec((1, SUBLANES, LANES), lambda i: (i, 0, 0)),
                   pl.BlockSpec((tm * TOP_K * n_seg, LANES), lambda i: (i, 0))],
        out_shape=[jax.ShapeDtypeStruct((n_tok, LANES), F32),
                   jax.ShapeDtypeStruct((n_tok // tm, SUBLANES, LANES), F32),
                   jax.ShapeDtypeStruct((n_tok * TOP_K * n_seg, LANES), F32)],
        compiler_params=_cparams("parallel"),
    )(idx_tile, u2)


def _deinterleave_body(w_ref, o_ref):
    w = w_ref[0].astype(BF16)
    half = o_ref.shape[2] // 2
    src = lax.broadcasted_iota(I32, (2 * MXU_TILE, MXU_TILE), 0)
    dst = lax.broadcasted_iota(I32, (2 * MXU_TILE, MXU_TILE), 1)
    pick_even = jnp.where(src == 2 * dst, 1.0, 0.0).astype(BF16)
    pick_odd = jnp.where(src == 2 * dst + 1, 1.0, 0.0).astype(BF16)
    for j in range(half // MXU_TILE):
        slab = w[:, 2 * MXU_TILE * j:2 * MXU_TILE * (j + 1)]
        lo = MXU_TILE * j
        o_ref[0, :, lo:lo + MXU_TILE] = jnp.dot(
            slab, pick_even, preferred_element_type=F32).astype(BF16)
        o_ref[0, :, half + lo:half + lo + MXU_TILE] = jnp.dot(
            slab, pick_odd, preferred_element_type=F32).astype(BF16)


def _deinterleave_w1(w1):
    n_e, d_in, two_f = w1.shape
    tr = 512
    return pl.pallas_call(
        _deinterleave_body,
        name="deinterleave_w1",
        grid=(n_e, d_in // tr),
        in_specs=[pl.BlockSpec((1, tr, two_f), lambda e, r: (e, r, 0))],
        out_specs=pl.BlockSpec((1, tr, two_f), lambda e, r: (e, r, 0)),
        out_shape=jax.ShapeDtypeStruct((n_e, d_in, two_f), BF16),
        compiler_params=_cparams("parallel", "parallel"),
    )(w1)


def _expert_body(be_ref, nu_ref, ft_ref, lt_ref, nv_ref, rg_ref, rl_ref, rc_ref,
                 xl_ref, w1_ref, w2_ref, bg_ref, bl_ref, b2_ref, y_ref, xbuf_ref, sem):
    i = pl.program_id(0)
    n_used = nu_ref[0]
    bm = ROW_BLOCK
    n_seg = D_MODEL // LANES
    n_loc = COMBINE_TILE * TOP_K

    def issue(blk, slot):
        e = be_ref[blk]
        r0 = blk * bm

        def piece(t, carry):
            g = rg_ref[t * N_EXPERTS + e]
            s = jnp.maximum(g, r0)
            n = jnp.minimum(g + rc_ref[t * N_EXPERTS + e], r0 + bm) - s

            @pl.when(n > 0)
            def _():
                src = pl.multiple_of((t * n_loc + rl_ref[t * N_EXPERTS + e] + s - g) * n_seg, n_seg)
                dst = pl.multiple_of((s - r0) * n_seg, n_seg)
                pltpu.make_async_copy(xl_ref.at[pl.ds(src, n * n_seg)],
                                      xbuf_ref.at[slot, pl.ds(dst, n * n_seg)], sem.at[slot]).start()
            return carry

        lax.fori_loop(ft_ref[blk], lt_ref[blk] + 1, piece, 0)

    @pl.when(i == 0)
    def _():
        xbuf_ref[...] = jnp.zeros_like(xbuf_ref)
        issue(0, 0)

    @pl.when(i + 1 < n_used)
    def _():
        issue(i + 1, (i + 1) % 2)

    @pl.when(i >= n_used)
    def _():
        y_ref[...] = jnp.zeros_like(y_ref)

    @pl.when(i < n_used)
    def _():
        slot = i % 2
        n_valid = nv_ref[i] * n_seg
        pltpu.make_async_copy(xl_ref.at[pl.ds(0, n_valid)], xbuf_ref.at[slot, pl.ds(0, n_valid)],
                              sem.at[slot]).wait()
        x = jnp.concatenate([xbuf_ref[slot, pl.ds(j, bm, stride=n_seg), :] for j in range(n_seg)],
                            axis=1).astype(BF16)
        hb = jnp.dot(x, w1_ref[0], preferred_element_type=F32)
        hg = hb[:, :D_EXPERT] + bg_ref[0]
        hl = hb[:, D_EXPERT:] + bl_ref[0]
        xg = jnp.minimum(hg, SWIGLU_LIMIT)
        xl = jnp.clip(hl, -SWIGLU_LIMIT, SWIGLU_LIMIT)
        act = xg * _sigmoid(SWIGLU_ALPHA * xg) * (xl + 1.0)
        y = jnp.dot(act.astype(BF16), w2_ref[0], preferred_element_type=F32) + b2_ref[0]
        for j in range(n_seg):
            y_ref[pl.ds(j, bm, stride=n_seg), :] = y[:, j * LANES:(j + 1) * LANES]


def _experts(x_local, block_tables, run_tables, w1d, w2, b1g, b1l, b2, m_pad):
    bm = ROW_BLOCK
    n_seg = D_MODEL // LANES
    wsel = lambda i, be, *_: (be[i], 0, 0)
    grid_spec = pltpu.PrefetchScalarGridSpec(
        num_scalar_prefetch=8,
        grid=(m_pad // bm,),
        in_specs=[pl.BlockSpec(memory_space=pl.ANY),
                  pl.BlockSpec((1, D_MODEL, 2 * D_EXPERT), wsel),
                  pl.BlockSpec((1, D_EXPERT, D_MODEL), wsel),
                  pl.BlockSpec((1, 1, D_EXPERT), wsel),
                  pl.BlockSpec((1, 1, D_EXPERT), wsel),
                  pl.BlockSpec((1, 1, D_MODEL), wsel)],
        out_specs=pl.BlockSpec((bm * n_seg, LANES), lambda i, *_: (i, 0)),
        scratch_shapes=[pltpu.VMEM((2, bm * n_seg, LANES), F32), pltpu.SemaphoreType.DMA((2,))],
    )
    return pl.pallas_call(
        _expert_body,
        name="expert_mlp",
        grid_spec=grid_spec,
        out_shape=jax.ShapeDtypeStruct((m_pad * n_seg, LANES), F32),
        compiler_params=_cparams("arbitrary"),
    )(*block_tables, *run_tables, x_local, w1d, w2, b1g, b1l, b2)


def _combine_body(rg_ref, rl_ref, rc_ref, y_ref, lpos_ref, gate_ref, h_ref, gt_ref, fg_ref, o_ref,
                  yloc_ref, sem):
    i = pl.program_id(0)
    tm = h_ref.shape[0]
    n_seg = D_MODEL // LANES
    n_loc = yloc_ref.shape[1] // n_seg

    def issue(tile, slot):
        for e in range(N_EXPERTS):
            c = rc_ref[tile * N_EXPERTS + e] * n_seg

            @pl.when(c > 0)
            def _():
                src = pl.multiple_of(rg_ref[tile * N_EXPERTS + e] * n_seg, n_seg)
                dst = pl.multiple_of(rl_ref[tile * N_EXPERTS + e] * n_seg, n_seg)
                pltpu.make_async_copy(y_ref.at[pl.ds(src, c)], yloc_ref.at[slot, pl.ds(dst, c)],
                                      sem.at[slot]).start()

    @pl.when(i == 0)
    def _():
        issue(0, 0)

    @pl.when(i + 1 < pl.num_programs(0))
    def _():
        issue(i + 1, (i + 1) % 2)

    slot = i % 2
    pltpu.make_async_copy(y_ref.at[pl.ds(0, n_loc * n_seg)], yloc_ref.at[slot], sem.at[slot]).wait()

    y = jnp.concatenate([yloc_ref[slot, pl.ds(j, n_loc, stride=n_seg), :] for j in range(n_seg)],
                        axis=1).astype(BF16)
    q = lax.broadcasted_iota(I32, (tm, n_loc), 1)
    lpos = lpos_ref[...].astype(I32)
    gate = gate_ref[...]
    gmat = jnp.zeros((tm, n_loc), F32)
    for kq in range(TOP_K):
        gmat = jnp.where(q == lpos[:, kq:kq + 1], gate[:, kq:kq + 1], gmat)
    ghi, glo = _split2(gmat)
    moe = (jnp.dot(ghi, y, preferred_element_type=F32)
           + jnp.dot(glo, y, preferred_element_type=F32))
    h = h_ref[...] + gt_ref[0, 0] * moe
    o_ref[...] = _rms(h) * fg_ref[...]


def _combine(y_rows, run_start, run_local, run_count, lpos, gate_tile, h, mod6, final_g,
             tiles_per_batch):
    n_tok = h.shape[0]
    tm = COMBINE_TILE
    grid_spec = pltpu.PrefetchScalarGridSpec(
        num_scalar_prefetch=3,
        grid=(n_tok // tm,),
        in_specs=[pl.BlockSpec(memory_space=pl.ANY),
                  pl.BlockSpec((tm, LANES), lambda i, *_: (i, 0)),
                  pl.BlockSpec((tm, LANES), lambda i, *_: (i, 0)),
                  pl.BlockSpec((tm, D_MODEL), lambda i, *_: (i, 0)),
                  pl.BlockSpec((1, 1, 1, D_MODEL), lambda i, *_: (i // tiles_per_batch, 5, 0, 0)),
                  pl.BlockSpec((1, D_MODEL), lambda i, *_: (0, 0))],
        out_specs=pl.BlockSpec((tm, D_MODEL), lambda i, *_: (i, 0)),
        scratch_shapes=[pltpu.VMEM((2, tm * TOP_K * (D_MODEL // LANES), LANES), F32),
                        pltpu.SemaphoreType.DMA((2,))],
    )
    return pl.pallas_call(
        _combine_body,
        name="combine_rows",
        grid_spec=grid_spec,
        out_shape=jax.ShapeDtypeStruct((n_tok, D_MODEL), F32),
        compiler_params=_cparams("arbitrary"),
    )(run_start, run_local, run_count, y_rows, lpos, gate_tile, h, mod6, final_g)


def kernel(x, c, ada_w, ada_b, norm1_g, w_in, conv_w, conv_gn, rwkv_mu, rwkv_w0, rwkv_w_up,
           rwkv_a0, rwkv_a_up, rwkv_g_up, rwkv_k_k, rwkv_k_a, rwkv_r_k, rwkv_ln_w, rwkv_ln_b,
           w_out, norm2_g, router_w, router_b, exp_w1, exp_b1, exp_w2, exp_b2, final_g):
    bsz, seq, _ = x.shape
    depth = ada_w.shape[0]
    assert depth == 1, "the final norm is fused into the last layer's combine step"
    n_tok = bsz * seq
    assert seq % TOKEN_TILE == 0 and seq % CHUNK == 0 and seq % COMBINE_TILE == 0
    n_rows = n_tok * TOP_K
    assert n_rows % ROW_BLOCK == 0
    n_blocks = n_rows // ROW_BLOCK + N_EXPERTS
    m_pad = n_blocks * ROW_BLOCK

    h = x.reshape(n_tok, D_MODEL)
    out = h
    for l in range(depth):
        mod6 = _ada_mod(c, ada_w[l], ada_b[l]).reshape(bsz, 6, 1, D_MODEL)
        pc, pr = _in_proj(h, norm1_g[l].reshape(1, -1), mod6, w_in[l].astype(BF16),
                          seq // TOKEN_TILE)
        yc = _conv_mixer(pc, conv_w[l], conv_gn[l].reshape(1, -1), bsz, seq)
        yr = _rwkv_mixer(pr, rwkv_mu[l], rwkv_w0[l], rwkv_w_up[l], rwkv_a0[l], rwkv_a_up[l],
                         rwkv_g_up[l], rwkv_k_k[l], rwkv_k_a[l], rwkv_r_k[l], rwkv_ln_w[l],
                         rwkv_ln_b[l], bsz, seq)
        rw = jnp.pad(router_w[l], ((0, 0), (0, LANES - N_EXPERTS)))
        rw_hi = rw.astype(BF16)
        rw_lo = (rw - rw_hi.astype(F32)).astype(BF16)
        rb = jnp.pad(router_b[l], (0, LANES - N_EXPERTS)).reshape(1, LANES)
        h, u2, gate_tile, idx_tile = _out_proj_router(
            yc.reshape(n_tok, D_CONV), yr.reshape(n_tok, D_RWKV), h, w_out[l].astype(BF16), mod6,
            norm2_g[l].reshape(1, -1), rw_hi, rw_lo, rb, seq // TOKEN_TILE)
        lpos, tab, x_local = _sort_rows(idx_tile, u2)
        run_count = tab[:, 0, :N_EXPERTS].astype(I32)
        run_local = tab[:, 1, :N_EXPERTS].astype(I32)
        counts = jnp.sum(run_count, axis=0)
        padded = (counts + ROW_BLOCK - 1) // ROW_BLOCK * ROW_BLOCK
        pad_end = jnp.cumsum(padded)
        pad_start = pad_end - padded
        run_start = pad_start[None, :] + jnp.cumsum(run_count, axis=0) - run_count
        n_used = (pad_end[-1] // ROW_BLOCK).astype(I32).reshape(1)
        blk_row = jnp.minimum(jnp.arange(n_blocks, dtype=I32), n_used[0] - 1) * ROW_BLOCK
        block_expert = jnp.minimum(
            jnp.sum((pad_end[None, :] <= blk_row[:, None]).astype(I32), axis=1), N_EXPERTS - 1)
        blk_start = run_start.T[block_expert]
        blk_end = blk_start + run_count.T[block_expert]
        first_tile = jnp.sum((blk_end <= blk_row[:, None]).astype(I32), axis=1)
        last_tile = jnp.sum((blk_start < blk_row[:, None] + ROW_BLOCK).astype(I32), axis=1) - 1
        n_valid = jnp.clip((pad_start + counts)[block_expert] - blk_row, 0, ROW_BLOCK).astype(I32)
        run_tables = (run_start.reshape(-1), run_local.reshape(-1), run_count.reshape(-1))
        y_rows = _experts(x_local, (block_expert, n_used, first_tile, last_tile, n_valid), run_tables,
                          _deinterleave_w1(exp_w1[l]), exp_w2[l].astype(BF16),
                          exp_b1[l][:, None, 0::2], exp_b1[l][:, None, 1::2], exp_b2[l][:, None, :],
                          m_pad)
        out = _combine(y_rows, *run_tables, lpos, gate_tile, h, mod6, final_g.reshape(1, -1),
                       seq // COMBINE_TILE)
    return out.reshape(bsz, seq, D_MODEL)
```

```python
import jax
import jax.numpy as jnp
from jax import lax
from jax.experimental import pallas as pl
from jax.experimental.pallas import tpu as pltpu

F32 = jnp.float32
BF16 = jnp.bfloat16
I32 = jnp.int32

D_MODEL = 1024
D_CONV = 512
CONV_GROUP_DIM = 64
D_RWKV = 512
HEAD_DIM = 64
N_HEADS = D_RWKV // HEAD_DIM
DECAY_LORA = 64
AAA_LORA = 64
GATE_LORA = 128
COLS_CONV = 3 * D_CONV
COLS_RWKV = 3 * D_RWKV + DECAY_LORA + AAA_LORA + GATE_LORA
N_EXPERTS = 32
TOP_K = 4
D_EXPERT = D_MODEL
SWIGLU_LIMIT = 7.0
SWIGLU_ALPHA = 1.702
NORM_EPS = 1e-5
GN_EPS = 64e-5

LANES = 128
SUBLANES = 8
MXU_TILE = 256
CHUNK = 64
RWKV_SEQS_PER_STEP = 4
RWKV_STAGE_LAG = 5
ROW_BLOCK = 512
TOKEN_TILE = 512
COMBINE_TILE = 256
VMEM_LIMIT = 48 * 1024 * 1024


def _cparams(*sem):
    return pltpu.CompilerParams(dimension_semantics=sem, vmem_limit_bytes=VMEM_LIMIT)


def _mm(a, b):
    return jnp.dot(a.astype(BF16), b.astype(BF16), preferred_element_type=F32)


def _mm_nt(a, b):
    return lax.dot_general(a.astype(BF16), b.astype(BF16), (((1,), (1,)), ((), ())),
                           preferred_element_type=F32)


def _mm_tn(a, b):
    return lax.dot_general(a.astype(BF16), b.astype(BF16), (((0,), (0,)), ((), ())),
                           preferred_element_type=F32)


def _split2(x):
    hi = x.astype(BF16)
    lo = (x - hi.astype(F32)).astype(BF16)
    return hi, lo


def _split3(x):
    hi = x.astype(BF16)
    r = x - hi.astype(F32)
    mid = r.astype(BF16)
    lo = (r - mid.astype(F32)).astype(BF16)
    return hi, mid, lo


def _sigmoid(x):
    return 1.0 / (1.0 + jnp.exp(-x))


def _rms(x):
    return x * lax.rsqrt(jnp.mean(x * x, axis=-1, keepdims=True) + NORM_EPS)


def _ada_body(c_ref, w_ref, b_ref, o_ref):
    c = c_ref[...]
    ca = c * _sigmoid(c)
    ah, al = _split2(ca)
    wh, wl = _split2(w_ref[...])
    acc = jnp.dot(ah, wh, preferred_element_type=F32)
    acc += jnp.dot(ah, wl, preferred_element_type=F32)
    acc += jnp.dot(al, wh, preferred_element_type=F32)
    o_ref[...] = acc + b_ref[...]


def _ada_mod(c, ada_w, ada_b):
    bsz = c.shape[0]
    n = ada_w.shape[1]
    tn = 1024
    return pl.pallas_call(
        _ada_body,
        name="ada_mod",
        grid=(n // tn,),
        in_specs=[pl.BlockSpec((bsz, D_MODEL), lambda j: (0, 0)),
                  pl.BlockSpec((D_MODEL, tn), lambda j: (0, j)),
                  pl.BlockSpec((1, tn), lambda j: (0, j))],
        out_specs=pl.BlockSpec((bsz, tn), lambda j: (0, j)),
        out_shape=jax.ShapeDtypeStruct((bsz, n), F32),
        compiler_params=_cparams("parallel"),
    )(c, ada_w, ada_b.reshape(1, n))


def _inproj_body(x_ref, g_ref, sh_ref, sc_ref, w_ref, oc_ref, or_ref):
    u = _rms(x_ref[...]) * g_ref[...]
    u = u * (1.0 + sc_ref[0, 0]) + sh_ref[0, 0]
    acc = jnp.dot(u.astype(BF16), w_ref[...], preferred_element_type=F32)
    oc_ref[...] = acc[:, :COLS_CONV]
    or_ref[...] = acc[:, COLS_CONV:]


def _in_proj(xf, g1, mod6, w_in_bf, tiles_per_batch):
    n_tok = xf.shape[0]
    tm = TOKEN_TILE
    n_in = w_in_bf.shape[1]
    return pl.pallas_call(
        _inproj_body,
        name="in_proj",
        grid=(n_tok // tm,),
        in_specs=[pl.BlockSpec((tm, D_MODEL), lambda i: (i, 0)),
                  pl.BlockSpec((1, D_MODEL), lambda i: (0, 0)),
                  pl.BlockSpec((1, 1, 1, D_MODEL), lambda i: (i // tiles_per_batch, 0, 0, 0)),
                  pl.BlockSpec((1, 1, 1, D_MODEL), lambda i: (i // tiles_per_batch, 1, 0, 0)),
                  pl.BlockSpec((D_MODEL, n_in), lambda i: (0, 0))],
        out_specs=[pl.BlockSpec((tm, COLS_CONV), lambda i: (i, 0)),
                   pl.BlockSpec((tm, COLS_RWKV), lambda i: (i, 0))],
        out_shape=[jax.ShapeDtypeStruct((n_tok, COLS_CONV), F32),
                   jax.ShapeDtypeStruct((n_tok, COLS_RWKV), F32)],
        compiler_params=_cparams("parallel"),
    )(xf, g1, mod6, mod6, w_in_bf)


def _conv_body(p_ref, cw_ref, gn_ref, o_ref, carry_ref):
    @pl.when(pl.program_id(1) == 0)
    def _():
        carry_ref[...] = jnp.zeros_like(carry_ref)

    p = p_ref[0]
    tc = p.shape[0]
    b_gate = p[:, :D_CONV]
    u = p[:, D_CONV:2 * D_CONV] * p[:, 2 * D_CONV:]
    ext = jnp.concatenate([carry_ref[...], u], axis=0)
    u1 = pltpu.roll(ext, 1, axis=0)[SUBLANES:]
    u2 = pltpu.roll(ext, 2, axis=0)[SUBLANES:]
    carry_ref[...] = u[tc - SUBLANES:, :]
    cw = cw_ref[...]
    y = b_gate * (cw[0:1] * u2 + cw[1:2] * u1 + cw[2:3] * u)
    outs = []
    for g in range(D_CONV // CONV_GROUP_DIM):
        outs.append(_rms(y[:, g * CONV_GROUP_DIM:(g + 1) * CONV_GROUP_DIM]))
    o_ref[0] = jnp.concatenate(outs, axis=-1) * gn_ref[...]


def _conv_mixer(pc, conv_w, conv_gn, bsz, seq):
    tc = TOKEN_TILE
    return pl.pallas_call(
        _conv_body,
        name="conv_mixer",
        grid=(bsz, seq // tc),
        in_specs=[pl.BlockSpec((1, tc, COLS_CONV), lambda b, i: (b, i, 0)),
                  pl.BlockSpec((3, D_CONV), lambda b, i: (0, 0)),
                  pl.BlockSpec((1, D_CONV), lambda b, i: (0, 0))],
        out_specs=pl.BlockSpec((1, tc, D_CONV), lambda b, i: (b, i, 0)),
        out_shape=jax.ShapeDtypeStruct((bsz, seq, D_CONV), F32),
        scratch_shapes=[pltpu.VMEM((SUBLANES, D_CONV), F32)],
        compiler_params=_cparams("parallel", "arbitrary"),
    )(pc.reshape(bsz, seq, COLS_CONV), conv_w, conv_gn)


def _rwkv_body(p_ref, mu_ref, w0_ref, wup_ref, a0_ref, aup_ref, gup_ref, kkw_ref, kaw_ref,
               rkw_ref, lnw_ref, lnb_ref, o_ref, state_ref, prev_ref):
    @pl.when(pl.program_id(1) == 0)
    def _():
        state_ref[...] = jnp.zeros_like(state_ref)
        prev_ref[...] = jnp.zeros_like(prev_ref)

    C = CHUNK
    N = HEAD_DIM
    nb = p_ref.shape[0]
    s1, s2, s3 = D_RWKV, 2 * D_RWKV, 3 * D_RWKV
    row1 = lax.broadcasted_iota(I32, (C, 1), 0)
    ri = lax.broadcasted_iota(I32, (C, C), 0)
    ci = lax.broadcasted_iota(I32, (C, C), 1)
    strict = ri > ci
    incl = ri >= ci
    eye = jnp.where(ri == ci, 1.0, 0.0).astype(F32)
    tri = jnp.where(incl, 1.0, 0.0).astype(BF16)

    def stages(bb):
        units = []
        P = p_ref[bb]
        Pprev = jnp.where(row1 == 0, prev_ref[bb, 0:1, :], pltpu.roll(P, 1, axis=0))
        prev_ref[bb] = jnp.broadcast_to(P[C - 1:C, :], prev_ref.shape[1:])
        p = P + (Pprev - P) * mu_ref[...]
        r = p[:, :s1]
        k = p[:, s1:s2]
        v = p[:, s2:s3]
        dw = p[:, s3:s3 + DECAY_LORA]
        da = p[:, s3 + DECAY_LORA:s3 + DECAY_LORA + AAA_LORA]
        dg = p[:, s3 + DECAY_LORA + AAA_LORA:]
        z = -(w0_ref[...] + _mm(jnp.tanh(dw), wup_ref[...]))
        softplus = jnp.maximum(z, 0.0) + jnp.log(1.0 + jnp.exp(-jnp.abs(z)))
        logw = -jnp.exp(-softplus - 0.5)
        a = _sigmoid(a0_ref[...] + _mm(da, aup_ref[...]))
        g = _mm(_sigmoid(dg), gup_ref[...])
        kk = k * kkw_ref[...]
        kmod = k * (1.0 + (a - 1.0) * kaw_ref[...])
        cum = sum(jnp.dot(tri, part, preferred_element_type=F32) for part in _split3(logw))
        gam = jnp.exp(cum)
        gam_prev = jnp.exp(cum - logw)
        ginv = jnp.exp(-cum)
        for h in range(N_HEADS):
            sl = slice(h * N, (h + 1) * N)
            kkh = kk[:, sl]
            kkh = kkh / jnp.maximum(jnp.sqrt(jnp.sum(kkh * kkh, axis=-1, keepdims=True)), 1e-12)
            rh, kh, vh = r[:, sl], kmod[:, sl], v[:, sl]
            At = (-kkh * gam_prev[:, sl]).astype(BF16)
            Rt = (rh * gam[:, sl]).astype(BF16)
            BK = jnp.concatenate([kkh * a[:, sl] * ginv[:, sl], kh * ginv[:, sl]],
                                 axis=0).astype(BF16)
            units.append(dict(
                slot=bb * N_HEADS + h, At=At, Rt=Rt, BK=BK, v=vh, vb=vh.astype(BF16),
                AR=jnp.concatenate([At, Rt], axis=0), g=g[:, sl], glast=gam[C - 1:C, sl],
                bonus=jnp.sum(rh * kh * rkw_ref[:, sl], axis=-1, keepdims=True),
                lnw=lnw_ref[:, sl], lnb=lnb_ref[:, sl]))
        yield
        for u in units:
            M = _mm_nt(u["AR"], u["BK"])
            lab = jnp.where(strict, M[:C, :C], 0.0)
            u["T"] = eye + lab
            u["Lp"] = lab.astype(BF16)
            u["Lak"] = jnp.where(strict, M[:C, C:], 0.0).astype(BF16)
            u["Tr"] = jnp.concatenate([jnp.where(incl, M[C:, :C], 0.0),
                                       jnp.where(incl, M[C:, C:], 0.0)], axis=1).astype(BF16)
        yield
        for u in units:
            u["X2"] = _mm(u["Lak"], u["vb"])
            u["Lp"] = _mm(u["Lp"], u["Lp"]).astype(BF16)
        yield
        for _ in range(4):
            for u in units:
                u["T"] = u["T"] + _mm(u["T"], u["Lp"])
                u["Lp"] = _mm(u["Lp"], u["Lp"]).astype(BF16)
            yield
        for u in units:
            u["T"] = u["T"] + _mm(u["T"], u["Lp"])
        yield
        for u in units:
            u["W"] = _mm(u["T"], jnp.concatenate([u["At"], u["X2"].astype(BF16)], axis=1))
        yield
        for u in units:
            u["S0"] = state_ref[u["slot"]]
            u["S0b"] = u["S0"].astype(BF16)
            U = _mm_nt(u["W"][:, :N], u["S0b"]) + u["W"][:, N:]
            u["UV"] = jnp.concatenate([U.astype(BF16), u["vb"]], axis=0)
        yield
        for u in units:
            u["O"] = _mm_nt(u["Rt"], u["S0b"]) + _mm(u["Tr"], u["UV"])
            state_ref[u["slot"]] = (u["S0"] + _mm_tn(u["UV"], u["BK"])) * u["glast"]
        yield
        outs = []
        for u in units:
            O = u["O"]
            mean = jnp.mean(O, axis=-1, keepdims=True)
            cen = O - mean
            var = jnp.mean(cen * cen, axis=-1, keepdims=True)
            on = cen * lax.rsqrt(var + GN_EPS) * u["lnw"] + u["lnb"]
            outs.append((on + u["bonus"] * u["v"]) * u["g"])
        o_ref[bb] = jnp.concatenate(outs, axis=-1)
        yield

    live = [(bb, stages(bb)) for bb in range(nb)]
    tick = 0
    while live:
        for item in list(live):
            if tick >= RWKV_STAGE_LAG * item[0] and next(item[1], "done") == "done":
                live.remove(item)
        tick += 1


def _rwkv_mixer(pr, mu, w0, w_up, a0, a_up, g_up, k_k, k_a, r_k, ln_w, ln_b, bsz, seq):
    C = CHUNK
    nb = RWKV_SEQS_PER_STEP if bsz % RWKV_SEQS_PER_STEP == 0 else 1
    row = lambda n: pl.BlockSpec((1, n), lambda b, i: (0, 0))
    full = lambda m, n: pl.BlockSpec((m, n), lambda b, i: (0, 0))
    return pl.pallas_call(
        _rwkv_body,
        name="rwkv_mixer",
        grid=(bsz // nb, seq // C),
        in_specs=[pl.BlockSpec((nb, C, COLS_RWKV), lambda b, i: (b, i, 0)),
                  row(COLS_RWKV), row(D_RWKV), full(DECAY_LORA, D_RWKV), row(D_RWKV),
                  full(AAA_LORA, D_RWKV), full(GATE_LORA, D_RWKV), row(D_RWKV), row(D_RWKV),
                  row(D_RWKV), row(D_RWKV), row(D_RWKV)],
        out_specs=pl.BlockSpec((nb, C, D_RWKV), lambda b, i: (b, i, 0)),
        out_shape=jax.ShapeDtypeStruct((bsz, seq, D_RWKV), F32),
        scratch_shapes=[pltpu.VMEM((nb * N_HEADS, HEAD_DIM, HEAD_DIM), F32),
                        pltpu.VMEM((nb, SUBLANES, COLS_RWKV), F32)],
        compiler_params=_cparams("parallel", "arbitrary"),
    )(pr.reshape(bsz, seq, COLS_RWKV), mu.reshape(1, -1), w0.reshape(1, -1), w_up,
      a0.reshape(1, -1), a_up, g_up, k_k.reshape(1, -1), k_a.reshape(1, -1), r_k.reshape(1, -1),
      ln_w.reshape(1, -1), ln_b.reshape(1, -1))


def _outproj_body(yc_ref, yr_ref, x_ref, wo_ref, gt_ref, g2_ref, sh_ref, sc_ref, rwh_ref, rwl_ref,
                  rb_ref, h_ref, u_ref, gate_ref, idx_ref):
    mix =_mm(yc_ref[...], wo_ref[:D_CONV, :]) + _mm(yr_ref[...], wo_ref[D_CONV:, :])
    h = x_ref[...] + gt_ref[0, 0] * mix
    h_ref[...] = h
    u2 = _rms(h) * g2_ref[...]
    u2 = u2 * (1.0 + sc_ref[0, 0]) + sh_ref[0, 0]
    u_ref[...] = u2

    uh, ul = _split2(u2)
    logits = (jnp.dot(uh, rwh_ref[...], preferred_element_type=F32)
              + jnp.dot(uh, rwl_ref[...], preferred_element_type=F32)
              + jnp.dot(ul, rwh_ref[...], preferred_element_type=F32)) + rb_ref[...]
    tm = logits.shape[0]
    lane = lax.broadcasted_iota(I32, (tm, LANES), 1)
    neg = jnp.float32(-jnp.inf)
    l = jnp.where(lane < N_EXPERTS, logits, neg)
    tops, idxs = [], []
    lane_f = lane.astype(F32)
    for _ in range(TOP_K):
        m = jnp.max(l, axis=-1, keepdims=True)
        ix = jnp.min(jnp.where(l == m, lane_f, float(LANES)), axis=-1, keepdims=True).astype(I32)
        tops.append(m)
        idxs.append(ix)
        l = jnp.where(lane == ix, neg, l)
    es = [jnp.exp(t - tops[0]) for t in tops]
    den = es[0] + es[1] + es[2] + es[3]
    gate_tile = jnp.zeros((tm, LANES), F32)
    idx_tile = jnp.zeros((tm, LANES), I32)
    for kq in range(TOP_K):
        gate_tile = jnp.where(lane == kq, es[kq] / den, gate_tile)
        idx_tile = jnp.where(lane == kq, idxs[kq], idx_tile)
    gate_ref[...] = gate_tile
    idx_ref[...] = idx_tile


def _out_proj_router(yc, yr, xf, w_out_bf, mod6, g2, rw_hi, rw_lo, rb, tiles_per_batch):
    n_tok = xf.shape[0]
    tm = TOKEN_TILE
    tok = lambda n: pl.BlockSpec((tm, n), lambda i: (i, 0))
    modspec = lambda which: pl.BlockSpec((1, 1, 1, D_MODEL),
                                         lambda i: (i // tiles_per_batch, which, 0, 0))
    const = lambda m, n: pl.BlockSpec((m, n), lambda i: (0, 0))
    return pl.pallas_call(
        _outproj_body,
        name="out_proj_router",
        grid=(n_tok // tm,),
        in_specs=[tok(D_CONV), tok(D_RWKV), tok(D_MODEL), const(D_MODEL, D_MODEL),
                  modspec(2), const(1, D_MODEL), modspec(3), modspec(4),
                  const(D_MODEL, LANES), const(D_MODEL, LANES), const(1, LANES)],
        out_specs=[tok(D_MODEL), tok(D_MODEL), tok(LANES), tok(LANES)],
        out_shape=[jax.ShapeDtypeStruct((n_tok, D_MODEL), F32),
                   jax.ShapeDtypeStruct((n_tok, D_MODEL), F32),
                   jax.ShapeDtypeStruct((n_tok, LANES), F32),
                   jax.ShapeDtypeStruct((n_tok, LANES), I32)],
        compiler_params=_cparams("parallel"),
    )(yc, yr, xf, w_out_bf, mod6, g2, mod6, mod6, rw_hi, rw_lo, rb)


def _sort_body(idx_ref, u_ref, lpos_ref, tab_ref, xl_ref):
    tm = idx_ref.shape[0]
    rr = lax.broadcasted_iota(I32, (tm, tm), 0)
    cc = lax.broadcasted_iota(I32, (tm, tm), 1)
    tri = jnp.where(rr > cc, 1.0, 0.0).astype(BF16)

    idx = idx_ref[...]
    lane = lax.broadcasted_iota(I32, (tm, LANES), 1)
    masks = [lane == idx[:, kq:kq + 1] for kq in range(TOP_K)]
    sel = jnp.zeros((tm, LANES), F32)
    for mk in masks:
        sel = sel + jnp.where(mk, 1.0, 0.0)
    prefix = jnp.dot(tri, sel.astype(BF16), preferred_element_type=F32)
    cnt = jnp.broadcast_to(jnp.sum(sel, axis=0, keepdims=True), (SUBLANES, LANES))
    lane8 = lax.broadcasted_iota(I32, (SUBLANES, LANES), 1)
    inc = cnt
    for sh in (1, 2, 4, 8, 16, 32, 64):
        inc = inc + jnp.where(lane8 >= sh, pltpu.roll(inc, sh, axis=1), 0.0)
    loff = inc - cnt
    local_row = loff[0:1, :] + prefix
    lpos_tile = jnp.zeros((tm, LANES), F32)
    for kq in range(TOP_K):
        lk = jnp.sum(jnp.where(masks[kq], local_row, 0.0), axis=-1, keepdims=True)
        lpos_tile = jnp.where(lane == kq, lk, lpos_tile)
    lpos_ref[...] = lpos_tile
    row8 = lax.broadcasted_iota(I32, (SUBLANES, LANES), 0)
    tab_ref[0] = jnp.where(row8 == 0, cnt, loff)

    n_loc = tm * TOP_K
    n_seg = D_MODEL // LANES
    lpos_t = jnp.transpose(lpos_tile)[0:SUBLANES, :].astype(I32)
    q = lax.broadcasted_iota(I32, (n_loc, tm), 0)
    pick = jnp.zeros((n_loc, tm), F32)
    for kq in range(TOP_K):
        pick = jnp.where(q == lpos_t[kq:kq + 1, :], 1.0, pick)
    xl = jnp.dot(pick.astype(BF16), u_ref[...].astype(BF16), preferred_element_type=F32)
    for j in range(n_seg):
        xl_ref[pl.ds(j, n_loc, stride=n_seg), :] = xl[:, j * LANES:(j + 1) * LANES]


def _sort_rows(idx_tile, u2):
    n_tok = idx_tile.shape[0]
    tm = COMBINE_TILE
    n_seg = D_MODEL // LANES
    return pl.pallas_call(
        _sort_body,
        name="sort_rows",
        grid=(n_tok // tm,),
        in_specs=[pl.BlockSpec((tm, LANES), lambda i: (i, 0)),
                  pl.BlockSpec((tm, D_MODEL), lambda i: (i, 0))],
        out_specs=[pl.BlockSpec((tm, LANES), lambda i: (i, 0)),
                   pl.BlockSpec((1, SUBLANES, LANES), lambda i: (i, 0, 0)),
                   pl.BlockSpec((tm * TOP_K * n_seg, LANES), lambda i: (i, 0))],
        out_shape=[jax.ShapeDtypeStruct((n_tok, LANES), F32),
                   jax.ShapeDtypeStruct((n_tok // tm, SUBLANES, LANES), F32),
                   jax.ShapeDtypeStruct((n_tok * TOP_K * n_seg, LANES), F32)],
        compiler_params=_cparams("parallel"),
    )(idx_tile, u2)


def _deinterleave_body(w_ref, o_ref):
    w = w_ref[0].astype(BF16)
    half = o_ref.shape[2] // 2
    src = lax.broadcasted_iota(I32, (2 * MXU_TILE, MXU_TILE), 0)
    dst = lax.broadcasted_iota(I32, (2 * MXU_TILE, MXU_TILE), 1)
    pick_even = jnp.where(src == 2 * dst, 1.0, 0.0).astype(BF16)
    pick_odd = jnp.where(src == 2 * dst + 1, 1.0, 0.0).astype(BF16)
    for j in range(half // MXU_TILE):
        slab = w[:, 2 * MXU_TILE * j:2 * MXU_TILE * (j + 1)]
        lo = MXU_TILE * j
        o_ref[0, :, lo:lo + MXU_TILE] = jnp.dot(
            slab, pick_even, preferred_element_type=F32).astype(BF16)
        o_ref[0, :, half + lo:half + lo + MXU_TILE] = jnp.dot(
            slab, pick_odd, preferred_element_type=F32).astype(BF16)


def _deinterleave_w1(w1):
    n_e, d_in, two_f = w1.shape
    tr = 512
    return pl.pallas_call(
        _deinterleave_body,
        name="deinterleave_w1",
        grid=(n_e, d_in // tr),
        in_specs=[pl.BlockSpec((1, tr, two_f), lambda e, r: (e, r, 0))],
        out_specs=pl.BlockSpec((1, tr, two_f), lambda e, r: (e, r, 0)),
        out_shape=jax.ShapeDtypeStruct((n_e, d_in, two_f), BF16),
        compiler_params=_cparams("parallel", "parallel"),
    )(w1)


def _expert_body(be_ref, nu_ref, ft_ref, lt_ref, nv_ref, rg_ref, rl_ref, rc_ref,
                 xl_ref, w1_ref, w2_ref, bg_ref, bl_ref, b2_ref, y_ref, xbuf_ref, sem):
    i = pl.program_id(0)
    n_used = nu_ref[0]
    bm = ROW_BLOCK
    n_seg = D_MODEL // LANES
    n_loc = COMBINE_TILE * TOP_K

    def issue(blk, slot):
        e = be_ref[blk]
        r0 = blk * bm

        def piece(t, carry):
            g = rg_ref[t * N_EXPERTS + e]
            s = jnp.maximum(g, r0)
            n = jnp.minimum(g + rc_ref[t * N_EXPERTS + e], r0 + bm) - s

            @pl.when(n > 0)
            def _():
                src = pl.multiple_of((t * n_loc + rl_ref[t * N_EXPERTS + e] + s - g) * n_seg, n_seg)
                dst = pl.multiple_of((s - r0) * n_seg, n_seg)
                pltpu.make_async_copy(xl_ref.at[pl.ds(src, n * n_seg)],
                                      xbuf_ref.at[slot, pl.ds(dst, n * n_seg)], sem.at[slot]).start()
            return carry

        lax.fori_loop(ft_ref[blk], lt_ref[blk] + 1, piece, 0)

    @pl.when(i == 0)
    def _():
        xbuf_ref[...] = jnp.zeros_like(xbuf_ref)
        issue(0, 0)

    @pl.when(i + 1 < n_used)
    def _():
        issue(i + 1, (i + 1) % 2)

    @pl.when(i >= n_used)
    def _():
        y_ref[...] = jnp.zeros_like(y_ref)

    @pl.when(i < n_used)
    def _():
        slot = i % 2
        n_valid = nv_ref[i] * n_seg
        pltpu.make_async_copy(xl_ref.at[pl.ds(0, n_valid)], xbuf_ref.at[slot, pl.ds(0, n_valid)],
                              sem.at[slot]).wait()
        x = jnp.concatenate([xbuf_ref[slot, pl.ds(j, bm, stride=n_seg), :] for j in range(n_seg)],
                            axis=1).astype(BF16)
        hb = jnp.dot(x, w1_ref[0], preferred_element_type=F32)
        hg = hb[:, :D_EXPERT] + bg_ref[0]
        hl = hb[:, D_EXPERT:] + bl_ref[0]
        xg = jnp.minimum(hg, SWIGLU_LIMIT)
        xl = jnp.clip(hl, -SWIGLU_LIMIT, SWIGLU_LIMIT)
        act = xg * _sigmoid(SWIGLU_ALPHA * xg) * (xl + 1.0)
        y = jnp.dot(act.astype(BF16), w2_ref[0], preferred_element_type=F32) + b2_ref[0]
        for j in range(n_seg):
            y_ref[pl.ds(j, bm, stride=n_seg), :] = y[:, j * LANES:(j + 1) * LANES]


def _experts(x_local, block_tables, run_tables, w1d, w2, b1g, b1l, b2, m_pad):
    bm = ROW_BLOCK
    n_seg = D_MODEL // LANES
    wsel = lambda i, be, *_: (be[i], 0, 0)
    grid_spec = pltpu.PrefetchScalarGridSpec(
        num_scalar_prefetch=8,
        grid=(m_pad // bm,),
        in_specs=[pl.BlockSpec(memory_space=pl.ANY),
                  pl.BlockSpec((1, D_MODEL, 2 * D_EXPERT), wsel),
                  pl.BlockSpec((1, D_EXPERT, D_MODEL), wsel),
                  pl.BlockSpec((1, 1, D_EXPERT), wsel),
                  pl.BlockSpec((1, 1, D_EXPERT), wsel),
                  pl.BlockSpec((1, 1, D_MODEL), wsel)],
        out_specs=pl.BlockSpec((bm * n_seg, LANES), lambda i, *_: (i, 0)),
        scratch_shapes=[pltpu.VMEM((2, bm * n_seg, LANES), F32), pltpu.SemaphoreType.DMA((2,))],
    )
    return pl.pallas_call(
        _expert_body,
        name="expert_mlp",
        grid_spec=grid_spec,
        out_shape=jax.ShapeDtypeStruct((m_pad * n_seg, LANES), F32),
        compiler_params=_cparams("arbitrary"),
    )(*block_tables, *run_tables, x_local, w1d, w2, b1g, b1l, b2)


def _combine_body(rg_ref, rl_ref, rc_ref, y_ref, lpos_ref, gate_ref, h_ref, gt_ref, fg_ref, o_ref,
                  yloc_ref, sem):
    i = pl.program_id(0)
    tm = h_ref.shape[0]
    n_seg = D_MODEL // LANES
    n_loc = yloc_ref.shape[1] // n_seg

    def issue(tile, slot):
        for e in range(N_EXPERTS):
            c = rc_ref[tile * N_EXPERTS + e] * n_seg

            @pl.when(c > 0)
            def _():
                src = pl.multiple_of(rg_ref[tile * N_EXPERTS + e] * n_seg, n_seg)
                dst = pl.multiple_of(rl_ref[tile * N_EXPERTS + e] * n_seg, n_seg)
                pltpu.make_async_copy(y_ref.at[pl.ds(src, c)], yloc_ref.at[slot, pl.ds(dst, c)],
                                      sem.at[slot]).start()

    @pl.when(i == 0)
    def _():
        issue(0, 0)

    @pl.when(i + 1 < pl.num_programs(0))
    def _():
        issue(i + 1, (i + 1) % 2)

    slot = i % 2
    pltpu.make_async_copy(y_ref.at[pl.ds(0, n_loc * n_seg)], yloc_ref.at[slot], sem.at[slot]).wait()

    y = jnp.concatenate([yloc_ref[slot, pl.ds(j, n_loc, stride=n_seg), :] for j in range(n_seg)],
                        axis=1).astype(BF16)
    q = lax.broadcasted_iota(I32, (tm, n_loc), 1)
    lpos = lpos_ref[...].astype(I32)
    gate = gate_ref[...]
    gmat = jnp.zeros((tm, n_loc), F32)
    for kq in range(TOP_K):
        gmat = jnp.where(q == lpos[:, kq:kq + 1], gate[:, kq:kq + 1], gmat)
    ghi, glo = _split2(gmat)
    moe = (jnp.dot(ghi, y, preferred_element_type=F32)
           + jnp.dot(glo, y, preferred_element_type=F32))
    h = h_ref[...] + gt_ref[0, 0] * moe
    o_ref[...] = _rms(h) * fg_ref[...]


def _combine(y_rows, run_start, run_local, run_count, lpos, gate_tile, h, mod6, final_g,
             tiles_per_batch):
    n_tok = h.shape[0]
    tm = COMBINE_TILE
    grid_spec = pltpu.PrefetchScalarGridSpec(
        num_scalar_prefetch=3,
        grid=(n_tok // tm,),
        in_specs=[pl.BlockSpec(memory_space=pl.ANY),
                  pl.BlockSpec((tm, LANES), lambda i, *_: (i, 0)),
                  pl.BlockSpec((tm, LANES), lambda i, *_: (i, 0)),
                  pl.BlockSpec((tm, D_MODEL), lambda i, *_: (i, 0)),
                  pl.BlockSpec((1, 1, 1, D_MODEL), lambda i, *_: (i // tiles_per_batch, 5, 0, 0)),
                  pl.BlockSpec((1, D_MODEL), lambda i, *_: (0, 0))],
        out_specs=pl.BlockSpec((tm, D_MODEL), lambda i, *_: (i, 0)),
        scratch_shapes=[pltpu.VMEM((2, tm * TOP_K * (D_MODEL // LANES), LANES), F32),
                        pltpu.SemaphoreType.DMA((2,))],
    )
    return pl.pallas_call(
        _combine_body,
        name="combine_rows",
        grid_spec=grid_spec,
        out_shape=jax.ShapeDtypeStruct((n_tok, D_MODEL), F32),
        compiler_params=_cparams("arbitrary"),
    )(run_start, run_local, run_count, y_rows, lpos, gate_tile, h, mod6, final_g)


def kernel(x, c, ada_w, ada_b, norm1_g, w_in, conv_w, conv_gn, rwkv_mu, rwkv_w0, rwkv_w_up,
           rwkv_a0, rwkv_a_up, rwkv_g_up, rwkv_k_k, rwkv_k_a, rwkv_r_k, rwkv_ln_w, rwkv_ln_b,
           w_out, norm2_g, router_w, router_b, exp_w1, exp_b1, exp_w2, exp_b2, final_g):
    bsz, seq, _ = x.shape
    depth = ada_w.shape[0]
    assert depth == 1, "the final norm is fused into the last layer's combine step"
    n_tok = bsz * seq
    assert seq % TOKEN_TILE == 0 and seq % CHUNK == 0 and seq % COMBINE_TILE == 0
    n_rows = n_tok * TOP_K
    assert n_rows % ROW_BLOCK == 0
    n_blocks = n_rows // ROW_BLOCK + N_EXPERTS
    m_pad = n_blocks * ROW_BLOCK

    h = x.reshape(n_tok, D_MODEL)
    out = h
    for l in range(depth):
        mod6 = _ada_mod(c, ada_w[l], ada_b[l]).reshape(bsz, 6, 1, D_MODEL)
        pc, pr = _in_proj(h, norm1_g[l].reshape(1, -1), mod6, w_in[l].astype(BF16),
                          seq // TOKEN_TILE)
        yc = _conv_mixer(pc, conv_w[l], conv_gn[l].reshape(1, -1), bsz, seq)
        yr = _rwkv_mixer(pr, rwkv_mu[l], rwkv_w0[l], rwkv_w_up[l], rwkv_a0[l], rwkv_a_up[l],
                         rwkv_g_up[l], rwkv_k_k[l], rwkv_k_a[l], rwkv_r_k[l], rwkv_ln_w[l],
                         rwkv_ln_b[l], bsz, seq)
        rw = jnp.pad(router_w[l], ((0, 0), (0, LANES - N_EXPERTS)))
        rw_hi = rw.astype(BF16)
        rw_lo = (rw - rw_hi.astype(F32)).astype(BF16)
        rb = jnp.pad(router_b[l], (0, LANES - N_EXPERTS)).reshape(1, LANES)
        h, u2, gate_tile, idx_tile = _out_proj_router(
            yc.reshape(n_tok, D_CONV), yr.reshape(n_tok, D_RWKV), h, w_out[l].astype(BF16), mod6,
            norm2_g[l].reshape(1, -1), rw_hi, rw_lo, rb, seq // TOKEN_TILE)
        lpos, tab, x_local = _sort_rows(idx_tile, u2)
        run_count = tab[:, 0, :N_EXPERTS].astype(I32)
        run_local = tab[:, 1, :N_EXPERTS].astype(I32)
        counts = jnp.sum(run_count, axis=0)
        padded = (counts + ROW_BLOCK - 1) // ROW_BLOCK * ROW_BLOCK
        pad_end = jnp.cumsum(padded)
        pad_start = pad_end - padded
        run_start = pad_start[None, :] + jnp.cumsum(run_count, axis=0) - run_count
        n_used = (pad_end[-1] // ROW_BLOCK).astype(I32).reshape(1)
        blk_row = jnp.minimum(jnp.arange(n_blocks, dtype=I32), n_used[0] - 1) * ROW_BLOCK
        block_expert = jnp.minimum(
            jnp.sum((pad_end[None, :] <= blk_row[:, None]).astype(I32), axis=1), N_EXPERTS - 1)
        blk_start = run_start.T[block_expert]
        blk_end = blk_start + run_count.T[block_expert]
        first_tile = jnp.sum((blk_end <= blk_row[:, None]).astype(I32), axis=1)
        last_tile = jnp.sum((blk_start < blk_row[:, None] + ROW_BLOCK).astype(I32), axis=1) - 1
        n_valid = jnp.clip((pad_start + counts)[block_expert] - blk_row, 0, ROW_BLOCK).astype(I32)
        run_tables = (run_start.reshape(-1), run_local.reshape(-1), run_count.reshape(-1))
        y_rows = _experts(x_local, (block_expert, n_used, first_tile, last_tile, n_valid), run_tables,
                          _deinterleave_w1(exp_w1[l]), exp_w2[l].astype(BF16),
                          exp_b1[l][:, None, 0::2], exp_b1[l][:, None, 1::2], exp_b2[l][:, None, :],
                          m_pad)
        out = _combine(y_rows, *run_tables, lpos, gate_tile, h, mod6, final_g.reshape(1, -1),
                       seq // COMBINE_TILE)
    return out.reshape(bsz, seq, D_MODEL)
```

```python
import jax
import jax.numpy as jnp
from jax import lax
from jax.experimental import pallas as pl
from jax.experimental.pallas import tpu as pltpu

F32 = jnp.float32
BF16 = jnp.bfloat16
I32 = jnp.int32

D_MODEL = 1024
D_CONV = 512
CONV_GROUP_DIM = 64
D_RWKV = 512
HEAD_DIM = 64
N_HEADS = D_RWKV // HEAD_DIM
DECAY_LORA = 64
AAA_LORA = 64
GATE_LORA = 128
COLS_CONV = 3 * D_CONV
COLS_RWKV = 3 * D_RWKV + DECAY_LORA + AAA_LORA + GATE_LORA
N_EXPERTS = 32
TOP_K = 4
D_EXPERT = D_MODEL
SWIGLU_LIMIT = 7.0
SWIGLU_ALPHA = 1.702
NORM_EPS = 1e-5
GN_EPS = 64e-5

LANES = 128
SUBLANES = 8
MXU_TILE = 256
CHUNK = 64
RWKV_SEQS_PER_STEP = 4
RWKV_STAGE_LAG = 5
ROW_BLOCK = 512
TOKEN_TILE = 512
COMBINE_TILE = 256
SORT_TILES_PER_STEP = 2
COMBINE_TILES_PER_STEP = 2
VMEM_LIMIT = 48 * 1024 * 1024


def _cparams(*sem):
    return pltpu.CompilerParams(dimension_semantics=sem, vmem_limit_bytes=VMEM_LIMIT)


def _mm(a, b):
    return jnp.dot(a.astype(BF16), b.astype(BF16), preferred_element_type=F32)


def _mm_nt(a, b):
    return lax.dot_general(a.astype(BF16), b.astype(BF16), (((1,), (1,)), ((), ())),
                           preferred_element_type=F32)


def _mm_tn(a, b):
    return lax.dot_general(a.astype(BF16), b.astype(BF16), (((0,), (0,)), ((), ())),
                           preferred_element_type=F32)


def _split2(x):
    hi = x.astype(BF16)
    lo = (x - hi.astype(F32)).astype(BF16)
    return hi, lo


def _split3(x):
    hi = x.astype(BF16)
    r = x - hi.astype(F32)
    mid = r.astype(BF16)
    lo = (r - mid.astype(F32)).astype(BF16)
    return hi, mid, lo


def _sigmoid(x):
    return 1.0 / (1.0 + jnp.exp(-x))


def _rms(x):
    return x * lax.rsqrt(jnp.mean(x * x, axis=-1, keepdims=True) + NORM_EPS)


def _ada_body(c_ref, w_ref, b_ref, o_ref):
    c = c_ref[...]
    ca = c * _sigmoid(c)
    ah, al = _split2(ca)
    wh, wl = _split2(w_ref[...])
    acc = jnp.dot(ah, wh, preferred_element_type=F32)
    acc += jnp.dot(ah, wl, preferred_element_type=F32)
    acc += jnp.dot(al, wh, preferred_element_type=F32)
    o_ref[...] = acc + b_ref[...]


def _ada_mod(c, ada_w, ada_b):
    bsz = c.shape[0]
    n = ada_w.shape[1]
    tn = 1024
    return pl.pallas_call(
        _ada_body,
        name="ada_mod",
        grid=(n // tn,),
        in_specs=[pl.BlockSpec((bsz, D_MODEL), lambda j: (0, 0)),
                  pl.BlockSpec((D_MODEL, tn), lambda j: (0, j)),
                  pl.BlockSpec((1, tn), lambda j: (0, j))],
        out_specs=pl.BlockSpec((bsz, tn), lambda j: (0, j)),
        out_shape=jax.ShapeDtypeStruct((bsz, n), F32),
        compiler_params=_cparams("parallel"),
    )(c, ada_w, ada_b.reshape(1, n))


def _inproj_body(x_ref, g_ref, sh_ref, sc_ref, w_ref, oc_ref, or_ref):
    u = _rms(x_ref[...]) * g_ref[...]
    u = u * (1.0 + sc_ref[0, 0]) + sh_ref[0, 0]
    acc = jnp.dot(u.astype(BF16), w_ref[...], preferred_element_type=F32)
    oc_ref[...] = acc[:, :COLS_CONV]
    or_ref[...] = acc[:, COLS_CONV:]


def _in_proj(xf, g1, mod6, w_in_bf, tiles_per_batch):
    n_tok = xf.shape[0]
    tm = TOKEN_TILE
    n_in = w_in_bf.shape[1]
    return pl.pallas_call(
        _inproj_body,
        name="in_proj",
        grid=(n_tok // tm,),
        in_specs=[pl.BlockSpec((tm, D_MODEL), lambda i: (i, 0)),
                  pl.BlockSpec((1, D_MODEL), lambda i: (0, 0)),
                  pl.BlockSpec((1, 1, 1, D_MODEL), lambda i: (i // tiles_per_batch, 0, 0, 0)),
                  pl.BlockSpec((1, 1, 1, D_MODEL), lambda i: (i // tiles_per_batch, 1, 0, 0)),
                  pl.BlockSpec((D_MODEL, n_in), lambda i: (0, 0))],
        out_specs=[pl.BlockSpec((tm, COLS_CONV), lambda i: (i, 0)),
                   pl.BlockSpec((tm, COLS_RWKV), lambda i: (i, 0))],
        out_shape=[jax.ShapeDtypeStruct((n_tok, COLS_CONV), F32),
                   jax.ShapeDtypeStruct((n_tok, COLS_RWKV), F32)],
        compiler_params=_cparams("parallel"),
    )(xf, g1, mod6, mod6, w_in_bf)


def _conv_body(p_ref, cw_ref, gn_ref, o_ref, carry_ref):
    @pl.when(pl.program_id(1) == 0)
    def _():
        carry_ref[...] = jnp.zeros_like(carry_ref)

    p = p_ref[0]
    tc = p.shape[0]
    b_gate = p[:, :D_CONV]
    u = p[:, D_CONV:2 * D_CONV] * p[:, 2 * D_CONV:]
    ext = jnp.concatenate([carry_ref[...], u], axis=0)
    u1 = pltpu.roll(ext, 1, axis=0)[SUBLANES:]
    u2 = pltpu.roll(ext, 2, axis=0)[SUBLANES:]
    carry_ref[...] = u[tc - SUBLANES:, :]
    cw = cw_ref[...]
    y = b_gate * (cw[0:1] * u2 + cw[1:2] * u1 + cw[2:3] * u)
    outs = []
    for g in range(D_CONV // CONV_GROUP_DIM):
        outs.append(_rms(y[:, g * CONV_GROUP_DIM:(g + 1) * CONV_GROUP_DIM]))
    o_ref[0] = jnp.concatenate(outs, axis=-1) * gn_ref[...]


def _conv_mixer(pc, conv_w, conv_gn, bsz, seq):
    tc = TOKEN_TILE
    return pl.pallas_call(
        _conv_body,
        name="conv_mixer",
        grid=(bsz, seq // tc),
        in_specs=[pl.BlockSpec((1, tc, COLS_CONV), lambda b, i: (b, i, 0)),
                  pl.BlockSpec((3, D_CONV), lambda b, i: (0, 0)),
                  pl.BlockSpec((1, D_CONV), lambda b, i: (0, 0))],
        out_specs=pl.BlockSpec((1, tc, D_CONV), lambda b, i: (b, i, 0)),
        out_shape=jax.ShapeDtypeStruct((bsz, seq, D_CONV), F32),
        scratch_shapes=[pltpu.VMEM((SUBLANES, D_CONV), F32)],
        compiler_params=_cparams("parallel", "arbitrary"),
    )(pc.reshape(bsz, seq, COLS_CONV), conv_w, conv_gn)


def _rwkv_body(p_ref, mu_ref, w0_ref, wup_ref, a0_ref, aup_ref, gup_ref, kkw_ref, kaw_ref,
               rkw_ref, lnw_ref, lnb_ref, o_ref, state_ref, prev_ref):
    @pl.when(pl.program_id(1) == 0)
    def _():
        state_ref[...] = jnp.zeros_like(state_ref)
        prev_ref[...] = jnp.zeros_like(prev_ref)

    C = CHUNK
    N = HEAD_DIM
    nb = p_ref.shape[0]
    s1, s2, s3 = D_RWKV, 2 * D_RWKV, 3 * D_RWKV
    row1 = lax.broadcasted_iota(I32, (C, 1), 0)
    ri = lax.broadcasted_iota(I32, (C, C), 0)
    ci = lax.broadcasted_iota(I32, (C, C), 1)
    strict = ri > ci
    incl = ri >= ci
    eye = jnp.where(ri == ci, 1.0, 0.0).astype(F32)
    li = lax.broadcasted_iota(I32, (LANES, LANES), 0)
    lj = lax.broadcasted_iota(I32, (LANES, LANES), 1)
    seg_ones = jnp.where((li < N) == (lj < N), 1.0, 0.0).astype(BF16)
    tri = jnp.where(incl, 1.0, 0.0).astype(BF16)

    def stages(bb):
        units = []
        P = p_ref[bb]
        Pprev = jnp.where(row1 == 0, prev_ref[bb, 0:1, :], pltpu.roll(P, 1, axis=0))
        prev_ref[bb] = jnp.broadcast_to(P[C - 1:C, :], prev_ref.shape[1:])
        p = P + (Pprev - P) * mu_ref[...]
        r = p[:, :s1]
        k = p[:, s1:s2]
        v = p[:, s2:s3]
        dw = p[:, s3:s3 + DECAY_LORA]
        da = p[:, s3 + DECAY_LORA:s3 + DECAY_LORA + AAA_LORA]
        dg = p[:, s3 + DECAY_LORA + AAA_LORA:]
        z = -(w0_ref[...] + _mm(jnp.tanh(dw), wup_ref[...]))
        softplus = jnp.maximum(z, 0.0) + jnp.log(1.0 + jnp.exp(-jnp.abs(z)))
        logw = -jnp.exp(-softplus - 0.5)
        a = _sigmoid(a0_ref[...] + _mm(da, aup_ref[...]))
        g = _mm(_sigmoid(dg), gup_ref[...])
        kk = k * kkw_ref[...]
        kmod = k * (1.0 + (a - 1.0) * kaw_ref[...])
        cum = sum(jnp.dot(tri, part, preferred_element_type=F32) for part in _split3(logw))
        gam = jnp.exp(cum)
        gam_prev = jnp.exp(cum - logw)
        ginv = jnp.exp(-cum)
        def head_sum(x):
            hi, lo = _split2(x)
            cols = []
            for t in range(D_RWKV // LANES):
                ts = slice(t * LANES, (t + 1) * LANES)
                cols.append(jnp.dot(hi[:, ts], seg_ones, preferred_element_type=F32)
                            + jnp.dot(lo[:, ts], seg_ones, preferred_element_type=F32))
            return jnp.concatenate(cols, axis=1)

        kkn = kk * lax.rsqrt(jnp.maximum(head_sum(kk * kk), 1e-24))
        bonus_v = head_sum(r * kmod * rkw_ref[...]) * v
        at_all = (-kkn * gam_prev).astype(BF16)
        rt_all = (r * gam).astype(BF16)
        bt_all = (kkn * a * ginv).astype(BF16)
        kt_all = (kmod * ginv).astype(BF16)
        vb_all = v.astype(BF16)
        for h in range(N_HEADS):
            sl = slice(h * N, (h + 1) * N)
            At, Rt = at_all[:, sl], rt_all[:, sl]
            units.append(dict(
                slot=bb * N_HEADS + h, At=At, Rt=Rt, vb=vb_all[:, sl],
                BK=jnp.concatenate([bt_all[:, sl], kt_all[:, sl]], axis=0),
                AR=jnp.concatenate([At, Rt], axis=0), glast=gam[C - 1:C, sl]))
        yield
        for u in units:
            M = _mm_nt(u["AR"], u["BK"])
            lab = jnp.where(strict, M[:C, :C], 0.0)
            u["T"] = eye + lab
            u["Lp"] = lab.astype(BF16)
            u["Lak"] = jnp.where(strict, M[:C, C:], 0.0).astype(BF16)
            u["Tr"] = jnp.concatenate([jnp.where(incl, M[C:, :C], 0.0),
                                       jnp.where(incl, M[C:, C:], 0.0)], axis=1).astype(BF16)
        yield
        for u in units:
            u["X2"] = _mm(u["Lak"], u["vb"])
            u["Lp"] = _mm(u["Lp"], u["Lp"]).astype(BF16)
        yield
        for _ in range(4):
            for u in units:
                u["T"] = u["T"] + _mm(u["T"], u["Lp"])
                u["Lp"] = _mm(u["Lp"], u["Lp"]).astype(BF16)
            yield
        for u in units:
            u["T"] = u["T"] + _mm(u["T"], u["Lp"])
        yield
        for u in units:
            u["W"] = _mm(u["T"], jnp.concatenate([u["At"], u["X2"].astype(BF16)], axis=1))
        yield
        for u in units:
            u["S0"] = state_ref[u["slot"]]
            u["S0b"] = u["S0"].astype(BF16)
            U = _mm_nt(u["W"][:, :N], u["S0b"]) + u["W"][:, N:]
            u["UV"] = jnp.concatenate([U.astype(BF16), u["vb"]], axis=0)
        yield
        for u in units:
            u["O"] = _mm_nt(u["Rt"], u["S0b"]) + _mm(u["Tr"], u["UV"])
            state_ref[u["slot"]] = (u["S0"] + _mm_tn(u["UV"], u["BK"])) * u["glast"]
        yield
        o_all = jnp.concatenate([u["O"] for u in units], axis=-1)
        cen = o_all - head_sum(o_all) * (1.0 / N)
        var = head_sum(cen * cen) * (1.0 / N)
        on = cen * lax.rsqrt(var + GN_EPS) * lnw_ref[...] + lnb_ref[...]
        o_ref[bb] = (on + bonus_v) * g
        yield

    live = [(bb, stages(bb)) for bb in range(nb)]
    tick = 0
    while live:
        for item in list(live):
            if tick >= RWKV_STAGE_LAG * item[0] and next(item[1], "done") == "done":
                live.remove(item)
        tick += 1


def _rwkv_mixer(pr, mu, w0, w_up, a0, a_up, g_up, k_k, k_a, r_k, ln_w, ln_b, bsz, seq):
    C = CHUNK
    nb = RWKV_SEQS_PER_STEP if bsz % RWKV_SEQS_PER_STEP == 0 else 1
    row = lambda n: pl.BlockSpec((1, n), lambda b, i: (0, 0))
    full = lambda m, n: pl.BlockSpec((m, n), lambda b, i: (0, 0))
    return pl.pallas_call(
        _rwkv_body,
        name="rwkv_mixer",
        grid=(bsz // nb, seq // C),
        in_specs=[pl.BlockSpec((nb, C, COLS_RWKV), lambda b, i: (b, i, 0)),
                  row(COLS_RWKV), row(D_RWKV), full(DECAY_LORA, D_RWKV), row(D_RWKV),
                  full(AAA_LORA, D_RWKV), full(GATE_LORA, D_RWKV), row(D_RWKV), row(D_RWKV),
                  row(D_RWKV), row(D_RWKV), row(D_RWKV)],
        out_specs=pl.BlockSpec((nb, C, D_RWKV), lambda b, i: (b, i, 0)),
        out_shape=jax.ShapeDtypeStruct((bsz, seq, D_RWKV), F32),
        scratch_shapes=[pltpu.VMEM((nb * N_HEADS, HEAD_DIM, HEAD_DIM), F32),
                        pltpu.VMEM((nb, SUBLANES, COLS_RWKV), F32)],
        compiler_params=_cparams("parallel", "arbitrary"),
    )(pr.reshape(bsz, seq, COLS_RWKV), mu.reshape(1, -1), w0.reshape(1, -1), w_up,
      a0.reshape(1, -1), a_up, g_up, k_k.reshape(1, -1), k_a.reshape(1, -1), r_k.reshape(1, -1),
      ln_w.reshape(1, -1), ln_b.reshape(1, -1))


def _outproj_body(yc_ref, yr_ref, x_ref, wo_ref, gt_ref, g2_ref, sh_ref, sc_ref, rwh_ref, rwl_ref,
                  rb_ref, h_ref, u_ref, gate_ref, idx_ref):
    mix =_mm(yc_ref[...], wo_ref[:D_CONV, :]) + _mm(yr_ref[...], wo_ref[D_CONV:, :])
    h = x_ref[...] + gt_ref[0, 0] * mix
    h_ref[...] = h
    u2 = _rms(h) * g2_ref[...]
    u2 = u2 * (1.0 + sc_ref[0, 0]) + sh_ref[0, 0]
    u_ref[...] = u2

    uh, ul = _split2(u2)
    logits = (jnp.dot(uh, rwh_ref[...], preferred_element_type=F32)
              + jnp.dot(uh, rwl_ref[...], preferred_element_type=F32)
              + jnp.dot(ul, rwh_ref[...], preferred_element_type=F32)) + rb_ref[...]
    tm = logits.shape[0]
    lane = lax.broadcasted_iota(I32, (tm, LANES), 1)
    neg = jnp.float32(-jnp.inf)
    l = jnp.where(lane < N_EXPERTS, logits, neg)
    tops, idxs = [], []
    lane_f = lane.astype(F32)
    for _ in range(TOP_K):
        m = jnp.max(l, axis=-1, keepdims=True)
        ix = jnp.min(jnp.where(l == m, lane_f, float(LANES)), axis=-1, keepdims=True).astype(I32)
        tops.append(m)
        idxs.append(ix)
        l = jnp.where(lane == ix, neg, l)
    es = [jnp.exp(t - tops[0]) for t in tops]
    den = es[0] + es[1] + es[2] + es[3]
    gate_tile = jnp.zeros((tm, LANES), F32)
    idx_tile = jnp.zeros((tm, LANES), I32)
    for kq in range(TOP_K):
        gate_tile = jnp.where(lane == kq, es[kq] / den, gate_tile)
        idx_tile = jnp.where(lane == kq, idxs[kq], idx_tile)
    gate_ref[...] = gate_tile
    idx_ref[...] = idx_tile


def _out_proj_router(yc, yr, xf, w_out_bf, mod6, g2, rw_hi, rw_lo, rb, tiles_per_batch):
    n_tok = xf.shape[0]
    tm = TOKEN_TILE
    tok = lambda n: pl.BlockSpec((tm, n), lambda i: (i, 0))
    modspec = lambda which: pl.BlockSpec((1, 1, 1, D_MODEL),
                                         lambda i: (i // tiles_per_batch, which, 0, 0))
    const = lambda m, n: pl.BlockSpec((m, n), lambda i: (0, 0))
    return pl.pallas_call(
        _outproj_body,
        name="out_proj_router",
        grid=(n_tok // tm,),
        in_specs=[tok(D_CONV), tok(D_RWKV), tok(D_MODEL), const(D_MODEL, D_MODEL),
                  modspec(2), const(1, D_MODEL), modspec(3), modspec(4),
                  const(D_MODEL, LANES), const(D_MODEL, LANES), const(1, LANES)],
        out_specs=[tok(D_MODEL), tok(D_MODEL), tok(LANES), tok(LANES)],
        out_shape=[jax.ShapeDtypeStruct((n_tok, D_MODEL), F32),
                   jax.ShapeDtypeStruct((n_tok, D_MODEL), F32),
                   jax.ShapeDtypeStruct((n_tok, LANES), F32),
                   jax.ShapeDtypeStruct((n_tok, LANES), I32)],
        compiler_params=_cparams("parallel"),
    )(yc, yr, xf, w_out_bf, mod6, g2, mod6, mod6, rw_hi, rw_lo, rb)


def _sort_body(idx_ref, u_ref, lpos_ref, tab_ref, xl_ref):
    tm = COMBINE_TILE
    n_sub = idx_ref.shape[0] // tm
    n_loc = tm * TOP_K
    n_seg = D_MODEL // LANES
    rr = lax.broadcasted_iota(I32, (tm, tm), 0)
    cc = lax.broadcasted_iota(I32, (tm, tm), 1)
    tri = jnp.where(rr > cc, 1.0, 0.0).astype(BF16)
    lane = lax.broadcasted_iota(I32, (tm, LANES), 1)
    lane8 = lax.broadcasted_iota(I32, (SUBLANES, LANES), 1)
    row8 = lax.broadcasted_iota(I32, (SUBLANES, LANES), 0)
    q = lax.broadcasted_iota(I32, (n_loc, tm), 0)
    subs = [dict(tok=slice(s * tm, (s + 1) * tm)) for s in range(n_sub)]

    for sb in subs:
        idx = idx_ref[sb["tok"], :]
        sb["masks"] = [lane == idx[:, kq:kq + 1] for kq in range(TOP_K)]
        sel = jnp.zeros((tm, LANES), F32)
        for mk in sb["masks"]:
            sel = sel + jnp.where(mk, 1.0, 0.0)
        sb["sel"] = sel
    for sb in subs:
        sb["prefix"] = jnp.dot(tri, sb["sel"].astype(BF16), preferred_element_type=F32)
    for s, sb in enumerate(subs):
        cnt = jnp.broadcast_to(jnp.sum(sb["sel"], axis=0, keepdims=True), (SUBLANES, LANES))
        inc = cnt
        for sh in (1, 2, 4, 8, 16, 32, 64):
            inc = inc + jnp.where(lane8 >= sh, pltpu.roll(inc, sh, axis=1), 0.0)
        loff = inc - cnt
        tab_ref[s] = jnp.where(row8 == 0, cnt, loff)
        local_row = loff[0:1, :] + sb["prefix"]
        lpos_tile = jnp.zeros((tm, LANES), F32)
        for kq in range(TOP_K):
            lk = jnp.sum(jnp.where(sb["masks"][kq], local_row, 0.0), axis=-1, keepdims=True)
            lpos_tile = jnp.where(lane == kq, lk, lpos_tile)
        lpos_ref[sb["tok"], :] = lpos_tile
        sb["lpos_t"] = jnp.transpose(lpos_tile)[0:SUBLANES, :].astype(I32)
    for sb in subs:
        pick = jnp.zeros((n_loc, tm), F32)
        for kq in range(TOP_K):
            pick = jnp.where(q == sb["lpos_t"][kq:kq + 1, :], 1.0, pick)
        sb["xl"] = jnp.dot(pick.astype(BF16), u_ref[sb["tok"], :].astype(BF16),
                           preferred_element_type=F32)
    for s, sb in enumerate(subs):
        for j in range(n_seg):
            xl_ref[pl.ds(s * n_loc * n_seg + j, n_loc, stride=n_seg), :] = (
                sb["xl"][:, j * LANES:(j + 1) * LANES])


def _sort_rows(idx_tile, u2):
    n_tok = idx_tile.shape[0]
    tm = COMBINE_TILE
    ts = tm * SORT_TILES_PER_STEP
    n_seg = D_MODEL // LANES
    return pl.pallas_call(
        _sort_body,
        name="sort_rows",
        grid=(n_tok // ts,),
        in_specs=[pl.BlockSpec((ts, LANES), lambda i: (i, 0)),
                  pl.BlockSpec((ts, D_MODEL), lambda i: (i, 0))],
        out_specs=[pl.BlockSpec((ts, LANES), lambda i: (i, 0)),
                   pl.BlockSpec((SORT_TILES_PER_STEP, SUBLANES, LANES), lambda i: (i, 0, 0)),
                   pl.BlockSpec((ts * TOP_K * n_seg, LANES), lambda i: (i, 0))],
        out_shape=[jax.ShapeDtypeStruct((n_tok, LANES), F32),
                   jax.ShapeDtypeStruct((n_tok // tm, SUBLANES, LANES), F32),
                   jax.ShapeDtypeStruct((n_tok * TOP_K * n_seg, LANES), F32)],
        compiler_params=_cparams("parallel"),
    )(idx_tile, u2)


def _deinterleave_body(w_ref, o_ref):
    w = w_ref[0].astype(BF16)
    half = o_ref.shape[2] // 2
    src = lax.broadcasted_iota(I32, (2 * MXU_TILE, MXU_TILE), 0)
    dst = lax.broadcasted_iota(I32, (2 * MXU_TILE, MXU_TILE), 1)
    pick_even = jnp.where(src == 2 * dst, 1.0, 0.0).astype(BF16)
    pick_odd = jnp.where(src == 2 * dst + 1, 1.0, 0.0).astype(BF16)
    for j in range(half // MXU_TILE):
        slab = w[:, 2 * MXU_TILE * j:2 * MXU_TILE * (j + 1)]
        lo = MXU_TILE * j
        o_ref[0, :, lo:lo + MXU_TILE] = jnp.dot(
            slab, pick_even, preferred_element_type=F32).astype(BF16)
        o_ref[0, :, half + lo:half + lo + MXU_TILE] = jnp.dot(
            slab, pick_odd, preferred_element_type=F32).astype(BF16)


def _deinterleave_w1(w1):
    n_e, d_in, two_f = w1.shape
    tr = 512
    return pl.pallas_call(
        _deinterleave_body,
        name="deinterleave_w1",
        grid=(n_e, d_in // tr),
        in_specs=[pl.BlockSpec((1, tr, two_f), lambda e, r: (e, r, 0))],
        out_specs=pl.BlockSpec((1, tr, two_f), lambda e, r: (e, r, 0)),
        out_shape=jax.ShapeDtypeStruct((n_e, d_in, two_f), BF16),
        compiler_params=_cparams("parallel", "parallel"),
    )(w1)


def _expert_body(be_ref, nu_ref, ft_ref, lt_ref, nv_ref, rg_ref, rl_ref, rc_ref,
                 xl_ref, w1_ref, w2_ref, bg_ref, bl_ref, b2_ref, y_ref, xbuf_ref, sem):
    i = pl.program_id(0)
    n_used = nu_ref[0]
    bm = ROW_BLOCK
    n_seg = D_MODEL // LANES
    n_loc = COMBINE_TILE * TOP_K

    def issue(blk, slot):
        e = be_ref[blk]
        r0 = blk * bm

        def piece(t, carry):
            g = rg_ref[t * N_EXPERTS + e]
            s = jnp.maximum(g, r0)
            n = jnp.minimum(g + rc_ref[t * N_EXPERTS + e], r0 + bm) - s

            @pl.when(n > 0)
            def _():
                src = pl.multiple_of((t * n_loc + rl_ref[t * N_EXPERTS + e] + s - g) * n_seg, n_seg)
                dst = pl.multiple_of((s - r0) * n_seg, n_seg)
                pltpu.make_async_copy(xl_ref.at[pl.ds(src, n * n_seg)],
                                      xbuf_ref.at[slot, pl.ds(dst, n * n_seg)], sem.at[slot]).start()
            return carry

        lax.fori_loop(ft_ref[blk], lt_ref[blk] + 1, piece, 0)

    @pl.when(i == 0)
    def _():
        xbuf_ref[...] = jnp.zeros_like(xbuf_ref)
        issue(0, 0)

    @pl.when(i + 1 < n_used)
    def _():
        issue(i + 1, (i + 1) % 2)

    @pl.when(i >= n_used)
    def _():
        y_ref[...] = jnp.zeros_like(y_ref)

    @pl.when(i < n_used)
    def _():
        slot = i % 2
        n_valid = nv_ref[i] * n_seg
        pltpu.make_async_copy(xl_ref.at[pl.ds(0, n_valid)], xbuf_ref.at[slot, pl.ds(0, n_valid)],
                              sem.at[slot]).wait()
        x = jnp.concatenate([xbuf_ref[slot, pl.ds(j, bm, stride=n_seg), :] for j in range(n_seg)],
                            axis=1).astype(BF16)
        hb = jnp.dot(x, w1_ref[0], preferred_element_type=F32)
        hg = hb[:, :D_EXPERT] + bg_ref[0]
        hl = hb[:, D_EXPERT:] + bl_ref[0]
        xg = jnp.minimum(hg, SWIGLU_LIMIT)
        xl = jnp.clip(hl, -SWIGLU_LIMIT, SWIGLU_LIMIT)
        act = xg * _sigmoid(SWIGLU_ALPHA * xg) * (xl + 1.0)
        y = jnp.dot(act.astype(BF16), w2_ref[0], preferred_element_type=F32) + b2_ref[0]
        for j in range(n_seg):
            y_ref[pl.ds(j, bm, stride=n_seg), :] = y[:, j * LANES:(j + 1) * LANES]


def _experts(x_local, block_tables, run_tables, w1d, w2, b1g, b1l, b2, m_pad):
    bm = ROW_BLOCK
    n_seg = D_MODEL // LANES
    wsel = lambda i, be, *_: (be[i], 0, 0)
    grid_spec = pltpu.PrefetchScalarGridSpec(
        num_scalar_prefetch=8,
        grid=(m_pad // bm,),
        in_specs=[pl.BlockSpec(memory_space=pl.ANY),
                  pl.BlockSpec((1, D_MODEL, 2 * D_EXPERT), wsel),
                  pl.BlockSpec((1, D_EXPERT, D_MODEL), wsel),
                  pl.BlockSpec((1, 1, D_EXPERT), wsel),
                  pl.BlockSpec((1, 1, D_EXPERT), wsel),
                  pl.BlockSpec((1, 1, D_MODEL), wsel)],
        out_specs=pl.BlockSpec((bm * n_seg, LANES), lambda i, *_: (i, 0)),
        scratch_shapes=[pltpu.VMEM((2, bm * n_seg, LANES), F32), pltpu.SemaphoreType.DMA((2,))],
    )
    return pl.pallas_call(
        _expert_body,
        name="expert_mlp",
        grid_spec=grid_spec,
        out_shape=jax.ShapeDtypeStruct((m_pad * n_seg, LANES), F32),
        compiler_params=_cparams("arbitrary"),
    )(*block_tables, *run_tables, x_local, w1d, w2, b1g, b1l, b2)


def _combine_body(rg_ref, rl_ref, rc_ref, y_ref, lpos_ref, gate_ref, h_ref, gt_ref, fg_ref, o_ref,
                  yloc_ref, sem):
    i = pl.program_id(0)
    tm = COMBINE_TILE
    n_sub = h_ref.shape[0] // tm
    n_seg = D_MODEL // LANES
    n_loc = tm * TOP_K

    def issue(step, slot):
        for s in range(n_sub):
            tile = step * n_sub + s
            for e in range(N_EXPERTS):
                c = rc_ref[tile * N_EXPERTS + e] * n_seg

                @pl.when(c > 0)
                def _():
                    src = pl.multiple_of(rg_ref[tile * N_EXPERTS + e] * n_seg, n_seg)
                    dst = pl.multiple_of((s * n_loc + rl_ref[tile * N_EXPERTS + e]) * n_seg, n_seg)
                    pltpu.make_async_copy(y_ref.at[pl.ds(src, c)], yloc_ref.at[slot, pl.ds(dst, c)],
                                          sem.at[slot]).start()

    @pl.when(i == 0)
    def _():
        issue(0, 0)

    @pl.when(i + 1 < pl.num_programs(0))
    def _():
        issue(i + 1, (i + 1) % 2)

    slot = i % 2
    pltpu.make_async_copy(y_ref.at[pl.ds(0, n_sub * n_loc * n_seg)], yloc_ref.at[slot],
                          sem.at[slot]).wait()

    q = lax.broadcasted_iota(I32, (tm, n_loc), 1)
    subs = [dict(tok=slice(s * tm, (s + 1) * tm)) for s in range(n_sub)]
    for s, sb in enumerate(subs):
        sb["y"] = jnp.concatenate(
            [yloc_ref[slot, pl.ds(s * n_loc * n_seg + j, n_loc, stride=n_seg), :]
             for j in range(n_seg)], axis=1).astype(BF16)
        lpos = lpos_ref[sb["tok"], :].astype(I32)
        gate = gate_ref[sb["tok"], :]
        gmat = jnp.zeros((tm, n_loc), F32)
        for kq in range(TOP_K):
            gmat = jnp.where(q == lpos[:, kq:kq + 1], gate[:, kq:kq + 1], gmat)
        sb["ghi"], sb["glo"] = _split2(gmat)
    for sb in subs:
        sb["moe"] = (jnp.dot(sb["ghi"], sb["y"], preferred_element_type=F32)
                     + jnp.dot(sb["glo"], sb["y"], preferred_element_type=F32))
    for sb in subs:
        h = h_ref[sb["tok"], :] + gt_ref[0, 0] * sb["moe"]
        o_ref[sb["tok"], :] = _rms(h) * fg_ref[...]


def _combine(y_rows, run_start, run_local, run_count, lpos, gate_tile, h, mod6, final_g,
             tiles_per_batch):
    n_tok = h.shape[0]
    tm = COMBINE_TILE * COMBINE_TILES_PER_STEP
    grid_spec = pltpu.PrefetchScalarGridSpec(
        num_scalar_prefetch=3,
        grid=(n_tok // tm,),
        in_specs=[pl.BlockSpec(memory_space=pl.ANY),
                  pl.BlockSpec((tm, LANES), lambda i, *_: (i, 0)),
                  pl.BlockSpec((tm, LANES), lambda i, *_: (i, 0)),
                  pl.BlockSpec((tm, D_MODEL), lambda i, *_: (i, 0)),
                  pl.BlockSpec((1, 1, 1, D_MODEL), lambda i, *_: (i // tiles_per_batch, 5, 0, 0)),
                  pl.BlockSpec((1, D_MODEL), lambda i, *_: (0, 0))],
        out_specs=pl.BlockSpec((tm, D_MODEL), lambda i, *_: (i, 0)),
        scratch_shapes=[pltpu.VMEM((2, tm * TOP_K * (D_MODEL // LANES), LANES), F32),
                        pltpu.SemaphoreType.DMA((2,))],
    )
    return pl.pallas_call(
        _combine_body,
        name="combine_rows",
        grid_spec=grid_spec,
        out_shape=jax.ShapeDtypeStruct((n_tok, D_MODEL), F32),
        compiler_params=_cparams("arbitrary"),
    )(run_start, run_local, run_count, y_rows, lpos, gate_tile, h, mod6, final_g)


def kernel(x, c, ada_w, ada_b, norm1_g, w_in, conv_w, conv_gn, rwkv_mu, rwkv_w0, rwkv_w_up,
           rwkv_a0, rwkv_a_up, rwkv_g_up, rwkv_k_k, rwkv_k_a, rwkv_r_k, rwkv_ln_w, rwkv_ln_b,
           w_out, norm2_g, router_w, router_b, exp_w1, exp_b1, exp_w2, exp_b2, final_g):
    bsz, seq, _ = x.shape
    depth = ada_w.shape[0]
    assert depth == 1, "the final norm is fused into the last layer's combine step"
    n_tok = bsz * seq
    assert seq % TOKEN_TILE == 0 and seq % CHUNK == 0 and seq % COMBINE_TILE == 0
    n_rows = n_tok * TOP_K
    assert n_rows % ROW_BLOCK == 0
    n_blocks = n_rows // ROW_BLOCK + N_EXPERTS
    m_pad = n_blocks * ROW_BLOCK

    h = x.reshape(n_tok, D_MODEL)
    out = h
    for l in range(depth):
        mod6 = _ada_mod(c, ada_w[l], ada_b[l]).reshape(bsz, 6, 1, D_MODEL)
        pc, pr = _in_proj(h, norm1_g[l].reshape(1, -1), mod6, w_in[l].astype(BF16),
                          seq // TOKEN_TILE)
        yc = _conv_mixer(pc, conv_w[l], conv_gn[l].reshape(1, -1), bsz, seq)
        yr = _rwkv_mixer(pr, rwkv_mu[l], rwkv_w0[l], rwkv_w_up[l], rwkv_a0[l], rwkv_a_up[l],
                         rwkv_g_up[l], rwkv_k_k[l], rwkv_k_a[l], rwkv_r_k[l], rwkv_ln_w[l],
                         rwkv_ln_b[l], bsz, seq)
        rw = jnp.pad(router_w[l], ((0, 0), (0, LANES - N_EXPERTS)))
        rw_hi = rw.astype(BF16)
        rw_lo = (rw - rw_hi.astype(F32)).astype(BF16)
        rb = jnp.pad(router_b[l], (0, LANES - N_EXPERTS)).reshape(1, LANES)
        h, u2, gate_tile, idx_tile = _out_proj_router(
            yc.reshape(n_tok, D_CONV), yr.reshape(n_tok, D_RWKV), h, w_out[l].astype(BF16), mod6,
            norm2_g[l].reshape(1, -1), rw_hi, rw_lo, rb, seq // TOKEN_TILE)
        lpos, tab, x_local = _sort_rows(idx_tile, u2)
        run_count = tab[:, 0, :N_EXPERTS].astype(I32)
        run_local = tab[:, 1, :N_EXPERTS].astype(I32)
        counts = jnp.sum(run_count, axis=0)
        padded = (counts + ROW_BLOCK - 1) // ROW_BLOCK * ROW_BLOCK
        pad_end = jnp.cumsum(padded)
        pad_start = pad_end - padded
        run_start = pad_start[None, :] + jnp.cumsum(run_count, axis=0) - run_count
        n_used = (pad_end[-1] // ROW_BLOCK).astype(I32).reshape(1)
        blk_row = jnp.minimum(jnp.arange(n_blocks, dtype=I32), n_used[0] - 1) * ROW_BLOCK
        block_expert = jnp.minimum(
            jnp.sum((pad_end[None, :] <= blk_row[:, None]).astype(I32), axis=1), N_EXPERTS - 1)
        blk_start = run_start.T[block_expert]
        blk_end = blk_start + run_count.T[block_expert]
        first_tile = jnp.sum((blk_end <= blk_row[:, None]).astype(I32), axis=1)
        last_tile = jnp.sum((blk_start < blk_row[:, None] + ROW_BLOCK).astype(I32), axis=1) - 1
        n_valid = jnp.clip((pad_start + counts)[block_expert] - blk_row, 0, ROW_BLOCK).astype(I32)
        run_tables = (run_start.reshape(-1), run_local.reshape(-1), run_count.reshape(-1))
        y_rows = _experts(x_local, (block_expert, n_used, first_tile, last_tile, n_valid), run_tables,
                          _deinterleave_w1(exp_w1[l]), exp_w2[l].astype(BF16),
                          exp_b1[l][:, None, 0::2], exp_b1[l][:, None, 1::2], exp_b2[l][:, None, :],
                          m_pad)
        out = _combine(y_rows, *run_tables, lpos, gate_tile, h, mod6, final_g.reshape(1, -1),
                       seq // (COMBINE_TILE * COMBINE_TILES_PER_STEP))
    return out.reshape(bsz, seq, D_MODEL)
```

```python
import jax
import jax.numpy as jnp
from jax import lax
from jax.experimental import pallas as pl
from jax.experimental.pallas import tpu as pltpu

F32 = jnp.float32
BF16 = jnp.bfloat16
I32 = jnp.int32

D_MODEL = 1024
D_CONV = 512
CONV_GROUP_DIM = 64
D_RWKV = 512
HEAD_DIM = 64
N_HEADS = D_RWKV // HEAD_DIM
DECAY_LORA = 64
AAA_LORA = 64
GATE_LORA = 128
COLS_CONV = 3 * D_CONV
COLS_RWKV = 3 * D_RWKV + DECAY_LORA + AAA_LORA + GATE_LORA
N_EXPERTS = 32
TOP_K = 4
D_EXPERT = D_MODEL
SWIGLU_LIMIT = 7.0
SWIGLU_ALPHA = 1.702
NORM_EPS = 1e-5
GN_EPS = 64e-5

LANES = 128
SUBLANES = 8
MXU_TILE = 256
CHUNK = 64
RWKV_SEQS_PER_STEP = 4
RWKV_STAGE_LAG = 5
ROW_BLOCK = 512
TOKEN_TILE = 512
COMBINE_TILE = 256
COMBINE_TILES_PER_STEP = 2
VMEM_LIMIT = 48 * 1024 * 1024


def _cparams(*sem):
    return pltpu.CompilerParams(dimension_semantics=sem, vmem_limit_bytes=VMEM_LIMIT)


def _mm(a, b):
    return jnp.dot(a.astype(BF16), b.astype(BF16), preferred_element_type=F32)


def _mm_nt(a, b):
    return lax.dot_general(a.astype(BF16), b.astype(BF16), (((1,), (1,)), ((), ())),
                           preferred_element_type=F32)


def _mm_tn(a, b):
    return lax.dot_general(a.astype(BF16), b.astype(BF16), (((0,), (0,)), ((), ())),
                           preferred_element_type=F32)


def _split2(x):
    hi = x.astype(BF16)
    lo = (x - hi.astype(F32)).astype(BF16)
    return hi, lo


def _split3(x):
    hi = x.astype(BF16)
    r = x - hi.astype(F32)
    mid = r.astype(BF16)
    lo = (r - mid.astype(F32)).astype(BF16)
    return hi, mid, lo


def _sigmoid(x):
    return 1.0 / (1.0 + jnp.exp(-x))


def _rms(x):
    return x * lax.rsqrt(jnp.mean(x * x, axis=-1, keepdims=True) + NORM_EPS)


def _ada_body(c_ref, w_ref, b_ref, o_ref):
    c = c_ref[...]
    ca = c * _sigmoid(c)
    ah, al = _split2(ca)
    wh, wl = _split2(w_ref[...])
    acc = jnp.dot(ah, wh, preferred_element_type=F32)
    acc += jnp.dot(ah, wl, preferred_element_type=F32)
    acc += jnp.dot(al, wh, preferred_element_type=F32)
    o_ref[...] = acc + b_ref[...]


def _ada_mod(c, ada_w, ada_b):
    bsz = c.shape[0]
    n = ada_w.shape[1]
    tn = 1024
    return pl.pallas_call(
        _ada_body,
        name="ada_mod",
        grid=(n // tn,),
        in_specs=[pl.BlockSpec((bsz, D_MODEL), lambda j: (0, 0)),
                  pl.BlockSpec((D_MODEL, tn), lambda j: (0, j)),
                  pl.BlockSpec((1, tn), lambda j: (0, j))],
        out_specs=pl.BlockSpec((bsz, tn), lambda j: (0, j)),
        out_shape=jax.ShapeDtypeStruct((bsz, n), F32),
        compiler_params=_cparams("parallel"),
    )(c, ada_w, ada_b.reshape(1, n))


def _group_sums(x, group):
    li = lax.broadcasted_iota(I32, (LANES, LANES), 0)
    lj = lax.broadcasted_iota(I32, (LANES, LANES), 1)
    same = jnp.where((li < group) == (lj < group), 1.0, 0.0).astype(BF16)
    hi, lo = _split2(x)
    cols = []
    for t in range(x.shape[1] // LANES):
        ts = slice(t * LANES, (t + 1) * LANES)
        cols.append(jnp.dot(hi[:, ts], same, preferred_element_type=F32)
                    + jnp.dot(lo[:, ts], same, preferred_element_type=F32))
    return jnp.concatenate(cols, axis=1)


def _inproj_body(x_ref, g_ref, sh_ref, sc_ref, w_ref, cw_ref, gn_ref, yc_ref, or_ref, carry_ref):
    @pl.when(pl.program_id(1) == 0)
    def _():
        carry_ref[...] = jnp.zeros_like(carry_ref)

    ub = _rms(x_ref[0]) * g_ref[...]
    ub = (ub * (1.0 + sc_ref[0, 0]) + sh_ref[0, 0]).astype(BF16)
    p = jnp.dot(ub, w_ref[:, :COLS_CONV], preferred_element_type=F32)
    or_ref[0] = jnp.dot(ub, w_ref[:, COLS_CONV:], preferred_element_type=F32)

    tc = p.shape[0]
    b_gate = p[:, :D_CONV]
    u = p[:, D_CONV:2 * D_CONV] * p[:, 2 * D_CONV:]
    ext = jnp.concatenate([carry_ref[...], u], axis=0)
    u1 = pltpu.roll(ext, 1, axis=0)[SUBLANES:]
    u2 = pltpu.roll(ext, 2, axis=0)[SUBLANES:]
    carry_ref[...] = u[tc - SUBLANES:, :]
    cw = cw_ref[...]
    y = b_gate * (cw[0:1] * u2 + cw[1:2] * u1 + cw[2:3] * u)
    ms = _group_sums(y * y, CONV_GROUP_DIM) * (1.0 / CONV_GROUP_DIM)
    yc_ref[0] = y * lax.rsqrt(ms + NORM_EPS) * gn_ref[...]


def _in_proj_conv(x, g1, mod6, w_in_bf, conv_w, conv_gn):
    bsz, seq, _ = x.shape
    tm = TOKEN_TILE
    n_in = w_in_bf.shape[1]
    const = lambda m, n: pl.BlockSpec((m, n), lambda b, i: (0, 0))
    return pl.pallas_call(
        _inproj_body,
        name="in_proj_conv",
        grid=(bsz, seq // tm),
        in_specs=[pl.BlockSpec((1, tm, D_MODEL), lambda b, i: (b, i, 0)),
                  const(1, D_MODEL),
                  pl.BlockSpec((1, 1, 1, D_MODEL), lambda b, i: (b, 0, 0, 0)),
                  pl.BlockSpec((1, 1, 1, D_MODEL), lambda b, i: (b, 1, 0, 0)),
                  const(D_MODEL, n_in), const(3, D_CONV), const(1, D_CONV)],
        out_specs=[pl.BlockSpec((1, tm, D_CONV), lambda b, i: (b, i, 0)),
                   pl.BlockSpec((1, tm, COLS_RWKV), lambda b, i: (b, i, 0))],
        out_shape=[jax.ShapeDtypeStruct((bsz, seq, D_CONV), F32),
                   jax.ShapeDtypeStruct((bsz, seq, COLS_RWKV), F32)],
        scratch_shapes=[pltpu.VMEM((SUBLANES, D_CONV), F32)],
        compiler_params=_cparams("parallel", "arbitrary"),
    )(x, g1, mod6, mod6, w_in_bf, conv_w, conv_gn)


def _rwkv_body(p_ref, mu_ref, w0_ref, wup_ref, a0_ref, aup_ref, gup_ref, kkw_ref, kaw_ref,
               rkw_ref, lnw_ref, lnb_ref, o_ref, state_ref, prev_ref):
    @pl.when(pl.program_id(1) == 0)
    def _():
        state_ref[...] = jnp.zeros_like(state_ref)
        prev_ref[...] = jnp.zeros_like(prev_ref)

    C = CHUNK
    N = HEAD_DIM
    nb = p_ref.shape[0]
    s1, s2, s3 = D_RWKV, 2 * D_RWKV, 3 * D_RWKV
    row1 = lax.broadcasted_iota(I32, (C, 1), 0)
    ri = lax.broadcasted_iota(I32, (C, C), 0)
    ci = lax.broadcasted_iota(I32, (C, C), 1)
    strict = ri > ci
    incl = ri >= ci
    ri2 = lax.broadcasted_iota(I32, (C, 2 * C), 0)
    ci2 = lax.broadcasted_iota(I32, (C, 2 * C), 1)
    ci2 = jnp.where(ci2 >= C, ci2 - C, ci2)
    strict2 = ri2 > ci2
    incl2 = ri2 >= ci2
    eye = jnp.where(ri == ci, 1.0, 0.0).astype(F32)
    tri = jnp.where(incl, 1.0, 0.0).astype(BF16)

    def stages(bb):
        units = []
        P = p_ref[bb]
        Pprev = jnp.where(row1 == 0, prev_ref[bb, 0:1, :], pltpu.roll(P, 1, axis=0))
        prev_ref[bb] = jnp.broadcast_to(P[C - 1:C, :], prev_ref.shape[1:])
        p = P + (Pprev - P) * mu_ref[...]
        r = p[:, :s1]
        k = p[:, s1:s2]
        v = p[:, s2:s3]
        dw = p[:, s3:s3 + DECAY_LORA]
        da = p[:, s3 + DECAY_LORA:s3 + DECAY_LORA + AAA_LORA]
        dg = p[:, s3 + DECAY_LORA + AAA_LORA:]
        z = -(w0_ref[...] + _mm(jnp.tanh(dw), wup_ref[...]))
        softplus = jnp.maximum(z, 0.0) + jnp.log(1.0 + jnp.exp(-jnp.abs(z)))
        logw = -jnp.exp(-softplus - 0.5)
        a = _sigmoid(a0_ref[...] + _mm(da, aup_ref[...]))
        g = _mm(_sigmoid(dg), gup_ref[...])
        kk = k * kkw_ref[...]
        kmod = k * (1.0 + (a - 1.0) * kaw_ref[...])
        cum = sum(jnp.dot(tri, part, preferred_element_type=F32) for part in _split3(logw))
        gam = jnp.exp(cum)
        gam_prev = jnp.exp(cum - logw)
        ginv = jnp.exp(-cum)
        def head_sum(x):
            return _group_sums(x, N)

        kkn = kk * lax.rsqrt(jnp.maximum(head_sum(kk * kk), 1e-24))
        bonus_v = head_sum(r * kmod * rkw_ref[...]) * v
        at_all = (-kkn * gam_prev).astype(BF16)
        rt_all = (r * gam).astype(BF16)
        bt_all = (kkn * a * ginv).astype(BF16)
        kt_all = (kmod * ginv).astype(BF16)
        vb_all = v.astype(BF16)
        for h in range(N_HEADS):
            sl = slice(h * N, (h + 1) * N)
            At, Rt = at_all[:, sl], rt_all[:, sl]
            units.append(dict(
                slot=bb * N_HEADS + h, At=At, Rt=Rt, vb=vb_all[:, sl],
                BK=jnp.concatenate([bt_all[:, sl], kt_all[:, sl]], axis=0),
                AR=jnp.concatenate([At, Rt], axis=0), glast=gam[C - 1:C, sl]))
        yield
        for u in units:
            M = _mm_nt(u["AR"], u["BK"])
            top = jnp.where(strict2, M[:C, :], 0.0)
            u["T"] = eye + top[:, :C]
            u["LL"] = top.astype(BF16)
            u["Lp"] = top[:, :C].astype(BF16)
            u["Tr"] = jnp.where(incl2, M[C:, :], 0.0).astype(BF16)
        yield
        for u in units:
            u["X2"] = _mm(u["LL"], jnp.concatenate([jnp.zeros_like(u["vb"]), u["vb"]], axis=0))
            u["Lp"] = _mm(u["Lp"], u["Lp"]).astype(BF16)
        yield
        for _ in range(4):
            for u in units:
                u["T"] = u["T"] + _mm(u["T"], u["Lp"])
                u["Lp"] = _mm(u["Lp"], u["Lp"]).astype(BF16)
            yield
        for u in units:
            u["T"] = u["T"] + _mm(u["T"], u["Lp"])
        yield
        for u in units:
            u["W"] = _mm(u["T"], jnp.concatenate([u["At"], u["X2"].astype(BF16)], axis=1))
        yield
        for u in units:
            u["S0"] = state_ref[u["slot"]]
            u["S0b"] = u["S0"].astype(BF16)
            U = _mm_nt(u["W"][:, :N], u["S0b"]) + u["W"][:, N:]
            u["UV"] = jnp.concatenate([U.astype(BF16), u["vb"]], axis=0)
        yield
        for u in units:
            u["O"] = _mm_nt(u["Rt"], u["S0b"]) + _mm(u["Tr"], u["UV"])
            state_ref[u["slot"]] = (u["S0"] + _mm_tn(u["UV"], u["BK"])) * u["glast"]
        yield
        o_all = jnp.concatenate([u["O"] for u in units], axis=-1)
        cen = o_all - head_sum(o_all) * (1.0 / N)
        var = head_sum(cen * cen) * (1.0 / N)
        on = cen * lax.rsqrt(var + GN_EPS) * lnw_ref[...] + lnb_ref[...]
        o_ref[bb] = (on + bonus_v) * g
        yield

    live = [(bb, stages(bb)) for bb in range(nb)]
    tick = 0
    while live:
        for item in list(live):
            if tick >= RWKV_STAGE_LAG * item[0] and next(item[1], "done") == "done":
                live.remove(item)
        tick += 1


def _rwkv_mixer(pr, mu, w0, w_up, a0, a_up, g_up, k_k, k_a, r_k, ln_w, ln_b, bsz, seq):
    C = CHUNK
    nb = RWKV_SEQS_PER_STEP if bsz % RWKV_SEQS_PER_STEP == 0 else 1
    row = lambda n: pl.BlockSpec((1, n), lambda b, i: (0, 0))
    full = lambda m, n: pl.BlockSpec((m, n), lambda b, i: (0, 0))
    return pl.pallas_call(
        _rwkv_body,
        name="rwkv_mixer",
        grid=(bsz // nb, seq // C),
        in_specs=[pl.BlockSpec((nb, C, COLS_RWKV), lambda b, i: (b, i, 0)),
                  row(COLS_RWKV), row(D_RWKV), full(DECAY_LORA, D_RWKV), row(D_RWKV),
                  full(AAA_LORA, D_RWKV), full(GATE_LORA, D_RWKV), row(D_RWKV), row(D_RWKV),
                  row(D_RWKV), row(D_RWKV), row(D_RWKV)],
        out_specs=pl.BlockSpec((nb, C, D_RWKV), lambda b, i: (b, i, 0)),
        out_shape=jax.ShapeDtypeStruct((bsz, seq, D_RWKV), F32),
        scratch_shapes=[pltpu.VMEM((nb * N_HEADS, HEAD_DIM, HEAD_DIM), F32),
                        pltpu.VMEM((nb, SUBLANES, COLS_RWKV), F32)],
        compiler_params=_cparams("parallel", "arbitrary"),
    )(pr.reshape(bsz, seq, COLS_RWKV), mu.reshape(1, -1), w0.reshape(1, -1), w_up,
      a0.reshape(1, -1), a_up, g_up, k_k.reshape(1, -1), k_a.reshape(1, -1), r_k.reshape(1, -1),
      ln_w.reshape(1, -1), ln_b.reshape(1, -1))


def _outproj_body(yc_ref, yr_ref, x_ref, wo_ref, gt_ref, g2_ref, sh_ref, sc_ref, rwh_ref, rwl_ref,
                  rb_ref, h_ref, gate_ref, lpos_ref, tab_ref, xl_ref):
    mix = _mm(yc_ref[...], wo_ref[:D_CONV, :]) + _mm(yr_ref[...], wo_ref[D_CONV:, :])
    h = x_ref[...] + gt_ref[0, 0] * mix
    h_ref[...] = h
    u2 = _rms(h) * g2_ref[...]
    u2 = u2 * (1.0 + sc_ref[0, 0]) + sh_ref[0, 0]

    uh, ul = _split2(u2)
    logits = (jnp.dot(uh, rwh_ref[...], preferred_element_type=F32)
              + jnp.dot(uh, rwl_ref[...], preferred_element_type=F32)
              + jnp.dot(ul, rwh_ref[...], preferred_element_type=F32)) + rb_ref[...]
    tm = logits.shape[0]
    lane = lax.broadcasted_iota(I32, (tm, LANES), 1)
    neg = jnp.float32(-jnp.inf)
    l = jnp.where(lane < N_EXPERTS, logits, neg)
    tops, idxs = [], []
    lane_f = lane.astype(F32)
    for _ in range(TOP_K):
        m = jnp.max(l, axis=-1, keepdims=True)
        ix = jnp.min(jnp.where(l == m, lane_f, float(LANES)), axis=-1, keepdims=True).astype(I32)
        tops.append(m)
        idxs.append(ix)
        l = jnp.where(lane == ix, neg, l)
    es = [jnp.exp(t - tops[0]) for t in tops]
    den = es[0] + es[1] + es[2] + es[3]
    gate_tile = jnp.zeros((tm, LANES), F32)
    idx_tile = jnp.zeros((tm, LANES), I32)
    for kq in range(TOP_K):
        gate_tile = jnp.where(lane == kq, es[kq] / den, gate_tile)
        idx_tile = jnp.where(lane == kq, idxs[kq], idx_tile)
    gate_ref[...] = gate_tile
    _sort_tiles(idx_tile, uh, lpos_ref, tab_ref, xl_ref)


def _out_proj_router(yc, yr, xf, w_out_bf, mod6, g2, rw_hi, rw_lo, rb, tiles_per_batch):
    n_tok = xf.shape[0]
    tm = TOKEN_TILE
    n_sub = tm // COMBINE_TILE
    n_seg = D_MODEL // LANES
    tok = lambda n: pl.BlockSpec((tm, n), lambda i: (i, 0))
    modspec = lambda which: pl.BlockSpec((1, 1, 1, D_MODEL),
                                         lambda i: (i // tiles_per_batch, which, 0, 0))
    const = lambda m, n: pl.BlockSpec((m, n), lambda i: (0, 0))
    return pl.pallas_call(
        _outproj_body,
        name="out_proj_router",
        grid=(n_tok // tm,),
        in_specs=[tok(D_CONV), tok(D_RWKV), tok(D_MODEL), const(D_MODEL, D_MODEL),
                  modspec(2), const(1, D_MODEL), modspec(3), modspec(4),
                  const(D_MODEL, LANES), const(D_MODEL, LANES), const(1, LANES)],
        out_specs=[tok(D_MODEL), tok(LANES), tok(LANES),
                   pl.BlockSpec((n_sub, SUBLANES, LANES), lambda i: (i, 0, 0)),
                   pl.BlockSpec((tm * TOP_K * n_seg, LANES), lambda i: (i, 0))],
        out_shape=[jax.ShapeDtypeStruct((n_tok, D_MODEL), F32),
                   jax.ShapeDtypeStruct((n_tok, LANES), F32),
                   jax.ShapeDtypeStruct((n_tok, LANES), F32),
                   jax.ShapeDtypeStruct((n_tok // COMBINE_TILE, SUBLANES, LANES), F32),
                   jax.ShapeDtypeStruct((n_tok * TOP_K * n_seg, LANES), F32)],
        compiler_params=_cparams("parallel"),
    )(yc, yr, xf, w_out_bf, mod6, g2, mod6, mod6, rw_hi, rw_lo, rb)


def _sort_tiles(idx_all, u_all, lpos_ref, tab_ref, xl_ref):
    tm = COMBINE_TILE
    n_sub = idx_all.shape[0] // tm
    n_loc = tm * TOP_K
    n_seg = D_MODEL // LANES
    rr = lax.broadcasted_iota(I32, (tm, tm), 0)
    cc = lax.broadcasted_iota(I32, (tm, tm), 1)
    tri = jnp.where(rr > cc, 1.0, 0.0).astype(BF16)
    lane = lax.broadcasted_iota(I32, (tm, LANES), 1)
    lane8 = lax.broadcasted_iota(I32, (SUBLANES, LANES), 1)
    row8 = lax.broadcasted_iota(I32, (SUBLANES, LANES), 0)
    q = lax.broadcasted_iota(I32, (n_loc, tm), 0)
    subs = [dict(tok=slice(s * tm, (s + 1) * tm)) for s in range(n_sub)]

    for sb in subs:
        idx = idx_all[sb["tok"], :]
        sb["masks"] = [lane == idx[:, kq:kq + 1] for kq in range(TOP_K)]
        sel = jnp.zeros((tm, LANES), F32)
        for mk in sb["masks"]:
            sel = sel + jnp.where(mk, 1.0, 0.0)
        sb["sel"] = sel
    for sb in subs:
        sb["prefix"] = jnp.dot(tri, sb["sel"].astype(BF16), preferred_element_type=F32)
    for s, sb in enumerate(subs):
        cnt = jnp.broadcast_to(jnp.sum(sb["sel"], axis=0, keepdims=True), (SUBLANES, LANES))
        inc = cnt
        for sh in (1, 2, 4, 8, 16, 32, 64):
            inc = inc + jnp.where(lane8 >= sh, pltpu.roll(inc, sh, axis=1), 0.0)
        loff = inc - cnt
        tab_ref[s] = jnp.where(row8 == 0, cnt, loff)
        local_row = loff[0:1, :] + sb["prefix"]
        lpos_tile = jnp.zeros((tm, LANES), F32)
        for kq in range(TOP_K):
            lk = jnp.sum(jnp.where(sb["masks"][kq], local_row, 0.0), axis=-1, keepdims=True)
            lpos_tile = jnp.where(lane == kq, lk, lpos_tile)
        lpos_ref[sb["tok"], :] = lpos_tile
        sb["lpos_t"] = jnp.transpose(lpos_tile)[0:SUBLANES, :].astype(I32)
    for sb in subs:
        pick = jnp.zeros((n_loc, tm), F32)
        for kq in range(TOP_K):
            pick = jnp.where(q == sb["lpos_t"][kq:kq + 1, :], 1.0, pick)
        sb["xl"] = jnp.dot(pick.astype(BF16), u_all[sb["tok"], :], preferred_element_type=F32)
    for s, sb in enumerate(subs):
        for j in range(n_seg):
            xl_ref[pl.ds(s * n_loc * n_seg + j, n_loc, stride=n_seg), :] = (
                sb["xl"][:, j * LANES:(j + 1) * LANES])


def _deinterleave_body(w_ref, o_ref):
    w = w_ref[0].astype(BF16)
    half = o_ref.shape[2] // 2
    src = lax.broadcasted_iota(I32, (2 * MXU_TILE, MXU_TILE), 0)
    dst = lax.broadcasted_iota(I32, (2 * MXU_TILE, MXU_TILE), 1)
    pick_even = jnp.where(src == 2 * dst, 1.0, 0.0).astype(BF16)
    pick_odd = jnp.where(src == 2 * dst + 1, 1.0, 0.0).astype(BF16)
    for j in range(half // MXU_TILE):
        slab = w[:, 2 * MXU_TILE * j:2 * MXU_TILE * (j + 1)]
        lo = MXU_TILE * j
        o_ref[0, :, lo:lo + MXU_TILE] = jnp.dot(
            slab, pick_even, preferred_element_type=F32).astype(BF16)
        o_ref[0, :, half + lo:half + lo + MXU_TILE] = jnp.dot(
            slab, pick_odd, preferred_element_type=F32).astype(BF16)


def _deinterleave_w1(w1):
    n_e, d_in, two_f = w1.shape
    tr = 512
    return pl.pallas_call(
        _deinterleave_body,
        name="deinterleave_w1",
        grid=(n_e, d_in // tr),
        in_specs=[pl.BlockSpec((1, tr, two_f), lambda e, r: (e, r, 0))],
        out_specs=pl.BlockSpec((1, tr, two_f), lambda e, r: (e, r, 0)),
        out_shape=jax.ShapeDtypeStruct((n_e, d_in, two_f), BF16),
        compiler_params=_cparams("parallel", "parallel"),
    )(w1)


def _expert_body(be_ref, nu_ref, ft_ref, lt_ref, nv_ref, rg_ref, rl_ref, rc_ref,
                 xl_ref, w1_ref, w2_ref, bg_ref, bl_ref, b2_ref, y_ref, xbuf_ref, sem):
    i = pl.program_id(0)
    n_used = nu_ref[0]
    bm = ROW_BLOCK
    n_seg = D_MODEL // LANES
    n_loc = COMBINE_TILE * TOP_K

    def issue(blk, slot):
        e = be_ref[blk]
        r0 = blk * bm

        def piece(t, carry):
            g = rg_ref[t * N_EXPERTS + e]
            s = jnp.maximum(g, r0)
            n = jnp.minimum(g + rc_ref[t * N_EXPERTS + e], r0 + bm) - s

            @pl.when(n > 0)
            def _():
                src = pl.multiple_of((t * n_loc + rl_ref[t * N_EXPERTS + e] + s - g) * n_seg, n_seg)
                dst = pl.multiple_of((s - r0) * n_seg, n_seg)
                pltpu.make_async_copy(xl_ref.at[pl.ds(src, n * n_seg)],
                                      xbuf_ref.at[slot, pl.ds(dst, n * n_seg)], sem.at[slot]).start()
            return carry

        lax.fori_loop(ft_ref[blk], lt_ref[blk] + 1, piece, 0)

    @pl.when(i == 0)
    def _():
        xbuf_ref[...] = jnp.zeros_like(xbuf_ref)
        issue(0, 0)

    @pl.when(i + 1 < n_used)
    def _():
        issue(i + 1, (i + 1) % 2)

    @pl.when(i >= n_used)
    def _():
        y_ref[...] = jnp.zeros_like(y_ref)

    @pl.when(i < n_used)
    def _():
        slot = i % 2
        n_valid = nv_ref[i] * n_seg
        pltpu.make_async_copy(xl_ref.at[pl.ds(0, n_valid)], xbuf_ref.at[slot, pl.ds(0, n_valid)],
                              sem.at[slot]).wait()
        x = jnp.concatenate([xbuf_ref[slot, pl.ds(j, bm, stride=n_seg), :] for j in range(n_seg)],
                            axis=1).astype(BF16)
        hb = jnp.dot(x, w1_ref[0], preferred_element_type=F32)
        hg = hb[:, :D_EXPERT] + bg_ref[0]
        hl = hb[:, D_EXPERT:] + bl_ref[0]
        xg = jnp.minimum(hg, SWIGLU_LIMIT)
        xl = jnp.clip(hl, -SWIGLU_LIMIT, SWIGLU_LIMIT)
        act = xg * _sigmoid(SWIGLU_ALPHA * xg) * (xl + 1.0)
        y = jnp.dot(act.astype(BF16), w2_ref[0], preferred_element_type=F32) + b2_ref[0]
        for j in range(n_seg):
            y_ref[pl.ds(j, bm, stride=n_seg), :] = y[:, j * LANES:(j + 1) * LANES]


def _experts(x_local, block_tables, run_tables, w1d, w2, b1g, b1l, b2, m_pad):
    bm = ROW_BLOCK
    n_seg = D_MODEL // LANES
    wsel = lambda i, be, *_: (be[i], 0, 0)
    grid_spec = pltpu.PrefetchScalarGridSpec(
        num_scalar_prefetch=8,
        grid=(m_pad // bm,),
        in_specs=[pl.BlockSpec(memory_space=pl.ANY),
                  pl.BlockSpec((1, D_MODEL, 2 * D_EXPERT), wsel),
                  pl.BlockSpec((1, D_EXPERT, D_MODEL), wsel),
                  pl.BlockSpec((1, 1, D_EXPERT), wsel),
                  pl.BlockSpec((1, 1, D_EXPERT), wsel),
                  pl.BlockSpec((1, 1, D_MODEL), wsel)],
        out_specs=pl.BlockSpec((bm * n_seg, LANES), lambda i, *_: (i, 0)),
        scratch_shapes=[pltpu.VMEM((2, bm * n_seg, LANES), F32), pltpu.SemaphoreType.DMA((2,))],
    )
    return pl.pallas_call(
        _expert_body,
        name="expert_mlp",
        grid_spec=grid_spec,
        out_shape=jax.ShapeDtypeStruct((m_pad * n_seg, LANES), F32),
        compiler_params=_cparams("arbitrary"),
    )(*block_tables, *run_tables, x_local, w1d, w2, b1g, b1l, b2)


def _combine_body(rg_ref, rl_ref, rc_ref, y_ref, lpos_ref, gate_ref, h_ref, gt_ref, fg_ref, o_ref,
                  yloc_ref, sem):
    i = pl.program_id(0)
    tm = COMBINE_TILE
    n_sub = h_ref.shape[0] // tm
    n_seg = D_MODEL // LANES
    n_loc = tm * TOP_K

    def issue(step, slot):
        for s in range(n_sub):
            tile = step * n_sub + s
            for e in range(N_EXPERTS):
                c = rc_ref[tile * N_EXPERTS + e] * n_seg

                @pl.when(c > 0)
                def _():
                    src = pl.multiple_of(rg_ref[tile * N_EXPERTS + e] * n_seg, n_seg)
                    dst = pl.multiple_of((s * n_loc + rl_ref[tile * N_EXPERTS + e]) * n_seg, n_seg)
                    pltpu.make_async_copy(y_ref.at[pl.ds(src, c)], yloc_ref.at[slot, pl.ds(dst, c)],
                                          sem.at[slot]).start()

    @pl.when(i == 0)
    def _():
        issue(0, 0)

    @pl.when(i + 1 < pl.num_programs(0))
    def _():
        issue(i + 1, (i + 1) % 2)

    slot = i % 2
    pltpu.make_async_copy(y_ref.at[pl.ds(0, n_sub * n_loc * n_seg)], yloc_ref.at[slot],
                          sem.at[slot]).wait()

    q = lax.broadcasted_iota(I32, (tm, n_loc), 1)
    subs = [dict(tok=slice(s * tm, (s + 1) * tm)) for s in range(n_sub)]
    for s, sb in enumerate(subs):
        sb["y"] = jnp.concatenate(
            [yloc_ref[slot, pl.ds(s * n_loc * n_seg + j, n_loc, stride=n_seg), :]
             for j in range(n_seg)], axis=1).astype(BF16)
        lpos = lpos_ref[sb["tok"], :].astype(I32)
        gate = gate_ref[sb["tok"], :]
        gmat = jnp.zeros((tm, n_loc), F32)
        for kq in range(TOP_K):
            gmat = jnp.where(q == lpos[:, kq:kq + 1], gate[:, kq:kq + 1], gmat)
        sb["ghi"], sb["glo"] = _split2(gmat)
    for sb in subs:
        sb["moe"] = (jnp.dot(sb["ghi"], sb["y"], preferred_element_type=F32)
                     + jnp.dot(sb["glo"], sb["y"], preferred_element_type=F32))
    for sb in subs:
        h = h_ref[sb["tok"], :] + gt_ref[0, 0] * sb["moe"]
        o_ref[sb["tok"], :] = _rms(h) * fg_ref[...]


def _combine(y_rows, run_start, run_local, run_count, lpos, gate_tile, h, mod6, final_g,
             tiles_per_batch):
    n_tok = h.shape[0]
    tm = COMBINE_TILE * COMBINE_TILES_PER_STEP
    grid_spec = pltpu.PrefetchScalarGridSpec(
        num_scalar_prefetch=3,
        grid=(n_tok // tm,),
        in_specs=[pl.BlockSpec(memory_space=pl.ANY),
                  pl.BlockSpec((tm, LANES), lambda i, *_: (i, 0)),
                  pl.BlockSpec((tm, LANES), lambda i, *_: (i, 0)),
                  pl.BlockSpec((tm, D_MODEL), lambda i, *_: (i, 0)),
                  pl.BlockSpec((1, 1, 1, D_MODEL), lambda i, *_: (i // tiles_per_batch, 5, 0, 0)),
                  pl.BlockSpec((1, D_MODEL), lambda i, *_: (0, 0))],
        out_specs=pl.BlockSpec((tm, D_MODEL), lambda i, *_: (i, 0)),
        scratch_shapes=[pltpu.VMEM((2, tm * TOP_K * (D_MODEL // LANES), LANES), F32),
                        pltpu.SemaphoreType.DMA((2,))],
    )
    return pl.pallas_call(
        _combine_body,
        name="combine_rows",
        grid_spec=grid_spec,
        out_shape=jax.ShapeDtypeStruct((n_tok, D_MODEL), F32),
        compiler_params=_cparams("arbitrary"),
    )(run_start, run_local, run_count, y_rows, lpos, gate_tile, h, mod6, final_g)


def kernel(x, c, ada_w, ada_b, norm1_g, w_in, conv_w, conv_gn, rwkv_mu, rwkv_w0, rwkv_w_up,
           rwkv_a0, rwkv_a_up, rwkv_g_up, rwkv_k_k, rwkv_k_a, rwkv_r_k, rwkv_ln_w, rwkv_ln_b,
           w_out, norm2_g, router_w, router_b, exp_w1, exp_b1, exp_w2, exp_b2, final_g):
    bsz, seq, _ = x.shape
    depth = ada_w.shape[0]
    assert depth == 1, "the final norm is fused into the last layer's combine step"
    n_tok = bsz * seq
    assert seq % TOKEN_TILE == 0 and seq % CHUNK == 0 and seq % COMBINE_TILE == 0
    n_rows = n_tok * TOP_K
    assert n_rows % ROW_BLOCK == 0
    n_blocks = n_rows // ROW_BLOCK + N_EXPERTS
    m_pad = n_blocks * ROW_BLOCK

    h = x.reshape(n_tok, D_MODEL)
    out = h
    for l in range(depth):
        mod6 = _ada_mod(c, ada_w[l], ada_b[l]).reshape(bsz, 6, 1, D_MODEL)
        yc, pr = _in_proj_conv(h.reshape(bsz, seq, D_MODEL), norm1_g[l].reshape(1, -1), mod6,
                               w_in[l].astype(BF16), conv_w[l], conv_gn[l].reshape(1, -1))
        yr = _rwkv_mixer(pr, rwkv_mu[l], rwkv_w0[l], rwkv_w_up[l], rwkv_a0[l], rwkv_a_up[l],
                         rwkv_g_up[l], rwkv_k_k[l], rwkv_k_a[l], rwkv_r_k[l], rwkv_ln_w[l],
                         rwkv_ln_b[l], bsz, seq)
        rw = jnp.pad(router_w[l], ((0, 0), (0, LANES - N_EXPERTS)))
        rw_hi = rw.astype(BF16)
        rw_lo = (rw - rw_hi.astype(F32)).astype(BF16)
        rb = jnp.pad(router_b[l], (0, LANES - N_EXPERTS)).reshape(1, LANES)
        h, gate_tile, lpos, tab, x_local = _out_proj_router(
            yc.reshape(n_tok, D_CONV), yr.reshape(n_tok, D_RWKV), h, w_out[l].astype(BF16), mod6,
            norm2_g[l].reshape(1, -1), rw_hi, rw_lo, rb, seq // TOKEN_TILE)
        run_count = tab[:, 0, :N_EXPERTS].astype(I32)
        run_local = tab[:, 1, :N_EXPERTS].astype(I32)
        counts = jnp.sum(run_count, axis=0)
        padded = (counts + ROW_BLOCK - 1) // ROW_BLOCK * ROW_BLOCK
        pad_end = jnp.cumsum(padded)
        pad_start = pad_end - padded
        run_start = pad_start[None, :] + jnp.cumsum(run_count, axis=0) - run_count
        n_used = (pad_end[-1] // ROW_BLOCK).astype(I32).reshape(1)
        blk_row = jnp.minimum(jnp.arange(n_blocks, dtype=I32), n_used[0] - 1) * ROW_BLOCK
        block_expert = jnp.minimum(
            jnp.sum((pad_end[None, :] <= blk_row[:, None]).astype(I32), axis=1), N_EXPERTS - 1)
        blk_start = run_start.T[block_expert]
        blk_end = blk_start + run_count.T[block_expert]
        first_tile = jnp.sum((blk_end <= blk_row[:, None]).astype(I32), axis=1)
        last_tile = jnp.sum((blk_start < blk_row[:, None] + ROW_BLOCK).astype(I32), axis=1) - 1
        n_valid = jnp.clip((pad_start + counts)[block_expert] - blk_row, 0, ROW_BLOCK).astype(I32)
        run_tables = (run_start.reshape(-1), run_local.reshape(-1), run_count.reshape(-1))
        y_rows = _experts(x_local, (block_expert, n_used, first_tile, last_tile, n_valid), run_tables,
                          _deinterleave_w1(exp_w1[l]), exp_w2[l].astype(BF16),
                          exp_b1[l][:, None, 0::2], exp_b1[l][:, None, 1::2], exp_b2[l][:, None, :],
                          m_pad)
        out = _combine(y_rows, *run_tables, lpos, gate_tile, h, mod6, final_g.reshape(1, -1),
                       seq // (COMBINE_TILE * COMBINE_TILES_PER_STEP))
    return out.reshape(bsz, seq, D_MODEL)
```

```python
import jax
import jax.numpy as jnp
from jax import lax
from jax.experimental import pallas as pl
from jax.experimental.pallas import tpu as pltpu

F32 = jnp.float32
BF16 = jnp.bfloat16
I32 = jnp.int32

D_MODEL = 1024
D_CONV = 512
CONV_GROUP_DIM = 64
D_RWKV = 512
HEAD_DIM = 64
N_HEADS = D_RWKV // HEAD_DIM
DECAY_LORA = 64
AAA_LORA = 64
GATE_LORA = 128
COLS_CONV = 3 * D_CONV
COLS_RWKV = 3 * D_RWKV + DECAY_LORA + AAA_LORA + GATE_LORA
N_EXPERTS = 32
TOP_K = 4
D_EXPERT = D_MODEL
SWIGLU_LIMIT = 7.0
SWIGLU_ALPHA = 1.702
NORM_EPS = 1e-5
GN_EPS = 64e-5

LANES = 128
SUBLANES = 8
MXU_TILE = 256
CHUNK = 64
RWKV_SEQS_PER_STEP = 8
RWKV_STAGE_LAG = 3
ROW_BLOCK = 512
TOKEN_TILE = 512
COMBINE_TILE = 256
COMBINE_TILES_PER_STEP = 2
VMEM_LIMIT = 48 * 1024 * 1024


def _cparams(*sem):
    return pltpu.CompilerParams(dimension_semantics=sem, vmem_limit_bytes=VMEM_LIMIT)


def _mm(a, b):
    return jnp.dot(a.astype(BF16), b.astype(BF16), preferred_element_type=F32)


def _mm_nt(a, b):
    return lax.dot_general(a.astype(BF16), b.astype(BF16), (((1,), (1,)), ((), ())),
                           preferred_element_type=F32)


def _mm_tn(a, b):
    return lax.dot_general(a.astype(BF16), b.astype(BF16), (((0,), (0,)), ((), ())),
                           preferred_element_type=F32)


def _split2(x):
    hi = x.astype(BF16)
    lo = (x - hi.astype(F32)).astype(BF16)
    return hi, lo


def _split3(x):
    hi = x.astype(BF16)
    r = x - hi.astype(F32)
    mid = r.astype(BF16)
    lo = (r - mid.astype(F32)).astype(BF16)
    return hi, mid, lo


def _sigmoid(x):
    return 1.0 / (1.0 + jnp.exp(-x))


def _rms(x):
    return x * lax.rsqrt(jnp.mean(x * x, axis=-1, keepdims=True) + NORM_EPS)


def _ada_body(c_ref, w_ref, b_ref, o_ref):
    c = c_ref[...]
    ca = c * _sigmoid(c)
    ah, al = _split2(ca)
    wh, wl = _split2(w_ref[...])
    acc = jnp.dot(ah, wh, preferred_element_type=F32)
    acc += jnp.dot(ah, wl, preferred_element_type=F32)
    acc += jnp.dot(al, wh, preferred_element_type=F32)
    o_ref[...] = acc + b_ref[...]


def _ada_mod(c, ada_w, ada_b):
    bsz = c.shape[0]
    n = ada_w.shape[1]
    tn = 1024
    return pl.pallas_call(
        _ada_body,
        name="ada_mod",
        grid=(n // tn,),
        in_specs=[pl.BlockSpec((bsz, D_MODEL), lambda j: (0, 0)),
                  pl.BlockSpec((D_MODEL, tn), lambda j: (0, j)),
                  pl.BlockSpec((1, tn), lambda j: (0, j))],
        out_specs=pl.BlockSpec((bsz, tn), lambda j: (0, j)),
        out_shape=jax.ShapeDtypeStruct((bsz, n), F32),
        compiler_params=_cparams("parallel"),
    )(c, ada_w, ada_b.reshape(1, n))


def _group_sums(x, group):
    li = lax.broadcasted_iota(I32, (LANES, LANES), 0)
    lj = lax.broadcasted_iota(I32, (LANES, LANES), 1)
    same = jnp.where((li < group) == (lj < group), 1.0, 0.0).astype(BF16)
    hi, lo = _split2(x)
    cols = []
    for t in range(x.shape[1] // LANES):
        ts = slice(t * LANES, (t + 1) * LANES)
        cols.append(jnp.dot(hi[:, ts], same, preferred_element_type=F32)
                    + jnp.dot(lo[:, ts], same, preferred_element_type=F32))
    return jnp.concatenate(cols, axis=1)


def _inproj_body(x_ref, g_ref, sh_ref, sc_ref, w_ref, cw_ref, gn_ref, yc_ref, or_ref, carry_ref):
    @pl.when(pl.program_id(1) == 0)
    def _():
        carry_ref[...] = jnp.zeros_like(carry_ref)

    ub = _rms(x_ref[0]) * g_ref[...]
    ub = (ub * (1.0 + sc_ref[0, 0]) + sh_ref[0, 0]).astype(BF16)
    p = jnp.dot(ub, w_ref[:, :COLS_CONV], preferred_element_type=F32)
    or_ref[0] = jnp.dot(ub, w_ref[:, COLS_CONV:], preferred_element_type=F32)

    tc = p.shape[0]
    b_gate = p[:, :D_CONV]
    u = p[:, D_CONV:2 * D_CONV] * p[:, 2 * D_CONV:]
    ext = jnp.concatenate([carry_ref[...], u], axis=0)
    u1 = pltpu.roll(ext, 1, axis=0)[SUBLANES:]
    u2 = pltpu.roll(ext, 2, axis=0)[SUBLANES:]
    carry_ref[...] = u[tc - SUBLANES:, :]
    cw = cw_ref[...]
    y = b_gate * (cw[0:1] * u2 + cw[1:2] * u1 + cw[2:3] * u)
    ms = _group_sums(y * y, CONV_GROUP_DIM) * (1.0 / CONV_GROUP_DIM)
    yc_ref[0] = y * lax.rsqrt(ms + NORM_EPS) * gn_ref[...]


def _in_proj_conv(x, g1, mod6, w_in_bf, conv_w, conv_gn):
    bsz, seq, _ = x.shape
    tm = TOKEN_TILE
    n_in = w_in_bf.shape[1]
    const = lambda m, n: pl.BlockSpec((m, n), lambda b, i: (0, 0))
    return pl.pallas_call(
        _inproj_body,
        name="in_proj_conv",
        grid=(bsz, seq // tm),
        in_specs=[pl.BlockSpec((1, tm, D_MODEL), lambda b, i: (b, i, 0)),
                  const(1, D_MODEL),
                  pl.BlockSpec((1, 1, 1, D_MODEL), lambda b, i: (b, 0, 0, 0)),
                  pl.BlockSpec((1, 1, 1, D_MODEL), lambda b, i: (b, 1, 0, 0)),
                  const(D_MODEL, n_in), const(3, D_CONV), const(1, D_CONV)],
        out_specs=[pl.BlockSpec((1, tm, D_CONV), lambda b, i: (b, i, 0)),
                   pl.BlockSpec((1, tm, COLS_RWKV), lambda b, i: (b, i, 0))],
        out_shape=[jax.ShapeDtypeStruct((bsz, seq, D_CONV), F32),
                   jax.ShapeDtypeStruct((bsz, seq, COLS_RWKV), F32)],
        scratch_shapes=[pltpu.VMEM((SUBLANES, D_CONV), F32)],
        compiler_params=_cparams("parallel", "arbitrary"),
    )(x, g1, mod6, mod6, w_in_bf, conv_w, conv_gn)


def _rwkv_body(p_ref, mu_ref, w0_ref, wup_ref, a0_ref, aup_ref, gup_ref, kkw_ref, kaw_ref,
               rkw_ref, lnw_ref, lnb_ref, o_ref, state_ref, prev_ref):
    @pl.when(pl.program_id(1) == 0)
    def _():
        state_ref[...] = jnp.zeros_like(state_ref)
        prev_ref[...] = jnp.zeros_like(prev_ref)

    C = CHUNK
    N = HEAD_DIM
    nb = p_ref.shape[0]
    s1, s2, s3 = D_RWKV, 2 * D_RWKV, 3 * D_RWKV
    row1 = lax.broadcasted_iota(I32, (C, 1), 0)
    ri = lax.broadcasted_iota(I32, (C, C), 0)
    ci = lax.broadcasted_iota(I32, (C, C), 1)
    strict = ri > ci
    incl = ri >= ci
    ri2 = lax.broadcasted_iota(I32, (C, 2 * C), 0)
    ci2 = lax.broadcasted_iota(I32, (C, 2 * C), 1)
    ci2 = jnp.where(ci2 >= C, ci2 - C, ci2)
    strict2 = ri2 > ci2
    incl2 = ri2 >= ci2
    eye = jnp.where(ri == ci, 1.0, 0.0).astype(F32)
    tri = jnp.where(incl, 1.0, 0.0).astype(BF16)

    def stages(bb):
        units = []
        P = p_ref[bb]
        Pprev = jnp.where(row1 == 0, prev_ref[bb, 0:1, :], pltpu.roll(P, 1, axis=0))
        prev_ref[bb] = jnp.broadcast_to(P[C - 1:C, :], prev_ref.shape[1:])
        p = P + (Pprev - P) * mu_ref[...]
        r = p[:, :s1]
        k = p[:, s1:s2]
        v = p[:, s2:s3]
        dw = p[:, s3:s3 + DECAY_LORA]
        da = p[:, s3 + DECAY_LORA:s3 + DECAY_LORA + AAA_LORA]
        dg = p[:, s3 + DECAY_LORA + AAA_LORA:]
        z = -(w0_ref[...] + _mm(jnp.tanh(dw), wup_ref[...]))
        softplus = jnp.maximum(z, 0.0) + jnp.log(1.0 + jnp.exp(-jnp.abs(z)))
        logw = -jnp.exp(-softplus - 0.5)
        a = _sigmoid(a0_ref[...] + _mm(da, aup_ref[...]))
        g = _mm(_sigmoid(dg), gup_ref[...])
        kk = k * kkw_ref[...]
        kmod = k * (1.0 + (a - 1.0) * kaw_ref[...])
        cum = sum(jnp.dot(tri, part, preferred_element_type=F32) for part in _split3(logw))
        gam = jnp.exp(cum)
        gam_prev = jnp.exp(cum - logw)
        ginv = jnp.exp(-cum)
        def head_sum(x):
            return _group_sums(x, N)

        kkn = kk * lax.rsqrt(jnp.maximum(head_sum(kk * kk), 1e-24))
        bonus_v = head_sum(r * kmod * rkw_ref[...]) * v
        at_all = (-kkn * gam_prev).astype(BF16)
        rt_all = (r * gam).astype(BF16)
        bt_all = (kkn * a * ginv).astype(BF16)
        kt_all = (kmod * ginv).astype(BF16)
        vb_all = v.astype(BF16)
        for h in range(N_HEADS):
            sl = slice(h * N, (h + 1) * N)
            At, Rt = at_all[:, sl], rt_all[:, sl]
            units.append(dict(
                slot=bb * N_HEADS + h, At=At, Rt=Rt, vb=vb_all[:, sl],
                BK=jnp.concatenate([bt_all[:, sl], kt_all[:, sl]], axis=0),
                AR=jnp.concatenate([At, Rt], axis=0), glast=gam[C - 1:C, sl]))
        yield
        for u in units:
            M = _mm_nt(u["AR"], u["BK"])
            top = jnp.where(strict2, M[:C, :], 0.0)
            u["T"] = eye + top[:, :C]
            u["LL"] = top.astype(BF16)
            u["Lp"] = top[:, :C].astype(BF16)
            u["Tr"] = jnp.where(incl2, M[C:, :], 0.0).astype(BF16)
        yield
        for u in units:
            u["X2"] = _mm(u["LL"], jnp.concatenate([jnp.zeros_like(u["vb"]), u["vb"]], axis=0))
            u["Lp"] = _mm(u["Lp"], u["Lp"]).astype(BF16)
        yield
        for _ in range(4):
            for u in units:
                u["T"] = u["T"] + _mm(u["T"], u["Lp"])
                u["Lp"] = _mm(u["Lp"], u["Lp"]).astype(BF16)
            yield
        for u in units:
            u["T"] = u["T"] + _mm(u["T"], u["Lp"])
        yield
        for u in units:
            u["W"] = _mm(u["T"], jnp.concatenate([u["At"], u["X2"].astype(BF16)], axis=1))
        yield
        for u in units:
            u["S0"] = state_ref[u["slot"]]
            u["S0b"] = u["S0"].astype(BF16)
            U = _mm_nt(u["W"][:, :N], u["S0b"]) + u["W"][:, N:]
            u["UV"] = jnp.concatenate([U.astype(BF16), u["vb"]], axis=0)
        yield
        for u in units:
            u["O"] = _mm_nt(u["Rt"], u["S0b"]) + _mm(u["Tr"], u["UV"])
            state_ref[u["slot"]] = (u["S0"] + _mm_tn(u["UV"], u["BK"])) * u["glast"]
        yield
        o_all = jnp.concatenate([u["O"] for u in units], axis=-1)
        cen = o_all - head_sum(o_all) * (1.0 / N)
        var = head_sum(cen * cen) * (1.0 / N)
        on = cen * lax.rsqrt(var + GN_EPS) * lnw_ref[...] + lnb_ref[...]
        o_ref[bb] = (on + bonus_v) * g
        yield

    live = [(bb, stages(bb)) for bb in range(nb)]
    tick = 0
    while live:
        for item in list(live):
            if tick >= RWKV_STAGE_LAG * item[0] and next(item[1], "done") == "done":
                live.remove(item)
        tick += 1


def _rwkv_mixer(pr, mu, w0, w_up, a0, a_up, g_up, k_k, k_a, r_k, ln_w, ln_b, bsz, seq):
    C = CHUNK
    nb = RWKV_SEQS_PER_STEP if bsz % RWKV_SEQS_PER_STEP == 0 else 1
    row = lambda n: pl.BlockSpec((1, n), lambda b, i: (0, 0))
    full = lambda m, n: pl.BlockSpec((m, n), lambda b, i: (0, 0))
    return pl.pallas_call(
        _rwkv_body,
        name="rwkv_mixer",
        grid=(bsz // nb, seq // C),
        in_specs=[pl.BlockSpec((nb, C, COLS_RWKV), lambda b, i: (b, i, 0)),
                  row(COLS_RWKV), row(D_RWKV), full(DECAY_LORA, D_RWKV), row(D_RWKV),
                  full(AAA_LORA, D_RWKV), full(GATE_LORA, D_RWKV), row(D_RWKV), row(D_RWKV),
                  row(D_RWKV), row(D_RWKV), row(D_RWKV)],
        out_specs=pl.BlockSpec((nb, C, D_RWKV), lambda b, i: (b, i, 0)),
        out_shape=jax.ShapeDtypeStruct((bsz, seq, D_RWKV), F32),
        scratch_shapes=[pltpu.VMEM((nb * N_HEADS, HEAD_DIM, HEAD_DIM), F32),
                        pltpu.VMEM((nb, SUBLANES, COLS_RWKV), F32)],
        compiler_params=_cparams("parallel", "arbitrary"),
    )(pr.reshape(bsz, seq, COLS_RWKV), mu.reshape(1, -1), w0.reshape(1, -1), w_up,
      a0.reshape(1, -1), a_up, g_up, k_k.reshape(1, -1), k_a.reshape(1, -1), r_k.reshape(1, -1),
      ln_w.reshape(1, -1), ln_b.reshape(1, -1))


def _outproj_body(yc_ref, yr_ref, x_ref, wo_ref, gt_ref, g2_ref, sh_ref, sc_ref, rwh_ref, rwl_ref,
                  rb_ref, h_ref, gate_ref, lpos_ref, tab_ref, xl_ref):
    mix = _mm(yc_ref[...], wo_ref[:D_CONV, :]) + _mm(yr_ref[...], wo_ref[D_CONV:, :])
    h = x_ref[...] + gt_ref[0, 0] * mix
    h_ref[...] = h
    u2 = _rms(h) * g2_ref[...]
    u2 = u2 * (1.0 + sc_ref[0, 0]) + sh_ref[0, 0]

    uh, ul = _split2(u2)
    logits = (jnp.dot(uh, rwh_ref[...], preferred_element_type=F32)
              + jnp.dot(uh, rwl_ref[...], preferred_element_type=F32)
              + jnp.dot(ul, rwh_ref[...], preferred_element_type=F32)) + rb_ref[...]
    tm = logits.shape[0]
    lane = lax.broadcasted_iota(I32, (tm, LANES), 1)
    neg = jnp.float32(-jnp.inf)
    l = jnp.where(lane < N_EXPERTS, logits, neg)
    tops, idxs = [], []
    lane_f = lane.astype(F32)
    for _ in range(TOP_K):
        m = jnp.max(l, axis=-1, keepdims=True)
        ix = jnp.min(jnp.where(l == m, lane_f, float(LANES)), axis=-1, keepdims=True).astype(I32)
        tops.append(m)
        idxs.append(ix)
        l = jnp.where(lane == ix, neg, l)
    es = [jnp.exp(t - tops[0]) for t in tops]
    den = es[0] + es[1] + es[2] + es[3]
    gate_tile = jnp.zeros((tm, LANES), F32)
    idx_tile = jnp.zeros((tm, LANES), I32)
    for kq in range(TOP_K):
        gate_tile = jnp.where(lane == kq, es[kq] / den, gate_tile)
        idx_tile = jnp.where(lane == kq, idxs[kq], idx_tile)
    gate_ref[...] = gate_tile
    _sort_tiles(idx_tile, uh, lpos_ref, tab_ref, xl_ref)


def _out_proj_router(yc, yr, xf, w_out_bf, mod6, g2, rw_hi, rw_lo, rb, tiles_per_batch):
    n_tok = xf.shape[0]
    tm = TOKEN_TILE
    n_sub = tm // COMBINE_TILE
    n_seg = D_MODEL // LANES
    tok = lambda n: pl.BlockSpec((tm, n), lambda i: (i, 0))
    modspec = lambda which: pl.BlockSpec((1, 1, 1, D_MODEL),
                                         lambda i: (i // tiles_per_batch, which, 0, 0))
    const = lambda m, n: pl.BlockSpec((m, n), lambda i: (0, 0))
    return pl.pallas_call(
        _outproj_body,
        name="out_proj_router",
        grid=(n_tok // tm,),
        in_specs=[tok(D_CONV), tok(D_RWKV), tok(D_MODEL), const(D_MODEL, D_MODEL),
                  modspec(2), const(1, D_MODEL), modspec(3), modspec(4),
                  const(D_MODEL, LANES), const(D_MODEL, LANES), const(1, LANES)],
        out_specs=[tok(D_MODEL), tok(LANES), tok(LANES),
                   pl.BlockSpec((n_sub, SUBLANES, LANES), lambda i: (i, 0, 0)),
                   pl.BlockSpec((tm * TOP_K * n_seg, LANES), lambda i: (i, 0))],
        out_shape=[jax.ShapeDtypeStruct((n_tok, D_MODEL), F32),
                   jax.ShapeDtypeStruct((n_tok, LANES), F32),
                   jax.ShapeDtypeStruct((n_tok, LANES), F32),
                   jax.ShapeDtypeStruct((n_tok // COMBINE_TILE, SUBLANES, LANES), F32),
                   jax.ShapeDtypeStruct((n_tok * TOP_K * n_seg, LANES), F32)],
        compiler_params=_cparams("parallel"),
    )(yc, yr, xf, w_out_bf, mod6, g2, mod6, mod6, rw_hi, rw_lo, rb)


def _sort_tiles(idx_all, u_all, lpos_ref, tab_ref, xl_ref):
    tm = COMBINE_TILE
    n_sub = idx_all.shape[0] // tm
    n_loc = tm * TOP_K
    n_seg = D_MODEL // LANES
    rr = lax.broadcasted_iota(I32, (tm, tm), 0)
    cc = lax.broadcasted_iota(I32, (tm, tm), 1)
    tri = jnp.where(rr > cc, 1.0, 0.0).astype(BF16)
    lane = lax.broadcasted_iota(I32, (tm, LANES), 1)
    lane8 = lax.broadcasted_iota(I32, (SUBLANES, LANES), 1)
    row8 = lax.broadcasted_iota(I32, (SUBLANES, LANES), 0)
    q = lax.broadcasted_iota(I32, (n_loc, tm), 0)
    subs = [dict(tok=slice(s * tm, (s + 1) * tm)) for s in range(n_sub)]

    for sb in subs:
        idx = idx_all[sb["tok"], :]
        sb["masks"] = [lane == idx[:, kq:kq + 1] for kq in range(TOP_K)]
        sel = jnp.zeros((tm, LANES), F32)
        for mk in sb["masks"]:
            sel = sel + jnp.where(mk, 1.0, 0.0)
        sb["sel"] = sel
    for sb in subs:
        sb["prefix"] = jnp.dot(tri, sb["sel"].astype(BF16), preferred_element_type=F32)
    for s, sb in enumerate(subs):
        cnt = jnp.broadcast_to(jnp.sum(sb["sel"], axis=0, keepdims=True), (SUBLANES, LANES))
        inc = cnt
        for sh in (1, 2, 4, 8, 16, 32, 64):
            inc = inc + jnp.where(lane8 >= sh, pltpu.roll(inc, sh, axis=1), 0.0)
        loff = inc - cnt
        tab_ref[s] = jnp.where(row8 == 0, cnt, loff)
        local_row = loff[0:1, :] + sb["prefix"]
        lpos_tile = jnp.zeros((tm, LANES), F32)
        for kq in range(TOP_K):
            lk = jnp.sum(jnp.where(sb["masks"][kq], local_row, 0.0), axis=-1, keepdims=True)
            lpos_tile = jnp.where(lane == kq, lk, lpos_tile)
        lpos_ref[sb["tok"], :] = lpos_tile
        sb["lpos_t"] = jnp.transpose(lpos_tile)[0:SUBLANES, :].astype(I32)
    for sb in subs:
        pick = jnp.zeros((n_loc, tm), F32)
        for kq in range(TOP_K):
            pick = jnp.where(q == sb["lpos_t"][kq:kq + 1, :], 1.0, pick)
        sb["xl"] = jnp.dot(pick.astype(BF16), u_all[sb["tok"], :], preferred_element_type=F32)
    for s, sb in enumerate(subs):
        for j in range(n_seg):
            xl_ref[pl.ds(s * n_loc * n_seg + j, n_loc, stride=n_seg), :] = (
                sb["xl"][:, j * LANES:(j + 1) * LANES])


def _deinterleave_body(w_ref, o_ref):
    w = w_ref[0].astype(BF16)
    half = o_ref.shape[2] // 2
    src = lax.broadcasted_iota(I32, (2 * MXU_TILE, MXU_TILE), 0)
    dst = lax.broadcasted_iota(I32, (2 * MXU_TILE, MXU_TILE), 1)
    pick_even = jnp.where(src == 2 * dst, 1.0, 0.0).astype(BF16)
    pick_odd = jnp.where(src == 2 * dst + 1, 1.0, 0.0).astype(BF16)
    for j in range(half // MXU_TILE):
        slab = w[:, 2 * MXU_TILE * j:2 * MXU_TILE * (j + 1)]
        lo = MXU_TILE * j
        o_ref[0, :, lo:lo + MXU_TILE] = jnp.dot(
            slab, pick_even, preferred_element_type=F32).astype(BF16)
        o_ref[0, :, half + lo:half + lo + MXU_TILE] = jnp.dot(
            slab, pick_odd, preferred_element_type=F32).astype(BF16)


def _deinterleave_w1(w1):
    n_e, d_in, two_f = w1.shape
    tr = 512
    return pl.pallas_call(
        _deinterleave_body,
        name="deinterleave_w1",
        grid=(n_e, d_in // tr),
        in_specs=[pl.BlockSpec((1, tr, two_f), lambda e, r: (e, r, 0))],
        out_specs=pl.BlockSpec((1, tr, two_f), lambda e, r: (e, r, 0)),
        out_shape=jax.ShapeDtypeStruct((n_e, d_in, two_f), BF16),
        compiler_params=_cparams("parallel", "parallel"),
    )(w1)


def _expert_body(be_ref, nu_ref, ft_ref, lt_ref, nv_ref, rg_ref, rl_ref, rc_ref,
                 xl_ref, w1_ref, w2_ref, bg_ref, bl_ref, b2_ref, y_ref, xbuf_ref, sem):
    i = pl.program_id(0)
    n_used = nu_ref[0]
    bm = ROW_BLOCK
    n_seg = D_MODEL // LANES
    n_loc = COMBINE_TILE * TOP_K

    def issue(blk, slot):
        e = be_ref[blk]
        r0 = blk * bm

        def piece(t, carry):
            g = rg_ref[t * N_EXPERTS + e]
            s = jnp.maximum(g, r0)
            n = jnp.minimum(g + rc_ref[t * N_EXPERTS + e], r0 + bm) - s

            @pl.when(n > 0)
            def _():
                src = pl.multiple_of((t * n_loc + rl_ref[t * N_EXPERTS + e] + s - g) * n_seg, n_seg)
                dst = pl.multiple_of((s - r0) * n_seg, n_seg)
                pltpu.make_async_copy(xl_ref.at[pl.ds(src, n * n_seg)],
                                      xbuf_ref.at[slot, pl.ds(dst, n * n_seg)], sem.at[slot]).start()
            return carry

        lax.fori_loop(ft_ref[blk], lt_ref[blk] + 1, piece, 0)

    @pl.when(i == 0)
    def _():
        xbuf_ref[...] = jnp.zeros_like(xbuf_ref)
        issue(0, 0)

    @pl.when(i + 1 < n_used)
    def _():
        issue(i + 1, (i + 1) % 2)

    @pl.when(i >= n_used)
    def _():
        y_ref[...] = jnp.zeros_like(y_ref)

    @pl.when(i < n_used)
    def _():
        slot = i % 2
        n_valid = nv_ref[i] * n_seg
        pltpu.make_async_copy(xl_ref.at[pl.ds(0, n_valid)], xbuf_ref.at[slot, pl.ds(0, n_valid)],
                              sem.at[slot]).wait()
        x = jnp.concatenate([xbuf_ref[slot, pl.ds(j, bm, stride=n_seg), :] for j in range(n_seg)],
                            axis=1).astype(BF16)
        hb = jnp.dot(x, w1_ref[0], preferred_element_type=F32)
        hg = hb[:, :D_EXPERT] + bg_ref[0]
        hl = hb[:, D_EXPERT:] + bl_ref[0]
        xg = jnp.minimum(hg, SWIGLU_LIMIT)
        xl = jnp.clip(hl, -SWIGLU_LIMIT, SWIGLU_LIMIT)
        act = xg * _sigmoid(SWIGLU_ALPHA * xg) * (xl + 1.0)
        y = jnp.dot(act.astype(BF16), w2_ref[0].astype(BF16),
                    preferred_element_type=F32) + b2_ref[0]
        for j in range(n_seg):
            y_ref[pl.ds(j, bm, stride=n_seg), :] = y[:, j * LANES:(j + 1) * LANES]


def _experts(x_local, block_tables, run_tables, w1d, w2, b1g, b1l, b2, m_pad):
    bm = ROW_BLOCK
    n_seg = D_MODEL // LANES
    wsel = lambda i, be, *_: (be[i], 0, 0)
    grid_spec = pltpu.PrefetchScalarGridSpec(
        num_scalar_prefetch=8,
        grid=(m_pad // bm,),
        in_specs=[pl.BlockSpec(memory_space=pl.ANY),
                  pl.BlockSpec((1, D_MODEL, 2 * D_EXPERT), wsel),
                  pl.BlockSpec((1, D_EXPERT, D_MODEL), wsel),
                  pl.BlockSpec((1, 1, D_EXPERT), wsel),
                  pl.BlockSpec((1, 1, D_EXPERT), wsel),
                  pl.BlockSpec((1, 1, D_MODEL), wsel)],
        out_specs=pl.BlockSpec((bm * n_seg, LANES), lambda i, *_: (i, 0)),
        scratch_shapes=[pltpu.VMEM((2, bm * n_seg, LANES), F32), pltpu.SemaphoreType.DMA((2,))],
    )
    return pl.pallas_call(
        _expert_body,
        name="expert_mlp",
        grid_spec=grid_spec,
        out_shape=jax.ShapeDtypeStruct((m_pad * n_seg, LANES), F32),
        compiler_params=_cparams("arbitrary"),
    )(*block_tables, *run_tables, x_local, w1d, w2, b1g, b1l, b2)


def _combine_body(rg_ref, rl_ref, rc_ref, y_ref, lpos_ref, gate_ref, h_ref, gt_ref, fg_ref, o_ref,
                  yloc_ref, sem):
    i = pl.program_id(0)
    tm = COMBINE_TILE
    n_sub = h_ref.shape[0] // tm
    n_seg = D_MODEL // LANES
    n_loc = tm * TOP_K

    def issue(step, slot):
        for s in range(n_sub):
            tile = step * n_sub + s
            for e in range(N_EXPERTS):
                c = rc_ref[tile * N_EXPERTS + e] * n_seg

                @pl.when(c > 0)
                def _():
                    src = pl.multiple_of(rg_ref[tile * N_EXPERTS + e] * n_seg, n_seg)
                    dst = pl.multiple_of((s * n_loc + rl_ref[tile * N_EXPERTS + e]) * n_seg, n_seg)
                    pltpu.make_async_copy(y_ref.at[pl.ds(src, c)], yloc_ref.at[slot, pl.ds(dst, c)],
                                          sem.at[slot]).start()

    @pl.when(i == 0)
    def _():
        issue(0, 0)

    @pl.when(i + 1 < pl.num_programs(0))
    def _():
        issue(i + 1, (i + 1) % 2)

    slot = i % 2
    pltpu.make_async_copy(y_ref.at[pl.ds(0, n_sub * n_loc * n_seg)], yloc_ref.at[slot],
                          sem.at[slot]).wait()

    q = lax.broadcasted_iota(I32, (tm, n_loc), 1)
    subs = [dict(tok=slice(s * tm, (s + 1) * tm)) for s in range(n_sub)]
    for s, sb in enumerate(subs):
        sb["y"] = jnp.concatenate(
            [yloc_ref[slot, pl.ds(s * n_loc * n_seg + j, n_loc, stride=n_seg), :]
             for j in range(n_seg)], axis=1).astype(BF16)
        lpos = lpos_ref[sb["tok"], :].astype(I32)
        gate = gate_ref[sb["tok"], :]
        gmat = jnp.zeros((tm, n_loc), F32)
        for kq in range(TOP_K):
            gmat = jnp.where(q == lpos[:, kq:kq + 1], gate[:, kq:kq + 1], gmat)
        sb["ghi"], sb["glo"] = _split2(gmat)
    for sb in subs:
        sb["moe"] = (jnp.dot(sb["ghi"], sb["y"], preferred_element_type=F32)
                     + jnp.dot(sb["glo"], sb["y"], preferred_element_type=F32))
    for sb in subs:
        h = h_ref[sb["tok"], :] + gt_ref[0, 0] * sb["moe"]
        o_ref[sb["tok"], :] = _rms(h) * fg_ref[...]


def _combine(y_rows, run_start, run_local, run_count, lpos, gate_tile, h, mod6, final_g,
             tiles_per_batch):
    n_tok = h.shape[0]
    tm = COMBINE_TILE * COMBINE_TILES_PER_STEP
    grid_spec = pltpu.PrefetchScalarGridSpec(
        num_scalar_prefetch=3,
        grid=(n_tok // tm,),
        in_specs=[pl.BlockSpec(memory_space=pl.ANY),
                  pl.BlockSpec((tm, LANES), lambda i, *_: (i, 0)),
                  pl.BlockSpec((tm, LANES), lambda i, *_: (i, 0)),
                  pl.BlockSpec((tm, D_MODEL), lambda i, *_: (i, 0)),
                  pl.BlockSpec((1, 1, 1, D_MODEL), lambda i, *_: (i // tiles_per_batch, 5, 0, 0)),
                  pl.BlockSpec((1, D_MODEL), lambda i, *_: (0, 0))],
        out_specs=pl.BlockSpec((tm, D_MODEL), lambda i, *_: (i, 0)),
        scratch_shapes=[pltpu.VMEM((2, tm * TOP_K * (D_MODEL // LANES), LANES), F32),
                        pltpu.SemaphoreType.DMA((2,))],
    )
    return pl.pallas_call(
        _combine_body,
        name="combine_rows",
        grid_spec=grid_spec,
        out_shape=jax.ShapeDtypeStruct((n_tok, D_MODEL), F32),
        compiler_params=_cparams("arbitrary"),
    )(run_start, run_local, run_count, y_rows, lpos, gate_tile, h, mod6, final_g)


def kernel(x, c, ada_w, ada_b, norm1_g, w_in, conv_w, conv_gn, rwkv_mu, rwkv_w0, rwkv_w_up,
           rwkv_a0, rwkv_a_up, rwkv_g_up, rwkv_k_k, rwkv_k_a, rwkv_r_k, rwkv_ln_w, rwkv_ln_b,
           w_out, norm2_g, router_w, router_b, exp_w1, exp_b1, exp_w2, exp_b2, final_g):
    bsz, seq, _ = x.shape
    depth = ada_w.shape[0]
    assert depth == 1, "the final norm is fused into the last layer's combine step"
    n_tok = bsz * seq
    assert seq % TOKEN_TILE == 0 and seq % CHUNK == 0 and seq % COMBINE_TILE == 0
    n_rows = n_tok * TOP_K
    assert n_rows % ROW_BLOCK == 0
    n_blocks = n_rows // ROW_BLOCK + N_EXPERTS
    m_pad = n_blocks * ROW_BLOCK

    h = x.reshape(n_tok, D_MODEL)
    out = h
    for l in range(depth):
        mod6 = _ada_mod(c, ada_w[l], ada_b[l]).reshape(bsz, 6, 1, D_MODEL)
        yc, pr = _in_proj_conv(h.reshape(bsz, seq, D_MODEL), norm1_g[l].reshape(1, -1), mod6,
                               w_in[l].astype(BF16), conv_w[l], conv_gn[l].reshape(1, -1))
        yr = _rwkv_mixer(pr, rwkv_mu[l], rwkv_w0[l], rwkv_w_up[l], rwkv_a0[l], rwkv_a_up[l],
                         rwkv_g_up[l], rwkv_k_k[l], rwkv_k_a[l], rwkv_r_k[l], rwkv_ln_w[l],
                         rwkv_ln_b[l], bsz, seq)
        rw = jnp.pad(router_w[l], ((0, 0), (0, LANES - N_EXPERTS)))
        rw_hi = rw.astype(BF16)
        rw_lo = (rw - rw_hi.astype(F32)).astype(BF16)
        rb = jnp.pad(router_b[l], (0, LANES - N_EXPERTS)).reshape(1, LANES)
        h, gate_tile, lpos, tab, x_local = _out_proj_router(
            yc.reshape(n_tok, D_CONV), yr.reshape(n_tok, D_RWKV), h, w_out[l].astype(BF16), mod6,
            norm2_g[l].reshape(1, -1), rw_hi, rw_lo, rb, seq // TOKEN_TILE)
        run_count = tab[:, 0, :N_EXPERTS].astype(I32)
        run_local = tab[:, 1, :N_EXPERTS].astype(I32)
        counts = jnp.sum(run_count, axis=0)
        padded = (counts + ROW_BLOCK - 1) // ROW_BLOCK * ROW_BLOCK
        pad_end = jnp.cumsum(padded)
        pad_start = pad_end - padded
        run_start = pad_start[None, :] + jnp.cumsum(run_count, axis=0) - run_count
        n_used = (pad_end[-1] // ROW_BLOCK).astype(I32).reshape(1)
        blk_row = jnp.minimum(jnp.arange(n_blocks, dtype=I32), n_used[0] - 1) * ROW_BLOCK
        block_expert = jnp.minimum(
            jnp.sum((pad_end[None, :] <= blk_row[:, None]).astype(I32), axis=1), N_EXPERTS - 1)
        blk_start = run_start.T[block_expert]
        blk_end = blk_start + run_count.T[block_expert]
        first_tile = jnp.sum((blk_end <= blk_row[:, None]).astype(I32), axis=1)
        last_tile = jnp.sum((blk_start < blk_row[:, None] + ROW_BLOCK).astype(I32), axis=1) - 1
        n_valid = jnp.clip((pad_start + counts)[block_expert] - blk_row, 0, ROW_BLOCK).astype(I32)
        run_tables = (run_start.reshape(-1), run_local.reshape(-1), run_count.reshape(-1))
        y_rows = _experts(x_local, (block_expert, n_used, first_tile, last_tile, n_valid), run_tables,
                          _deinterleave_w1(exp_w1[l]), exp_w2[l],
                          exp_b1[l][:, None, 0::2], exp_b1[l][:, None, 1::2], exp_b2[l][:, None, :],
                          m_pad)
        out = _combine(y_rows, *run_tables, lpos, gate_tile, h, mod6, final_g.reshape(1, -1),
                       seq // (COMBINE_TILE * COMBINE_TILES_PER_STEP))
    return out.reshape(bsz, seq, D_MODEL)
```

```python
import jax
import jax.numpy as jnp
from jax import lax
from jax.experimental import pallas as pl
from jax.experimental.pallas import tpu as pltpu

F32 = jnp.float32
BF16 = jnp.bfloat16
I32 = jnp.int32

D_MODEL = 1024
D_CONV = 512
CONV_GROUP_DIM = 64
D_RWKV = 512
HEAD_DIM = 64
N_HEADS = D_RWKV // HEAD_DIM
DECAY_LORA = 64
AAA_LORA = 64
GATE_LORA = 128
COLS_CONV = 3 * D_CONV
COLS_RWKV = 3 * D_RWKV + DECAY_LORA + AAA_LORA + GATE_LORA
N_EXPERTS = 32
TOP_K = 4
D_EXPERT = D_MODEL
SWIGLU_LIMIT = 7.0
SWIGLU_ALPHA = 1.702
NORM_EPS = 1e-5
GN_EPS = 64e-5

LANES = 128
SUBLANES = 8
MXU_TILE = 256
CHUNK = 64
RWKV_SEQS_PER_STEP = 8
RWKV_STAGE_LAG = 3
ROW_BLOCK = 512
TOKEN_TILE = 512
COMBINE_TILE = 256
COMBINE_TILES_PER_STEP = 2
VMEM_LIMIT = 48 * 1024 * 1024


def _cparams(*sem):
    return pltpu.CompilerParams(dimension_semantics=sem, vmem_limit_bytes=VMEM_LIMIT)


def _mm(a, b):
    return jnp.dot(a.astype(BF16), b.astype(BF16), preferred_element_type=F32)


def _mm_nt(a, b):
    return lax.dot_general(a.astype(BF16), b.astype(BF16), (((1,), (1,)), ((), ())),
                           preferred_element_type=F32)


def _mm_tn(a, b):
    return lax.dot_general(a.astype(BF16), b.astype(BF16), (((0,), (0,)), ((), ())),
                           preferred_element_type=F32)


def _split2(x):
    hi = x.astype(BF16)
    lo = (x - hi.astype(F32)).astype(BF16)
    return hi, lo


def _split3(x):
    hi = x.astype(BF16)
    r = x - hi.astype(F32)
    mid = r.astype(BF16)
    lo = (r - mid.astype(F32)).astype(BF16)
    return hi, mid, lo


def _sigmoid(x):
    return 1.0 / (1.0 + jnp.exp(-x))


def _rms(x):
    return x * lax.rsqrt(jnp.mean(x * x, axis=-1, keepdims=True) + NORM_EPS)


def _ada_body(c_ref, w_ref, b_ref, o_ref):
    c = c_ref[...]
    ca = c * _sigmoid(c)
    ah, al = _split2(ca)
    wh, wl = _split2(w_ref[...])
    acc = jnp.dot(ah, wh, preferred_element_type=F32)
    acc += jnp.dot(ah, wl, preferred_element_type=F32)
    acc += jnp.dot(al, wh, preferred_element_type=F32)
    o_ref[...] = acc + b_ref[...]


def _ada_mod(c, ada_w, ada_b):
    bsz = c.shape[0]
    n = ada_w.shape[1]
    tn = 1024
    return pl.pallas_call(
        _ada_body,
        name="ada_mod",
        grid=(n // tn,),
        in_specs=[pl.BlockSpec((bsz, D_MODEL), lambda j: (0, 0)),
                  pl.BlockSpec((D_MODEL, tn), lambda j: (0, j)),
                  pl.BlockSpec((1, tn), lambda j: (0, j))],
        out_specs=pl.BlockSpec((bsz, tn), lambda j: (0, j)),
        out_shape=jax.ShapeDtypeStruct((bsz, n), F32),
        compiler_params=_cparams("parallel"),
    )(c, ada_w, ada_b.reshape(1, n))


def _group_sums(x, group):
    li = lax.broadcasted_iota(I32, (LANES, LANES), 0)
    lj = lax.broadcasted_iota(I32, (LANES, LANES), 1)
    same = jnp.where((li < group) == (lj < group), 1.0, 0.0).astype(BF16)
    hi, lo = _split2(x)
    cols = []
    for t in range(x.shape[1] // LANES):
        ts = slice(t * LANES, (t + 1) * LANES)
        cols.append(jnp.dot(hi[:, ts], same, preferred_element_type=F32)
                    + jnp.dot(lo[:, ts], same, preferred_element_type=F32))
    return jnp.concatenate(cols, axis=1)


def _inproj_body(x_ref, g_ref, sh_ref, sc_ref, w_ref, cw_ref, gn_ref, yc_ref, or_ref, carry_ref):
    @pl.when(pl.program_id(1) == 0)
    def _():
        carry_ref[...] = jnp.zeros_like(carry_ref)

    ub = _rms(x_ref[0]) * g_ref[...]
    ub = (ub * (1.0 + sc_ref[0, 0]) + sh_ref[0, 0]).astype(BF16)
    p = jnp.dot(ub, w_ref[:, :COLS_CONV], preferred_element_type=F32)
    or_ref[0] = jnp.dot(ub, w_ref[:, COLS_CONV:], preferred_element_type=F32)

    tc = p.shape[0]
    b_gate = p[:, :D_CONV]
    u = p[:, D_CONV:2 * D_CONV] * p[:, 2 * D_CONV:]
    ext = jnp.concatenate([carry_ref[...], u], axis=0)
    u1 = pltpu.roll(ext, 1, axis=0)[SUBLANES:]
    u2 = pltpu.roll(ext, 2, axis=0)[SUBLANES:]
    carry_ref[...] = u[tc - SUBLANES:, :]
    cw = cw_ref[...]
    y = b_gate * (cw[0:1] * u2 + cw[1:2] * u1 + cw[2:3] * u)
    ms = _group_sums(y * y, CONV_GROUP_DIM) * (1.0 / CONV_GROUP_DIM)
    yc_ref[0] = y * lax.rsqrt(ms + NORM_EPS) * gn_ref[...]


def _in_proj_conv(x, g1, mod6, w_in_bf, conv_w, conv_gn):
    bsz, seq, _ = x.shape
    tm = TOKEN_TILE
    n_in = w_in_bf.shape[1]
    const = lambda m, n: pl.BlockSpec((m, n), lambda b, i: (0, 0))
    return pl.pallas_call(
        _inproj_body,
        name="in_proj_conv",
        grid=(bsz, seq // tm),
        in_specs=[pl.BlockSpec((1, tm, D_MODEL), lambda b, i: (b, i, 0)),
                  const(1, D_MODEL),
                  pl.BlockSpec((1, 1, 1, D_MODEL), lambda b, i: (b, 0, 0, 0)),
                  pl.BlockSpec((1, 1, 1, D_MODEL), lambda b, i: (b, 1, 0, 0)),
                  const(D_MODEL, n_in), const(3, D_CONV), const(1, D_CONV)],
        out_specs=[pl.BlockSpec((1, tm, D_CONV), lambda b, i: (b, i, 0)),
                   pl.BlockSpec((1, tm, COLS_RWKV), lambda b, i: (b, i, 0))],
        out_shape=[jax.ShapeDtypeStruct((bsz, seq, D_CONV), F32),
                   jax.ShapeDtypeStruct((bsz, seq, COLS_RWKV), F32)],
        scratch_shapes=[pltpu.VMEM((SUBLANES, D_CONV), F32)],
        compiler_params=_cparams("parallel", "arbitrary"),
    )(x, g1, mod6, mod6, w_in_bf, conv_w, conv_gn)


def _rwkv_body(p_ref, mu_ref, w0_ref, wup_ref, a0_ref, aup_ref, gup_ref, kkw_ref, kaw_ref,
               rkw_ref, lnw_ref, lnb_ref, o_ref, state_ref, prev_ref):
    @pl.when(pl.program_id(1) == 0)
    def _():
        state_ref[...] = jnp.zeros_like(state_ref)
        prev_ref[...] = jnp.zeros_like(prev_ref)

    C = CHUNK
    N = HEAD_DIM
    nb = p_ref.shape[0]
    s1, s2, s3 = D_RWKV, 2 * D_RWKV, 3 * D_RWKV
    row1 = lax.broadcasted_iota(I32, (C, 1), 0)
    ri = lax.broadcasted_iota(I32, (C, C), 0)
    ci = lax.broadcasted_iota(I32, (C, C), 1)
    strict = ri > ci
    incl = ri >= ci
    ri2 = lax.broadcasted_iota(I32, (C, 2 * C), 0)
    ci2 = lax.broadcasted_iota(I32, (C, 2 * C), 1)
    ci2 = jnp.where(ci2 >= C, ci2 - C, ci2)
    strict2 = ri2 > ci2
    incl2 = ri2 >= ci2
    eye = jnp.where(ri == ci, 1.0, 0.0).astype(F32)
    tri = jnp.where(incl, 1.0, 0.0).astype(BF16)

    def stages(bb):
        units = []
        P = p_ref[bb]
        Pprev = jnp.where(row1 == 0, prev_ref[bb, 0:1, :], pltpu.roll(P, 1, axis=0))
        prev_ref[bb] = jnp.broadcast_to(P[C - 1:C, :], prev_ref.shape[1:])
        p = P + (Pprev - P) * mu_ref[...]
        r = p[:, :s1]
        k = p[:, s1:s2]
        v = p[:, s2:s3]
        dw = p[:, s3:s3 + DECAY_LORA]
        da = p[:, s3 + DECAY_LORA:s3 + DECAY_LORA + AAA_LORA]
        dg = p[:, s3 + DECAY_LORA + AAA_LORA:]
        z = -(w0_ref[...] + _mm(jnp.tanh(dw), wup_ref[...]))
        softplus = jnp.maximum(z, 0.0) + jnp.log(1.0 + jnp.exp(-jnp.abs(z)))
        logw = -jnp.exp(-softplus - 0.5)
        a = _sigmoid(a0_ref[...] + _mm(da, aup_ref[...]))
        g = _mm(_sigmoid(dg), gup_ref[...])
        kk = k * kkw_ref[...]
        kmod = k * (1.0 + (a - 1.0) * kaw_ref[...])
        cum = sum(jnp.dot(tri, part, preferred_element_type=F32) for part in _split3(logw))
        gam = jnp.exp(cum)
        gam_prev = jnp.exp(cum - logw)
        ginv = jnp.exp(-cum)
        def head_sum(x):
            return _group_sums(x, N)

        kkn = kk * lax.rsqrt(jnp.maximum(head_sum(kk * kk), 1e-24))
        bonus_v = head_sum(r * kmod * rkw_ref[...]) * v
        at_all = (-kkn * gam_prev).astype(BF16)
        rt_all = (r * gam).astype(BF16)
        bt_all = (kkn * a * ginv).astype(BF16)
        kt_all = (kmod * ginv).astype(BF16)
        vb_all = v.astype(BF16)
        for h in range(N_HEADS):
            sl = slice(h * N, (h + 1) * N)
            At, Rt = at_all[:, sl], rt_all[:, sl]
            units.append(dict(
                slot=bb * N_HEADS + h, At=At, Rt=Rt, vb=vb_all[:, sl],
                BK=jnp.concatenate([bt_all[:, sl], kt_all[:, sl]], axis=0),
                AR=jnp.concatenate([At, Rt], axis=0), glast=gam[C - 1:C, sl]))
        yield
        for u in units:
            M = _mm_nt(u["AR"], u["BK"])
            top = jnp.where(strict2, M[:C, :], 0.0)
            u["T"] = eye + top[:, :C]
            u["LL"] = top.astype(BF16)
            u["Lp"] = top[:, :C].astype(BF16)
            u["Tr"] = jnp.where(incl2, M[C:, :], 0.0).astype(BF16)
        yield
        for u in units:
            u["X2"] = _mm(u["LL"], jnp.concatenate([jnp.zeros_like(u["vb"]), u["vb"]], axis=0))
            u["Lp"] = _mm(u["Lp"], u["Lp"]).astype(BF16)
        yield
        for _ in range(4):
            for u in units:
                u["T"] = u["T"] + _mm(u["T"], u["Lp"])
                u["Lp"] = _mm(u["Lp"], u["Lp"]).astype(BF16)
            yield
        for u in units:
            u["T"] = u["T"] + _mm(u["T"], u["Lp"])
        yield
        for u in units:
            u["W"] = _mm(u["T"], jnp.concatenate([u["At"], u["X2"].astype(BF16)], axis=1))
        yield
        for u in units:
            u["S0"] = state_ref[u["slot"]]
            u["S0b"] = u["S0"].astype(BF16)
            U = _mm_nt(u["W"][:, :N], u["S0b"]) + u["W"][:, N:]
            u["UV"] = jnp.concatenate([U.astype(BF16), u["vb"]], axis=0)
        yield
        for u in units:
            u["O"] = _mm_nt(u["Rt"], u["S0b"]) + _mm(u["Tr"], u["UV"])
            state_ref[u["slot"]] = (u["S0"] + _mm_tn(u["UV"], u["BK"])) * u["glast"]
        yield
        o_all = jnp.concatenate([u["O"] for u in units], axis=-1)
        cen = o_all - head_sum(o_all) * (1.0 / N)
        var = head_sum(cen * cen) * (1.0 / N)
        on = cen * lax.rsqrt(var + GN_EPS) * lnw_ref[...] + lnb_ref[...]
        o_ref[bb] = (on + bonus_v) * g
        yield

    live = [(bb, stages(bb)) for bb in range(nb)]
    tick = 0
    while live:
        for item in list(live):
            if tick >= RWKV_STAGE_LAG * item[0] and next(item[1], "done") == "done":
                live.remove(item)
        tick += 1


def _rwkv_mixer(pr, mu, w0, w_up, a0, a_up, g_up, k_k, k_a, r_k, ln_w, ln_b, bsz, seq):
    C = CHUNK
    nb = RWKV_SEQS_PER_STEP if bsz % RWKV_SEQS_PER_STEP == 0 else 1
    row = lambda n: pl.BlockSpec((1, n), lambda b, i: (0, 0))
    full = lambda m, n: pl.BlockSpec((m, n), lambda b, i: (0, 0))
    return pl.pallas_call(
        _rwkv_body,
        name="rwkv_mixer",
        grid=(bsz // nb, seq // C),
        in_specs=[pl.BlockSpec((nb, C, COLS_RWKV), lambda b, i: (b, i, 0)),
                  row(COLS_RWKV), row(D_RWKV), full(DECAY_LORA, D_RWKV), row(D_RWKV),
                  full(AAA_LORA, D_RWKV), full(GATE_LORA, D_RWKV), row(D_RWKV), row(D_RWKV),
                  row(D_RWKV), row(D_RWKV), row(D_RWKV)],
        out_specs=pl.BlockSpec((nb, C, D_RWKV), lambda b, i: (b, i, 0)),
        out_shape=jax.ShapeDtypeStruct((bsz, seq, D_RWKV), F32),
        scratch_shapes=[pltpu.VMEM((nb * N_HEADS, HEAD_DIM, HEAD_DIM), F32),
                        pltpu.VMEM((nb, SUBLANES, COLS_RWKV), F32)],
        compiler_params=_cparams("parallel", "arbitrary"),
    )(pr.reshape(bsz, seq, COLS_RWKV), mu.reshape(1, -1), w0.reshape(1, -1), w_up,
      a0.reshape(1, -1), a_up, g_up, k_k.reshape(1, -1), k_a.reshape(1, -1), r_k.reshape(1, -1),
      ln_w.reshape(1, -1), ln_b.reshape(1, -1))


def _outproj_body(yc_ref, yr_ref, x_ref, wo_ref, gt_ref, g2_ref, sh_ref, sc_ref, rwh_ref, rwl_ref,
                  rb_ref, h_ref, route_ref, tab_ref, xl_ref):
    mix = _mm(yc_ref[...], wo_ref[:D_CONV, :]) + _mm(yr_ref[...], wo_ref[D_CONV:, :])
    h = x_ref[...] + gt_ref[0, 0] * mix
    h_ref[...] = h
    u2 = _rms(h) * g2_ref[...]
    u2 = u2 * (1.0 + sc_ref[0, 0]) + sh_ref[0, 0]

    uh, ul = _split2(u2)
    tm = u2.shape[0]
    logits = (_mm_nt(rwh_ref[...], uh) + _mm_nt(rwl_ref[...], uh) + _mm_nt(rwh_ref[...], ul)
              + jnp.concatenate([rb_ref[...]] * (tm // LANES), axis=1))
    eid = lax.broadcasted_iota(I32, (N_EXPERTS, tm), 0).astype(F32)
    neg = jnp.float32(-jnp.inf)
    l = logits
    tops, idxs = [], []
    for _ in range(TOP_K):
        m = jnp.max(l, axis=0, keepdims=True)
        ix = jnp.min(jnp.where(l == m, eid, float(N_EXPERTS)), axis=0, keepdims=True)
        tops.append(m)
        idxs.append(ix)
        l = jnp.where(eid == ix, neg, l)
    es = [jnp.exp(t - tops[0]) for t in tops]
    den = es[0] + es[1] + es[2] + es[3]
    gates = [e / den for e in es]
    lpos_rows = _sort_tiles(eid, idxs, uh, tab_ref, xl_ref)
    row8 = lax.broadcasted_iota(I32, (SUBLANES, tm), 0)
    packed = jnp.zeros((SUBLANES, tm), F32)
    for kq, val in enumerate(gates + lpos_rows):
        packed = jnp.where(row8 == kq, val, packed)
    route_ref[...] = jnp.transpose(
        jnp.concatenate([packed, jnp.zeros((LANES - SUBLANES, tm), F32)], axis=0))


def _out_proj_router(yc, yr, xf, w_out_bf, mod6, g2, rw_hi, rw_lo, rb, tiles_per_batch):
    n_tok = xf.shape[0]
    tm = TOKEN_TILE
    n_sub = tm // COMBINE_TILE
    n_seg = D_MODEL // LANES
    tok = lambda n: pl.BlockSpec((tm, n), lambda i: (i, 0))
    modspec = lambda which: pl.BlockSpec((1, 1, 1, D_MODEL),
                                         lambda i: (i // tiles_per_batch, which, 0, 0))
    const = lambda m, n: pl.BlockSpec((m, n), lambda i: (0, 0))
    return pl.pallas_call(
        _outproj_body,
        name="out_proj_router",
        grid=(n_tok // tm,),
        in_specs=[tok(D_CONV), tok(D_RWKV), tok(D_MODEL), const(D_MODEL, D_MODEL),
                  modspec(2), const(1, D_MODEL), modspec(3), modspec(4),
                  const(N_EXPERTS, D_MODEL), const(N_EXPERTS, D_MODEL), const(N_EXPERTS, LANES)],
        out_specs=[tok(D_MODEL), tok(LANES),
                   pl.BlockSpec((n_sub, N_EXPERTS, LANES), lambda i: (i, 0, 0)),
                   pl.BlockSpec((tm * TOP_K * n_seg, LANES), lambda i: (i, 0))],
        out_shape=[jax.ShapeDtypeStruct((n_tok, D_MODEL), F32),
                   jax.ShapeDtypeStruct((n_tok, LANES), F32),
                   jax.ShapeDtypeStruct((n_tok // COMBINE_TILE, N_EXPERTS, LANES), F32),
                   jax.ShapeDtypeStruct((n_tok * TOP_K * n_seg, LANES), F32)],
        compiler_params=_cparams("parallel"),
    )(yc, yr, xf, w_out_bf, mod6, g2, mod6, mod6, rw_hi, rw_lo, rb)


def _sort_tiles(eid, idxs, u_all, tab_ref, xl_ref):
    tm = COMBINE_TILE
    n_sub = u_all.shape[0] // tm
    n_loc = tm * TOP_K
    n_seg = D_MODEL // LANES
    rr = lax.broadcasted_iota(I32, (tm, tm), 0)
    cc = lax.broadcasted_iota(I32, (tm, tm), 1)
    before = jnp.where(rr < cc, 1.0, 0.0).astype(BF16)
    ones_col = jnp.ones((tm, LANES), BF16)
    er = lax.broadcasted_iota(I32, (N_EXPERTS, N_EXPERTS), 0)
    ec = lax.broadcasted_iota(I32, (N_EXPERTS, N_EXPERTS), 1)
    lower = jnp.where(er > ec, 1.0, 0.0).astype(BF16)
    lane = lax.broadcasted_iota(I32, (N_EXPERTS, LANES), 1)
    q = lax.broadcasted_iota(I32, (n_loc, tm), 0)
    subs = [dict(tok=slice(s * tm, (s + 1) * tm)) for s in range(n_sub)]
    chosen = [jnp.broadcast_to(ix, eid.shape) for ix in idxs]
    eid_tile = lax.broadcasted_iota(I32, (N_EXPERTS, tm), 0).astype(F32)

    for sb in subs:
        sb["masks"] = [eid_tile == ch[:, sb["tok"]] for ch in chosen]
        sel = jnp.zeros((N_EXPERTS, tm), F32)
        for mk in sb["masks"]:
            sel = sel + jnp.where(mk, 1.0, 0.0)
        sb["sel"] = sel.astype(BF16)
    for sb in subs:
        sb["prefix"] = jnp.dot(sb["sel"], before, preferred_element_type=F32)
        sb["cnt"] = jnp.dot(sb["sel"], ones_col, preferred_element_type=F32)
    for sb in subs:
        sb["loff"] = jnp.dot(lower, sb["cnt"].astype(BF16), preferred_element_type=F32)
    for s, sb in enumerate(subs):
        tab_ref[s] = jnp.where(lane == 0, sb["cnt"], jnp.where(lane == 1, sb["loff"], 0.0))
        local_row = jnp.concatenate([sb["loff"]] * (tm // LANES), axis=1) + sb["prefix"]
        sb["lpos"] = [jnp.sum(jnp.where(mk, local_row, 0.0), axis=0, keepdims=True)
                      for mk in sb["masks"]]
    for sb in subs:
        pick = jnp.zeros((n_loc, tm), F32)
        for lp in sb["lpos"]:
            pick = jnp.where(q == lp.astype(I32), 1.0, pick)
        sb["xl"] = jnp.dot(pick.astype(BF16), u_all[sb["tok"], :], preferred_element_type=F32)
    for s, sb in enumerate(subs):
        for j in range(n_seg):
            xl_ref[pl.ds(s * n_loc * n_seg + j, n_loc, stride=n_seg), :] = (
                sb["xl"][:, j * LANES:(j + 1) * LANES])
    return [jnp.concatenate([jnp.broadcast_to(sb["lpos"][kq], (SUBLANES, tm)) for sb in subs], axis=1)
            for kq in range(TOP_K)]


def _deinterleave_body(w_ref, o_ref):
    w = w_ref[0].astype(BF16)
    half = o_ref.shape[2] // 2
    src = lax.broadcasted_iota(I32, (2 * MXU_TILE, MXU_TILE), 0)
    dst = lax.broadcasted_iota(I32, (2 * MXU_TILE, MXU_TILE), 1)
    pick_even = jnp.where(src == 2 * dst, 1.0, 0.0).astype(BF16)
    pick_odd = jnp.where(src == 2 * dst + 1, 1.0, 0.0).astype(BF16)
    for j in range(half // MXU_TILE):
        slab = w[:, 2 * MXU_TILE * j:2 * MXU_TILE * (j + 1)]
        lo = MXU_TILE * j
        o_ref[0, :, lo:lo + MXU_TILE] = jnp.dot(
            slab, pick_even, preferred_element_type=F32).astype(BF16)
        o_ref[0, :, half + lo:half + lo + MXU_TILE] = jnp.dot(
            slab, pick_odd, preferred_element_type=F32).astype(BF16)


def _deinterleave_w1(w1):
    n_e, d_in, two_f = w1.shape
    tr = 512
    return pl.pallas_call(
        _deinterleave_body,
        name="deinterleave_w1",
        grid=(n_e, d_in // tr),
        in_specs=[pl.BlockSpec((1, tr, two_f), lambda e, r: (e, r, 0))],
        out_specs=pl.BlockSpec((1, tr, two_f), lambda e, r: (e, r, 0)),
        out_shape=jax.ShapeDtypeStruct((n_e, d_in, two_f), BF16),
        compiler_params=_cparams("parallel", "parallel"),
    )(w1)


def _expert_body(be_ref, nu_ref, ft_ref, lt_ref, nv_ref, rg_ref, rl_ref, rc_ref,
                 xl_ref, w1_ref, w2_ref, bg_ref, bl_ref, b2_ref, y_ref, xbuf_ref, sem):
    i = pl.program_id(0)
    n_used = nu_ref[0]
    bm = ROW_BLOCK
    n_seg = D_MODEL // LANES
    n_loc = COMBINE_TILE * TOP_K

    def issue(blk, slot):
        e = be_ref[blk]
        r0 = blk * bm

        def piece(t, carry):
            g = rg_ref[t * N_EXPERTS + e]
            s = jnp.maximum(g, r0)
            n = jnp.minimum(g + rc_ref[t * N_EXPERTS + e], r0 + bm) - s

            @pl.when(n > 0)
            def _():
                src = pl.multiple_of((t * n_loc + rl_ref[t * N_EXPERTS + e] + s - g) * n_seg, n_seg)
                dst = pl.multiple_of((s - r0) * n_seg, n_seg)
                pltpu.make_async_copy(xl_ref.at[pl.ds(src, n * n_seg)],
                                      xbuf_ref.at[slot, pl.ds(dst, n * n_seg)], sem.at[slot]).start()
            return carry

        lax.fori_loop(ft_ref[blk], lt_ref[blk] + 1, piece, 0)

    @pl.when(i == 0)
    def _():
        xbuf_ref[...] = jnp.zeros_like(xbuf_ref)
        issue(0, 0)

    @pl.when(i + 1 < n_used)
    def _():
        issue(i + 1, (i + 1) % 2)

    @pl.when(i >= n_used)
    def _():
        y_ref[...] = jnp.zeros_like(y_ref)

    @pl.when(i < n_used)
    def _():
        slot = i % 2
        n_valid = nv_ref[i] * n_seg
        pltpu.make_async_copy(xl_ref.at[pl.ds(0, n_valid)], xbuf_ref.at[slot, pl.ds(0, n_valid)],
                              sem.at[slot]).wait()
        x = jnp.concatenate([xbuf_ref[slot, pl.ds(j, bm, stride=n_seg), :] for j in range(n_seg)],
                            axis=1).astype(BF16)
        hb = jnp.dot(x, w1_ref[0], preferred_element_type=F32)
        hg = hb[:, :D_EXPERT] + bg_ref[0]
        hl = hb[:, D_EXPERT:] + bl_ref[0]
        xg = jnp.minimum(hg, SWIGLU_LIMIT)
        xl = jnp.clip(hl, -SWIGLU_LIMIT, SWIGLU_LIMIT)
        act = xg * _sigmoid(SWIGLU_ALPHA * xg) * (xl + 1.0)
        y = jnp.dot(act.astype(BF16), w2_ref[0].astype(BF16),
                    preferred_element_type=F32) + b2_ref[0]
        for j in range(n_seg):
            y_ref[pl.ds(j, bm, stride=n_seg), :] = y[:, j * LANES:(j + 1) * LANES]


def _experts(x_local, block_tables, run_tables, w1d, w2, b1g, b1l, b2, m_pad):
    bm = ROW_BLOCK
    n_seg = D_MODEL // LANES
    wsel = lambda i, be, *_: (be[i], 0, 0)
    grid_spec = pltpu.PrefetchScalarGridSpec(
        num_scalar_prefetch=8,
        grid=(m_pad // bm,),
        in_specs=[pl.BlockSpec(memory_space=pl.ANY),
                  pl.BlockSpec((1, D_MODEL, 2 * D_EXPERT), wsel),
                  pl.BlockSpec((1, D_EXPERT, D_MODEL), wsel),
                  pl.BlockSpec((1, 1, D_EXPERT), wsel),
                  pl.BlockSpec((1, 1, D_EXPERT), wsel),
                  pl.BlockSpec((1, 1, D_MODEL), wsel)],
        out_specs=pl.BlockSpec((bm * n_seg, LANES), lambda i, *_: (i, 0)),
        scratch_shapes=[pltpu.VMEM((2, bm * n_seg, LANES), F32), pltpu.SemaphoreType.DMA((2,))],
    )
    return pl.pallas_call(
        _expert_body,
        name="expert_mlp",
        grid_spec=grid_spec,
        out_shape=jax.ShapeDtypeStruct((m_pad * n_seg, LANES), F32),
        compiler_params=_cparams("arbitrary"),
    )(*block_tables, *run_tables, x_local, w1d, w2, b1g, b1l, b2)


def _combine_body(rg_ref, rl_ref, rc_ref, y_ref, route_ref, h_ref, gt_ref, fg_ref, o_ref,
                  yloc_ref, sem):
    i = pl.program_id(0)
    tm = COMBINE_TILE
    n_sub = h_ref.shape[0] // tm
    n_seg = D_MODEL // LANES
    n_loc = tm * TOP_K

    def issue(step, slot):
        for s in range(n_sub):
            tile = step * n_sub + s
            for e in range(N_EXPERTS):
                c = rc_ref[tile * N_EXPERTS + e] * n_seg

                @pl.when(c > 0)
                def _():
                    src = pl.multiple_of(rg_ref[tile * N_EXPERTS + e] * n_seg, n_seg)
                    dst = pl.multiple_of((s * n_loc + rl_ref[tile * N_EXPERTS + e]) * n_seg, n_seg)
                    pltpu.make_async_copy(y_ref.at[pl.ds(src, c)], yloc_ref.at[slot, pl.ds(dst, c)],
                                          sem.at[slot]).start()

    @pl.when(i == 0)
    def _():
        issue(0, 0)

    @pl.when(i + 1 < pl.num_programs(0))
    def _():
        issue(i + 1, (i + 1) % 2)

    slot = i % 2
    pltpu.make_async_copy(y_ref.at[pl.ds(0, n_sub * n_loc * n_seg)], yloc_ref.at[slot],
                          sem.at[slot]).wait()

    q = lax.broadcasted_iota(I32, (tm, n_loc), 1)
    subs = [dict(tok=slice(s * tm, (s + 1) * tm)) for s in range(n_sub)]
    for s, sb in enumerate(subs):
        sb["y"] = jnp.concatenate(
            [yloc_ref[slot, pl.ds(s * n_loc * n_seg + j, n_loc, stride=n_seg), :]
             for j in range(n_seg)], axis=1).astype(BF16)
        route = route_ref[sb["tok"], :]
        lpos = route.astype(I32)
        gmat = jnp.zeros((tm, n_loc), F32)
        for kq in range(TOP_K):
            gmat = jnp.where(q == lpos[:, TOP_K + kq:TOP_K + kq + 1], route[:, kq:kq + 1], gmat)
        sb["ghi"], sb["glo"] = _split2(gmat)
    for sb in subs:
        sb["moe"] = (jnp.dot(sb["ghi"], sb["y"], preferred_element_type=F32)
                     + jnp.dot(sb["glo"], sb["y"], preferred_element_type=F32))
    for sb in subs:
        h = h_ref[sb["tok"], :] + gt_ref[0, 0] * sb["moe"]
        o_ref[sb["tok"], :] = _rms(h) * fg_ref[...]


def _combine(y_rows, run_start, run_local, run_count, route, h, mod6, final_g, tiles_per_batch):
    n_tok = h.shape[0]
    tm = COMBINE_TILE * COMBINE_TILES_PER_STEP
    grid_spec = pltpu.PrefetchScalarGridSpec(
        num_scalar_prefetch=3,
        grid=(n_tok // tm,),
        in_specs=[pl.BlockSpec(memory_space=pl.ANY),
                  pl.BlockSpec((tm, LANES), lambda i, *_: (i, 0)),
                  pl.BlockSpec((tm, D_MODEL), lambda i, *_: (i, 0)),
                  pl.BlockSpec((1, 1, 1, D_MODEL), lambda i, *_: (i // tiles_per_batch, 5, 0, 0)),
                  pl.BlockSpec((1, D_MODEL), lambda i, *_: (0, 0))],
        out_specs=pl.BlockSpec((tm, D_MODEL), lambda i, *_: (i, 0)),
        scratch_shapes=[pltpu.VMEM((2, tm * TOP_K * (D_MODEL // LANES), LANES), F32),
                        pltpu.SemaphoreType.DMA((2,))],
    )
    return pl.pallas_call(
        _combine_body,
        name="combine_rows",
        grid_spec=grid_spec,
        out_shape=jax.ShapeDtypeStruct((n_tok, D_MODEL), F32),
        compiler_params=_cparams("arbitrary"),
    )(run_start, run_local, run_count, y_rows, route, h, mod6, final_g)


def kernel(x, c, ada_w, ada_b, norm1_g, w_in, conv_w, conv_gn, rwkv_mu, rwkv_w0, rwkv_w_up,
           rwkv_a0, rwkv_a_up, rwkv_g_up, rwkv_k_k, rwkv_k_a, rwkv_r_k, rwkv_ln_w, rwkv_ln_b,
           w_out, norm2_g, router_w, router_b, exp_w1, exp_b1, exp_w2, exp_b2, final_g):
    bsz, seq, _ = x.shape
    depth = ada_w.shape[0]
    assert depth == 1, "the final norm is fused into the last layer's combine step"
    n_tok = bsz * seq
    assert seq % TOKEN_TILE == 0 and seq % CHUNK == 0 and seq % COMBINE_TILE == 0
    n_rows = n_tok * TOP_K
    assert n_rows % ROW_BLOCK == 0
    n_blocks = n_rows // ROW_BLOCK + N_EXPERTS
    m_pad = n_blocks * ROW_BLOCK

    h = x.reshape(n_tok, D_MODEL)
    out = h
    for l in range(depth):
        mod6 = _ada_mod(c, ada_w[l], ada_b[l]).reshape(bsz, 6, 1, D_MODEL)
        yc, pr = _in_proj_conv(h.reshape(bsz, seq, D_MODEL), norm1_g[l].reshape(1, -1), mod6,
                               w_in[l].astype(BF16), conv_w[l], conv_gn[l].reshape(1, -1))
        yr = _rwkv_mixer(pr, rwkv_mu[l], rwkv_w0[l], rwkv_w_up[l], rwkv_a0[l], rwkv_a_up[l],
                         rwkv_g_up[l], rwkv_k_k[l], rwkv_k_a[l], rwkv_r_k[l], rwkv_ln_w[l],
                         rwkv_ln_b[l], bsz, seq)
        rw = router_w[l].T
        rw_hi = rw.astype(BF16)
        rw_lo = (rw - rw_hi.astype(F32)).astype(BF16)
        rb = jnp.broadcast_to(router_b[l][:, None], (N_EXPERTS, LANES))
        h, route, tab, x_local = _out_proj_router(
            yc.reshape(n_tok, D_CONV), yr.reshape(n_tok, D_RWKV), h, w_out[l].astype(BF16), mod6,
            norm2_g[l].reshape(1, -1), rw_hi, rw_lo, rb, seq // TOKEN_TILE)
        run_count = tab[:, :, 0].astype(I32)
        run_local = tab[:, :, 1].astype(I32)
        counts = jnp.sum(run_count, axis=0)
        padded = (counts + ROW_BLOCK - 1) // ROW_BLOCK * ROW_BLOCK
        pad_end = jnp.cumsum(padded)
        pad_start = pad_end - padded
        run_start = pad_start[None, :] + jnp.cumsum(run_count, axis=0) - run_count
        n_used = (pad_end[-1] // ROW_BLOCK).astype(I32).reshape(1)
        blk_row = jnp.minimum(jnp.arange(n_blocks, dtype=I32), n_used[0] - 1) * ROW_BLOCK
        block_expert = jnp.minimum(
            jnp.sum((pad_end[None, :] <= blk_row[:, None]).astype(I32), axis=1), N_EXPERTS - 1)
        blk_start = run_start.T[block_expert]
        blk_end = blk_start + run_count.T[block_expert]
        first_tile = jnp.sum((blk_end <= blk_row[:, None]).astype(I32), axis=1)
        last_tile = jnp.sum((blk_start < blk_row[:, None] + ROW_BLOCK).astype(I32), axis=1) - 1
        n_valid = jnp.clip((pad_start + counts)[block_expert] - blk_row, 0, ROW_BLOCK).astype(I32)
        run_tables = (run_start.reshape(-1), run_local.reshape(-1), run_count.reshape(-1))
        y_rows = _experts(x_local, (block_expert, n_used, first_tile, last_tile, n_valid), run_tables,
                          _deinterleave_w1(exp_w1[l]), exp_w2[l],
                          exp_b1[l][:, None, 0::2], exp_b1[l][:, None, 1::2], exp_b2[l][:, None, :],
                          m_pad)
        out = _combine(y_rows, *run_tables, route, h, mod6, final_g.reshape(1, -1),
                       seq // (COMBINE_TILE * COMBINE_TILES_PER_STEP))
    return out.reshape(bsz, seq, D_MODEL)
```

```python
import jax
import jax.numpy as jnp
from jax import lax
from jax.experimental import pallas as pl
from jax.experimental.pallas import tpu as pltpu

F32 = jnp.float32
BF16 = jnp.bfloat16
I32 = jnp.int32

D_MODEL = 1024
D_CONV = 512
CONV_GROUP_DIM = 64
D_RWKV = 512
HEAD_DIM = 64
N_HEADS = D_RWKV // HEAD_DIM
DECAY_LORA = 64
AAA_LORA = 64
GATE_LORA = 128
COLS_CONV = 3 * D_CONV
COLS_RWKV = 3 * D_RWKV + DECAY_LORA + AAA_LORA + GATE_LORA
N_EXPERTS = 32
TOP_K = 4
D_EXPERT = D_MODEL
SWIGLU_LIMIT = 7.0
SWIGLU_ALPHA = 1.702
NORM_EPS = 1e-5
GN_EPS = 64e-5

LANES = 128
SUBLANES = 8
MXU_TILE = 256
CHUNK = 64
RWKV_SEQS_PER_STEP = 8
RWKV_HEADS_PER_GROUP = 4
RWKV_WAVES = 8
RWKV_STAGE_LAG = 1
ROW_BLOCK = 512
TOKEN_TILE = 512
COMBINE_TILE = 256
COMBINE_TILES_PER_STEP = 2
VMEM_LIMIT = 48 * 1024 * 1024


def _cparams(*sem):
    return pltpu.CompilerParams(dimension_semantics=sem, vmem_limit_bytes=VMEM_LIMIT)


def _mm(a, b):
    return jnp.dot(a.astype(BF16), b.astype(BF16), preferred_element_type=F32)


def _mm_nt(a, b):
    return lax.dot_general(a.astype(BF16), b.astype(BF16), (((1,), (1,)), ((), ())),
                           preferred_element_type=F32)


def _mm_tn(a, b):
    return lax.dot_general(a.astype(BF16), b.astype(BF16), (((0,), (0,)), ((), ())),
                           preferred_element_type=F32)


def _split2(x):
    hi = x.astype(BF16)
    lo = (x - hi.astype(F32)).astype(BF16)
    return hi, lo


def _split3(x):
    hi = x.astype(BF16)
    r = x - hi.astype(F32)
    mid = r.astype(BF16)
    lo = (r - mid.astype(F32)).astype(BF16)
    return hi, mid, lo


def _sigmoid(x):
    return 1.0 / (1.0 + jnp.exp(-x))


def _rms(x):
    return x * lax.rsqrt(jnp.mean(x * x, axis=-1, keepdims=True) + NORM_EPS)


def _ada_body(c_ref, w_ref, b_ref, o_ref):
    c = c_ref[...]
    ca = c * _sigmoid(c)
    ah, al = _split2(ca)
    wh, wl = _split2(w_ref[...])
    acc = jnp.dot(ah, wh, preferred_element_type=F32)
    acc += jnp.dot(ah, wl, preferred_element_type=F32)
    acc += jnp.dot(al, wh, preferred_element_type=F32)
    o_ref[...] = acc + b_ref[...]


def _ada_mod(c, ada_w, ada_b):
    bsz = c.shape[0]
    n = ada_w.shape[1]
    tn = 1024
    return pl.pallas_call(
        _ada_body,
        name="ada_mod",
        grid=(n // tn,),
        in_specs=[pl.BlockSpec((bsz, D_MODEL), lambda j: (0, 0)),
                  pl.BlockSpec((D_MODEL, tn), lambda j: (0, j)),
                  pl.BlockSpec((1, tn), lambda j: (0, j))],
        out_specs=pl.BlockSpec((bsz, tn), lambda j: (0, j)),
        out_shape=jax.ShapeDtypeStruct((bsz, n), F32),
        compiler_params=_cparams("parallel"),
    )(c, ada_w, ada_b.reshape(1, n))


def _group_sums(x, group):
    shift = group.bit_length() - 1
    li = jnp.right_shift(lax.broadcasted_iota(I32, (MXU_TILE, MXU_TILE), 0), shift)
    lj = jnp.right_shift(lax.broadcasted_iota(I32, (MXU_TILE, MXU_TILE), 1), shift)
    same = jnp.where(li == lj, 1.0, 0.0).astype(BF16)
    hi, lo = _split2(x)
    cols = []
    for t in range(x.shape[1] // MXU_TILE):
        ts = slice(t * MXU_TILE, (t + 1) * MXU_TILE)
        cols.append(jnp.dot(hi[:, ts], same, preferred_element_type=F32)
                    + jnp.dot(lo[:, ts], same, preferred_element_type=F32))
    return jnp.concatenate(cols, axis=1)


def _inproj_body(x_ref, g_ref, sh_ref, sc_ref, w_ref, cw_ref, gn_ref, yc_ref, or_ref, carry_ref):
    @pl.when(pl.program_id(1) == 0)
    def _():
        carry_ref[...] = jnp.zeros_like(carry_ref)

    ub = _rms(x_ref[0]) * g_ref[...]
    ub = (ub * (1.0 + sc_ref[0, 0]) + sh_ref[0, 0]).astype(BF16)
    p = jnp.dot(ub, w_ref[:, :COLS_CONV], preferred_element_type=F32)
    or_ref[0] = jnp.dot(ub, w_ref[:, COLS_CONV:], preferred_element_type=F32)

    tc = p.shape[0]
    b_gate = p[:, :D_CONV]
    u = p[:, D_CONV:2 * D_CONV] * p[:, 2 * D_CONV:]
    ext = jnp.concatenate([carry_ref[...], u], axis=0)
    u1 = pltpu.roll(ext, 1, axis=0)[SUBLANES:]
    u2 = pltpu.roll(ext, 2, axis=0)[SUBLANES:]
    carry_ref[...] = u[tc - SUBLANES:, :]
    cw = cw_ref[...]
    y = b_gate * (cw[0:1] * u2 + cw[1:2] * u1 + cw[2:3] * u)
    ms = _group_sums(y * y, CONV_GROUP_DIM) * (1.0 / CONV_GROUP_DIM)
    yc_ref[0] = y * lax.rsqrt(ms + NORM_EPS) * gn_ref[...]


def _in_proj_conv(x, g1, mod6, w_in_bf, conv_w, conv_gn):
    bsz, seq, _ = x.shape
    tm = TOKEN_TILE
    n_in = w_in_bf.shape[1]
    const = lambda m, n: pl.BlockSpec((m, n), lambda b, i: (0, 0))
    return pl.pallas_call(
        _inproj_body,
        name="in_proj_conv",
        grid=(bsz, seq // tm),
        in_specs=[pl.BlockSpec((1, tm, D_MODEL), lambda b, i: (b, i, 0)),
                  const(1, D_MODEL),
                  pl.BlockSpec((1, 1, 1, D_MODEL), lambda b, i: (b, 0, 0, 0)),
                  pl.BlockSpec((1, 1, 1, D_MODEL), lambda b, i: (b, 1, 0, 0)),
                  const(D_MODEL, n_in), const(3, D_CONV), const(1, D_CONV)],
        out_specs=[pl.BlockSpec((1, tm, D_CONV), lambda b, i: (b, i, 0)),
                   pl.BlockSpec((1, tm, COLS_RWKV), lambda b, i: (b, i, 0))],
        out_shape=[jax.ShapeDtypeStruct((bsz, seq, D_CONV), F32),
                   jax.ShapeDtypeStruct((bsz, seq, COLS_RWKV), F32)],
        scratch_shapes=[pltpu.VMEM((SUBLANES, D_CONV), F32)],
        compiler_params=_cparams("parallel", "arbitrary"),
    )(x, g1, mod6, mod6, w_in_bf, conv_w, conv_gn)


def _rwkv_body(p_ref, mu_ref, w0_ref, wup_ref, a0_ref, aup_ref, gup_ref, kkw_ref, kaw_ref,
               rkw_ref, lnw_ref, lnb_ref, o_ref, state_ref, prev_ref):
    @pl.when(pl.program_id(1) == 0)
    def _():
        state_ref[...] = jnp.zeros_like(state_ref)
        prev_ref[...] = jnp.zeros_like(prev_ref)

    C = CHUNK
    N = HEAD_DIM
    G = RWKV_HEADS_PER_GROUP
    GW = G * N
    nb = p_ref.shape[0]
    s1, s2, s3 = D_RWKV, 2 * D_RWKV, 3 * D_RWKV
    row1 = lax.broadcasted_iota(I32, (C, 1), 0)
    ri = lax.broadcasted_iota(I32, (C, C), 0)
    ci = lax.broadcasted_iota(I32, (C, C), 1)
    tri = jnp.where(ri >= ci, 1.0, 0.0).astype(BF16)
    rg = lax.broadcasted_iota(I32, (C, GW), 0)
    cg = jnp.bitwise_and(lax.broadcasted_iota(I32, (C, GW), 1), N - 1)
    strict_g = rg > cg
    incl_g = rg >= cg
    eye_g = jnp.where(rg == cg, 1.0, 0.0).astype(F32)
    br = jnp.right_shift(lax.broadcasted_iota(I32, (GW, GW), 0), 6)
    bc = jnp.right_shift(lax.broadcasted_iota(I32, (GW, GW), 1), 6)
    same_head = jnp.where(br == bc, 1.0, 0.0)
    same_head_b = same_head.astype(BF16)

    def blockdiag(x):
        return jnp.concatenate([x.astype(BF16)] * G, axis=0) * same_head_b

    def stages(bb):
        units = []
        P = p_ref[bb]
        Pprev = jnp.where(row1 == 0, prev_ref[bb, 0:1, :], pltpu.roll(P, 1, axis=0))
        prev_ref[bb] = jnp.broadcast_to(P[C - 1:C, :], prev_ref.shape[1:])
        p = P + (Pprev - P) * mu_ref[...]
        r = p[:, :s1]
        k = p[:, s1:s2]
        v = p[:, s2:s3]
        dw = p[:, s3:s3 + DECAY_LORA]
        da = p[:, s3 + DECAY_LORA:s3 + DECAY_LORA + AAA_LORA]
        dg = p[:, s3 + DECAY_LORA + AAA_LORA:]
        z = -(w0_ref[...] + _mm(jnp.tanh(dw), wup_ref[...]))
        softplus = jnp.maximum(z, 0.0) + jnp.log(1.0 + jnp.exp(-jnp.abs(z)))
        logw = -jnp.exp(-softplus - 0.5)
        a = _sigmoid(a0_ref[...] + _mm(da, aup_ref[...]))
        g = _mm(_sigmoid(dg), gup_ref[...])
        kk = k * kkw_ref[...]
        kmod = k * (1.0 + (a - 1.0) * kaw_ref[...])
        cum = sum(jnp.dot(tri, part, preferred_element_type=F32) for part in _split3(logw))
        gam = jnp.exp(cum)
        gam_prev = jnp.exp(cum - logw)
        ginv = jnp.exp(-cum)
        def head_sum(x):
            return _group_sums(x, N)

        kkn = kk * lax.rsqrt(jnp.maximum(head_sum(kk * kk), 1e-24))
        bonus_v = head_sum(r * kmod * rkw_ref[...]) * v
        at_all = (-kkn * gam_prev).astype(BF16)
        rt_all = (r * gam).astype(BF16)
        bt_all = (kkn * a * ginv).astype(BF16)
        kt_all = (kmod * ginv).astype(BF16)
        vb_all = v.astype(BF16)
        for gi in range(N_HEADS // G):
            gs = slice(gi * GW, (gi + 1) * GW)
            units.append(dict(
                slot=bb * (N_HEADS // G) + gi, At=at_all[:, gs], Rt=rt_all[:, gs], V=vb_all[:, gs],
                AR=jnp.concatenate([at_all[:, gs], rt_all[:, gs]], axis=0),
                BK=jnp.concatenate([bt_all[:, gs], kt_all[:, gs]], axis=0),
                Bbd=blockdiag(bt_all[:, gs]), Kbd=blockdiag(kt_all[:, gs]),
                glast=gam[C - 1:C, gs]))
        yield
        for u in units:
            mb = _mm_nt(u["AR"], u["Bbd"])
            mk = _mm_nt(u["AR"], u["Kbd"])
            u["Lp"] = jnp.where(strict_g, mb[:C], 0.0)
            u["T"] = eye_g + u["Lp"]
            u["Trb"] = jnp.where(incl_g, mb[C:], 0.0).astype(BF16)
            u["Lak"] = jnp.where(strict_g, mk[:C], 0.0).astype(BF16)
            u["Trk"] = jnp.where(incl_g, mk[C:], 0.0).astype(BF16)
        yield
        for u in units:
            u["Vbd"] = blockdiag(u["V"])
            u["X2"] = _mm(u["Lak"], u["Vbd"])
            u["Lp"] = _mm(u["Lp"], blockdiag(u["Lp"]))
        yield
        for _ in range(4):
            for u in units:
                bd = blockdiag(u["Lp"])
                u["T"] = u["T"] + _mm(u["T"], bd)
                u["Lp"] = _mm(u["Lp"], bd)
            yield
        for u in units:
            u["T"] = u["T"] + _mm(u["T"], blockdiag(u["Lp"]))
        yield
        for u in units:
            tb = u["T"].astype(BF16)
            u["W1"] = _mm(tb, blockdiag(u["At"]))
            u["W2"] = _mm(tb, blockdiag(u["X2"]))
        yield
        for u in units:
            u["S0"] = state_ref[u["slot"]]
            u["Sbd"] = blockdiag(u["S0"])
            u["U"] = _mm_nt(u["W1"], u["Sbd"]) + u["W2"]
        yield
        for u in units:
            u["O"] = (_mm_nt(u["Rt"], u["Sbd"]) + _mm(u["Trb"], blockdiag(u["U"]))
                      + _mm(u["Trk"], u["Vbd"]))
            uv = jnp.concatenate([u["U"].astype(BF16), u["V"]], axis=0)
            cross = _mm_tn(uv, u["BK"]) * same_head
            s_add = cross[0:N]
            for hh in range(1, G):
                s_add = s_add + cross[hh * N:(hh + 1) * N]
            state_ref[u["slot"]] = (u["S0"] + s_add) * u["glast"]
        yield
        o_all = jnp.concatenate([u["O"] for u in units], axis=-1)
        cen = o_all - head_sum(o_all) * (1.0 / N)
        var = head_sum(cen * cen) * (1.0 / N)
        on = cen * lax.rsqrt(var + GN_EPS) * lnw_ref[...] + lnb_ref[...]
        o_ref[bb] = (on + bonus_v) * g
        yield

    waves = min(RWKV_WAVES, nb)
    live = [(RWKV_STAGE_LAG * (bb * waves // nb), stages(bb)) for bb in range(nb)]
    tick = 0
    while live:
        for item in list(live):
            if tick >= item[0] and next(item[1], "done") == "done":
                live.remove(item)
        tick += 1


def _rwkv_mixer(pr, mu, w0, w_up, a0, a_up, g_up, k_k, k_a, r_k, ln_w, ln_b, bsz, seq):
    C = CHUNK
    nb = RWKV_SEQS_PER_STEP if bsz % RWKV_SEQS_PER_STEP == 0 else 1
    row = lambda n: pl.BlockSpec((1, n), lambda b, i: (0, 0))
    full = lambda m, n: pl.BlockSpec((m, n), lambda b, i: (0, 0))
    return pl.pallas_call(
        _rwkv_body,
        name="rwkv_mixer",
        grid=(bsz // nb, seq // C),
        in_specs=[pl.BlockSpec((nb, C, COLS_RWKV), lambda b, i: (b, i, 0)),
                  row(COLS_RWKV), row(D_RWKV), full(DECAY_LORA, D_RWKV), row(D_RWKV),
                  full(AAA_LORA, D_RWKV), full(GATE_LORA, D_RWKV), row(D_RWKV), row(D_RWKV),
                  row(D_RWKV), row(D_RWKV), row(D_RWKV)],
        out_specs=pl.BlockSpec((nb, C, D_RWKV), lambda b, i: (b, i, 0)),
        out_shape=jax.ShapeDtypeStruct((bsz, seq, D_RWKV), F32),
        scratch_shapes=[pltpu.VMEM((nb * N_HEADS // RWKV_HEADS_PER_GROUP, HEAD_DIM,
                                    RWKV_HEADS_PER_GROUP * HEAD_DIM), F32),
                        pltpu.VMEM((nb, SUBLANES, COLS_RWKV), F32)],
        compiler_params=_cparams("parallel", "arbitrary"),
    )(pr.reshape(bsz, seq, COLS_RWKV), mu.reshape(1, -1), w0.reshape(1, -1), w_up,
      a0.reshape(1, -1), a_up, g_up, k_k.reshape(1, -1), k_a.reshape(1, -1), r_k.reshape(1, -1),
      ln_w.reshape(1, -1), ln_b.reshape(1, -1))


def _outproj_body(yc_ref, yr_ref, x_ref, wo_ref, gt_ref, g2_ref, sh_ref, sc_ref, rwh_ref, rwl_ref,
                  rb_ref, h_ref, route_ref, tab_ref, xl_ref):
    mix = _mm(yc_ref[...], wo_ref[:D_CONV, :]) + _mm(yr_ref[...], wo_ref[D_CONV:, :])
    h = x_ref[...] + gt_ref[0, 0] * mix
    h_ref[...] = h
    u2 = _rms(h) * g2_ref[...]
    u2 = u2 * (1.0 + sc_ref[0, 0]) + sh_ref[0, 0]

    uh, ul = _split2(u2)
    tm = u2.shape[0]
    logits = (_mm_nt(rwh_ref[...], uh) + _mm_nt(rwl_ref[...], uh) + _mm_nt(rwh_ref[...], ul)
              + jnp.concatenate([rb_ref[...]] * (tm // LANES), axis=1))
    eid = lax.broadcasted_iota(I32, (N_EXPERTS, tm), 0).astype(F32)
    neg = jnp.float32(-jnp.inf)
    l = logits
    tops, idxs = [], []
    for _ in range(TOP_K):
        m = jnp.max(l, axis=0, keepdims=True)
        ix = jnp.min(jnp.where(l == m, eid, float(N_EXPERTS)), axis=0, keepdims=True)
        tops.append(m)
        idxs.append(ix)
        l = jnp.where(eid == ix, neg, l)
    es = [jnp.exp(t - tops[0]) for t in tops]
    den = es[0] + es[1] + es[2] + es[3]
    gates = [e / den for e in es]
    lpos_rows = _sort_tiles(eid, idxs, uh, tab_ref, xl_ref)
    row8 = lax.broadcasted_iota(I32, (SUBLANES, tm), 0)
    packed = jnp.zeros((SUBLANES, tm), F32)
    for kq, val in enumerate(gates + lpos_rows):
        packed = jnp.where(row8 == kq, val, packed)
    route_ref[...] = jnp.transpose(
        jnp.concatenate([packed, jnp.zeros((LANES - SUBLANES, tm), F32)], axis=0))


def _out_proj_router(yc, yr, xf, w_out_bf, mod6, g2, rw_hi, rw_lo, rb, tiles_per_batch):
    n_tok = xf.shape[0]
    tm = TOKEN_TILE
    n_sub = tm // COMBINE_TILE
    n_seg = D_MODEL // LANES
    tok = lambda n: pl.BlockSpec((tm, n), lambda i: (i, 0))
    modspec = lambda which: pl.BlockSpec((1, 1, 1, D_MODEL),
                                         lambda i: (i // tiles_per_batch, which, 0, 0))
    const = lambda m, n: pl.BlockSpec((m, n), lambda i: (0, 0))
    return pl.pallas_call(
        _outproj_body,
        name="out_proj_router",
        grid=(n_tok // tm,),
        in_specs=[tok(D_CONV), tok(D_RWKV), tok(D_MODEL), const(D_MODEL, D_MODEL),
                  modspec(2), const(1, D_MODEL), modspec(3), modspec(4),
                  const(N_EXPERTS, D_MODEL), const(N_EXPERTS, D_MODEL), const(N_EXPERTS, LANES)],
        out_specs=[tok(D_MODEL), tok(LANES),
                   pl.BlockSpec((n_sub, N_EXPERTS, LANES), lambda i: (i, 0, 0)),
                   pl.BlockSpec((tm * TOP_K * n_seg, LANES), lambda i: (i, 0))],
        out_shape=[jax.ShapeDtypeStruct((n_tok, D_MODEL), F32),
                   jax.ShapeDtypeStruct((n_tok, LANES), F32),
                   jax.ShapeDtypeStruct((n_tok // COMBINE_TILE, N_EXPERTS, LANES), F32),
                   jax.ShapeDtypeStruct((n_tok * TOP_K * n_seg, LANES), F32)],
        compiler_params=_cparams("parallel"),
    )(yc, yr, xf, w_out_bf, mod6, g2, mod6, mod6, rw_hi, rw_lo, rb)


def _sort_tiles(eid, idxs, u_all, tab_ref, xl_ref):
    tm = COMBINE_TILE
    n_sub = u_all.shape[0] // tm
    n_loc = tm * TOP_K
    n_seg = D_MODEL // LANES
    rr = lax.broadcasted_iota(I32, (tm, tm), 0)
    cc = lax.broadcasted_iota(I32, (tm, tm), 1)
    before = jnp.where(rr < cc, 1.0, 0.0).astype(BF16)
    ones_col = jnp.ones((tm, LANES), BF16)
    er = lax.broadcasted_iota(I32, (N_EXPERTS, N_EXPERTS), 0)
    ec = lax.broadcasted_iota(I32, (N_EXPERTS, N_EXPERTS), 1)
    lower = jnp.where(er > ec, 1.0, 0.0).astype(BF16)
    lane = lax.broadcasted_iota(I32, (N_EXPERTS, LANES), 1)
    q = lax.broadcasted_iota(I32, (n_loc, tm), 0)
    subs = [dict(tok=slice(s * tm, (s + 1) * tm)) for s in range(n_sub)]
    chosen = [jnp.broadcast_to(ix, eid.shape) for ix in idxs]
    eid_tile = lax.broadcasted_iota(I32, (N_EXPERTS, tm), 0).astype(F32)

    for sb in subs:
        sb["masks"] = [eid_tile == ch[:, sb["tok"]] for ch in chosen]
        sel = jnp.zeros((N_EXPERTS, tm), F32)
        for mk in sb["masks"]:
            sel = sel + jnp.where(mk, 1.0, 0.0)
        sb["sel"] = sel.astype(BF16)
    for sb in subs:
        sb["prefix"] = jnp.dot(sb["sel"], before, preferred_element_type=F32)
        sb["cnt"] = jnp.dot(sb["sel"], ones_col, preferred_element_type=F32)
    for sb in subs:
        sb["loff"] = jnp.dot(lower, sb["cnt"].astype(BF16), preferred_element_type=F32)
    for s, sb in enumerate(subs):
        tab_ref[s] = jnp.where(lane == 0, sb["cnt"], jnp.where(lane == 1, sb["loff"], 0.0))
        local_row = jnp.concatenate([sb["loff"]] * (tm // LANES), axis=1) + sb["prefix"]
        sb["lpos"] = [jnp.sum(jnp.where(mk, local_row, 0.0), axis=0, keepdims=True)
                      for mk in sb["masks"]]
    for sb in subs:
        pick = jnp.zeros((n_loc, tm), F32)
        for lp in sb["lpos"]:
            pick = jnp.where(q == lp.astype(I32), 1.0, pick)
        sb["xl"] = jnp.dot(pick.astype(BF16), u_all[sb["tok"], :], preferred_element_type=F32)
    for s, sb in enumerate(subs):
        for j in range(n_seg):
            xl_ref[pl.ds(s * n_loc * n_seg + j, n_loc, stride=n_seg), :] = (
                sb["xl"][:, j * LANES:(j + 1) * LANES])
    return [jnp.concatenate([jnp.broadcast_to(sb["lpos"][kq], (SUBLANES, tm)) for sb in subs], axis=1)
            for kq in range(TOP_K)]


def _deinterleave_body(w_ref, o_ref):
    w = w_ref[0].astype(BF16)
    half = o_ref.shape[2] // 2
    src = lax.broadcasted_iota(I32, (2 * MXU_TILE, MXU_TILE), 0)
    dst = lax.broadcasted_iota(I32, (2 * MXU_TILE, MXU_TILE), 1)
    pick_even = jnp.where(src == 2 * dst, 1.0, 0.0).astype(BF16)
    pick_odd = jnp.where(src == 2 * dst + 1, 1.0, 0.0).astype(BF16)
    for j in range(half // MXU_TILE):
        slab = w[:, 2 * MXU_TILE * j:2 * MXU_TILE * (j + 1)]
        lo = MXU_TILE * j
        o_ref[0, :, lo:lo + MXU_TILE] = jnp.dot(
            slab, pick_even, preferred_element_type=F32).astype(BF16)
        o_ref[0, :, half + lo:half + lo + MXU_TILE] = jnp.dot(
            slab, pick_odd, preferred_element_type=F32).astype(BF16)


def _deinterleave_w1(w1):
    n_e, d_in, two_f = w1.shape
    tr = 512
    return pl.pallas_call(
        _deinterleave_body,
        name="deinterleave_w1",
        grid=(n_e, d_in // tr),
        in_specs=[pl.BlockSpec((1, tr, two_f), lambda e, r: (e, r, 0))],
        out_specs=pl.BlockSpec((1, tr, two_f), lambda e, r: (e, r, 0)),
        out_shape=jax.ShapeDtypeStruct((n_e, d_in, two_f), BF16),
        compiler_params=_cparams("parallel", "parallel"),
    )(w1)


def _expert_body(be_ref, nu_ref, ft_ref, lt_ref, nv_ref, rg_ref, rl_ref, rc_ref,
                 xl_ref, w1_ref, w2_ref, bg_ref, bl_ref, b2_ref, y_ref, xbuf_ref, sem):
    i = pl.program_id(0)
    n_used = nu_ref[0]
    bm = ROW_BLOCK
    n_seg = D_MODEL // LANES
    n_loc = COMBINE_TILE * TOP_K

    def issue(blk, slot):
        e = be_ref[blk]
        r0 = blk * bm

        def piece(t, carry):
            g = rg_ref[t * N_EXPERTS + e]
            s = jnp.maximum(g, r0)
            n = jnp.minimum(g + rc_ref[t * N_EXPERTS + e], r0 + bm) - s

            @pl.when(n > 0)
            def _():
                src = pl.multiple_of((t * n_loc + rl_ref[t * N_EXPERTS + e] + s - g) * n_seg, n_seg)
                dst = pl.multiple_of((s - r0) * n_seg, n_seg)
                pltpu.make_async_copy(xl_ref.at[pl.ds(src, n * n_seg)],
                                      xbuf_ref.at[slot, pl.ds(dst, n * n_seg)], sem.at[slot]).start()
            return carry

        lax.fori_loop(ft_ref[blk], lt_ref[blk] + 1, piece, 0)

    @pl.when(i == 0)
    def _():
        xbuf_ref[...] = jnp.zeros_like(xbuf_ref)
        issue(0, 0)

    @pl.when(i + 1 < n_used)
    def _():
        issue(i + 1, (i + 1) % 2)

    @pl.when(i >= n_used)
    def _():
        y_ref[...] = jnp.zeros_like(y_ref)

    @pl.when(i < n_used)
    def _():
        slot = i % 2
        n_valid = nv_ref[i] * n_seg
        pltpu.make_async_copy(xl_ref.at[pl.ds(0, n_valid)], xbuf_ref.at[slot, pl.ds(0, n_valid)],
                              sem.at[slot]).wait()
        x = jnp.concatenate([xbuf_ref[slot, pl.ds(j, bm, stride=n_seg), :] for j in range(n_seg)],
                            axis=1).astype(BF16)
        hb = jnp.dot(x, w1_ref[0], preferred_element_type=F32)
        hg = hb[:, :D_EXPERT] + bg_ref[0]
        hl = hb[:, D_EXPERT:] + bl_ref[0]
        xg = jnp.minimum(hg, SWIGLU_LIMIT)
        xl = jnp.clip(hl, -SWIGLU_LIMIT, SWIGLU_LIMIT)
        act = xg * _sigmoid(SWIGLU_ALPHA * xg) * (xl + 1.0)
        y = jnp.dot(act.astype(BF16), w2_ref[0].astype(BF16),
                    preferred_element_type=F32) + b2_ref[0]
        for j in range(n_seg):
            y_ref[pl.ds(j, bm, stride=n_seg), :] = y[:, j * LANES:(j + 1) * LANES]


def _experts(x_local, block_tables, run_tables, w1d, w2, b1g, b1l, b2, m_pad):
    bm = ROW_BLOCK
    n_seg = D_MODEL // LANES
    wsel = lambda i, be, *_: (be[i], 0, 0)
    grid_spec = pltpu.PrefetchScalarGridSpec(
        num_scalar_prefetch=8,
        grid=(m_pad // bm,),
        in_specs=[pl.BlockSpec(memory_space=pl.ANY),
                  pl.BlockSpec((1, D_MODEL, 2 * D_EXPERT), wsel),
                  pl.BlockSpec((1, D_EXPERT, D_MODEL), wsel),
                  pl.BlockSpec((1, 1, D_EXPERT), wsel),
                  pl.BlockSpec((1, 1, D_EXPERT), wsel),
                  pl.BlockSpec((1, 1, D_MODEL), wsel)],
        out_specs=pl.BlockSpec((bm * n_seg, LANES), lambda i, *_: (i, 0)),
        scratch_shapes=[pltpu.VMEM((2, bm * n_seg, LANES), F32), pltpu.SemaphoreType.DMA((2,))],
    )
    return pl.pallas_call(
        _expert_body,
        name="expert_mlp",
        grid_spec=grid_spec,
        out_shape=jax.ShapeDtypeStruct((m_pad * n_seg, LANES), F32),
        compiler_params=_cparams("arbitrary"),
    )(*block_tables, *run_tables, x_local, w1d, w2, b1g, b1l, b2)


def _combine_body(rg_ref, rl_ref, rc_ref, y_ref, route_ref, h_ref, gt_ref, fg_ref, o_ref,
                  yloc_ref, sem):
    i = pl.program_id(0)
    tm = COMBINE_TILE
    n_sub = h_ref.shape[0] // tm
    n_seg = D_MODEL // LANES
    n_loc = tm * TOP_K

    def issue(step, slot):
        for s in range(n_sub):
            tile = step * n_sub + s
            for e in range(N_EXPERTS):
                c = rc_ref[tile * N_EXPERTS + e] * n_seg

                @pl.when(c > 0)
                def _():
                    src = pl.multiple_of(rg_ref[tile * N_EXPERTS + e] * n_seg, n_seg)
                    dst = pl.multiple_of((s * n_loc + rl_ref[tile * N_EXPERTS + e]) * n_seg, n_seg)
                    pltpu.make_async_copy(y_ref.at[pl.ds(src, c)], yloc_ref.at[slot, pl.ds(dst, c)],
                                          sem.at[slot]).start()

    @pl.when(i == 0)
    def _():
        issue(0, 0)

    @pl.when(i + 1 < pl.num_programs(0))
    def _():
        issue(i + 1, (i + 1) % 2)

    slot = i % 2
    pltpu.make_async_copy(y_ref.at[pl.ds(0, n_sub * n_loc * n_seg)], yloc_ref.at[slot],
                          sem.at[slot]).wait()

    q = lax.broadcasted_iota(I32, (tm, n_loc), 1)
    subs = [dict(tok=slice(s * tm, (s + 1) * tm)) for s in range(n_sub)]
    for s, sb in enumerate(subs):
        sb["y"] = jnp.concatenate(
            [yloc_ref[slot, pl.ds(s * n_loc * n_seg + j, n_loc, stride=n_seg), :]
             for j in range(n_seg)], axis=1).astype(BF16)
        route = route_ref[sb["tok"], :]
        lpos = route.astype(I32)
        gmat = jnp.zeros((tm, n_loc), F32)
        for kq in range(TOP_K):
            gmat = jnp.where(q == lpos[:, TOP_K + kq:TOP_K + kq + 1], route[:, kq:kq + 1], gmat)
        sb["ghi"], sb["glo"] = _split2(gmat)
    for sb in subs:
        sb["moe"] = (jnp.dot(sb["ghi"], sb["y"], preferred_element_type=F32)
                     + jnp.dot(sb["glo"], sb["y"], preferred_element_type=F32))
    for sb in subs:
        h = h_ref[sb["tok"], :] + gt_ref[0, 0] * sb["moe"]
        o_ref[sb["tok"], :] = _rms(h) * fg_ref[...]


def _combine(y_rows, run_start, run_local, run_count, route, h, mod6, final_g, tiles_per_batch):
    n_tok = h.shape[0]
    tm = COMBINE_TILE * COMBINE_TILES_PER_STEP
    grid_spec = pltpu.PrefetchScalarGridSpec(
        num_scalar_prefetch=3,
        grid=(n_tok // tm,),
        in_specs=[pl.BlockSpec(memory_space=pl.ANY),
                  pl.BlockSpec((tm, LANES), lambda i, *_: (i, 0)),
                  pl.BlockSpec((tm, D_MODEL), lambda i, *_: (i, 0)),
                  pl.BlockSpec((1, 1, 1, D_MODEL), lambda i, *_: (i // tiles_per_batch, 5, 0, 0)),
                  pl.BlockSpec((1, D_MODEL), lambda i, *_: (0, 0))],
        out_specs=pl.BlockSpec((tm, D_MODEL), lambda i, *_: (i, 0)),
        scratch_shapes=[pltpu.VMEM((2, tm * TOP_K * (D_MODEL // LANES), LANES), F32),
                        pltpu.SemaphoreType.DMA((2,))],
    )
    return pl.pallas_call(
        _combine_body,
        name="combine_rows",
        grid_spec=grid_spec,
        out_shape=jax.ShapeDtypeStruct((n_tok, D_MODEL), F32),
        compiler_params=_cparams("arbitrary"),
    )(run_start, run_local, run_count, y_rows, route, h, mod6, final_g)


def kernel(x, c, ada_w, ada_b, norm1_g, w_in, conv_w, conv_gn, rwkv_mu, rwkv_w0, rwkv_w_up,
           rwkv_a0, rwkv_a_up, rwkv_g_up, rwkv_k_k, rwkv_k_a, rwkv_r_k, rwkv_ln_w, rwkv_ln_b,
           w_out, norm2_g, router_w, router_b, exp_w1, exp_b1, exp_w2, exp_b2, final_g):
    bsz, seq, _ = x.shape
    depth = ada_w.shape[0]
    assert depth == 1, "the final norm is fused into the last layer's combine step"
    n_tok = bsz * seq
    assert seq % TOKEN_TILE == 0 and seq % CHUNK == 0 and seq % COMBINE_TILE == 0
    n_rows = n_tok * TOP_K
    assert n_rows % ROW_BLOCK == 0
    n_blocks = n_rows // ROW_BLOCK + N_EXPERTS
    m_pad = n_blocks * ROW_BLOCK

    h = x.reshape(n_tok, D_MODEL)
    out = h
    for l in range(depth):
        mod6 = _ada_mod(c, ada_w[l], ada_b[l]).reshape(bsz, 6, 1, D_MODEL)
        yc, pr = _in_proj_conv(h.reshape(bsz, seq, D_MODEL), norm1_g[l].reshape(1, -1), mod6,
                               w_in[l].astype(BF16), conv_w[l], conv_gn[l].reshape(1, -1))
        yr = _rwkv_mixer(pr, rwkv_mu[l], rwkv_w0[l], rwkv_w_up[l], rwkv_a0[l], rwkv_a_up[l],
                         rwkv_g_up[l], rwkv_k_k[l], rwkv_k_a[l], rwkv_r_k[l], rwkv_ln_w[l],
                         rwkv_ln_b[l], bsz, seq)
        rw = router_w[l].T
        rw_hi = rw.astype(BF16)
        rw_lo = (rw - rw_hi.astype(F32)).astype(BF16)
        rb = jnp.broadcast_to(router_b[l][:, None], (N_EXPERTS, LANES))
        h, route, tab, x_local = _out_proj_router(
            yc.reshape(n_tok, D_CONV), yr.reshape(n_tok, D_RWKV), h, w_out[l].astype(BF16), mod6,
            norm2_g[l].reshape(1, -1), rw_hi, rw_lo, rb, seq // TOKEN_TILE)
        run_count = tab[:, :, 0].astype(I32)
        run_local = tab[:, :, 1].astype(I32)
        counts = jnp.sum(run_count, axis=0)
        padded = (counts + ROW_BLOCK - 1) // ROW_BLOCK * ROW_BLOCK
        pad_end = jnp.cumsum(padded)
        pad_start = pad_end - padded
        run_start = pad_start[None, :] + jnp.cumsum(run_count, axis=0) - run_count
        n_used = (pad_end[-1] // ROW_BLOCK).astype(I32).reshape(1)
        blk_row = jnp.minimum(jnp.arange(n_blocks, dtype=I32), n_used[0] - 1) * ROW_BLOCK
        block_expert = jnp.minimum(
            jnp.sum((pad_end[None, :] <= blk_row[:, None]).astype(I32), axis=1), N_EXPERTS - 1)
        blk_start = run_start.T[block_expert]
        blk_end = blk_start + run_count.T[block_expert]
        first_tile = jnp.sum((blk_end <= blk_row[:, None]).astype(I32), axis=1)
        last_tile = jnp.sum((blk_start < blk_row[:, None] + ROW_BLOCK).astype(I32), axis=1) - 1
        n_valid = jnp.clip((pad_start + counts)[block_expert] - blk_row, 0, ROW_BLOCK).astype(I32)
        run_tables = (run_start.reshape(-1), run_local.reshape(-1), run_count.reshape(-1))
        y_rows = _experts(x_local, (block_expert, n_used, first_tile, last_tile, n_valid), run_tables,
                          _deinterleave_w1(exp_w1[l]), exp_w2[l],
                          exp_b1[l][:, None, 0::2], exp_b1[l][:, None, 1::2], exp_b2[l][:, None, :],
                          m_pad)
        out = _combine(y_rows, *run_tables, route, h, mod6, final_g.reshape(1, -1),
                       seq // (COMBINE_TILE * COMBINE_TILES_PER_STEP))
    return out.reshape(bsz, seq, D_MODEL)
```

```python
import jax
import jax.numpy as jnp
from jax import lax
from jax.experimental import pallas as pl
from jax.experimental.pallas import tpu as pltpu

F32 = jnp.float32
BF16 = jnp.bfloat16
I32 = jnp.int32

D_MODEL = 1024
D_CONV = 512
CONV_GROUP_DIM = 64
D_RWKV = 512
HEAD_DIM = 64
N_HEADS = D_RWKV // HEAD_DIM
DECAY_LORA = 64
AAA_LORA = 64
GATE_LORA = 128
COLS_CONV = 3 * D_CONV
COLS_RWKV = 3 * D_RWKV + DECAY_LORA + AAA_LORA + GATE_LORA
N_EXPERTS = 32
TOP_K = 4
D_EXPERT = D_MODEL
SWIGLU_LIMIT = 7.0
SWIGLU_ALPHA = 1.702
NORM_EPS = 1e-5
GN_EPS = 64e-5

LANES = 128
SUBLANES = 8
MXU_TILE = 256
CHUNK = 64
RWKV_SEQS_PER_STEP = 8
RWKV_STAGE_LAG = 3
ROW_BLOCK = 512
TOKEN_TILE = 512
COMBINE_TILE = 256
COMBINE_TILES_PER_STEP = 2
VMEM_LIMIT = 48 * 1024 * 1024
EXPERT_VMEM_LIMIT = 56 * 1024 * 1024


def _cparams(*sem):
    return pltpu.CompilerParams(dimension_semantics=sem, vmem_limit_bytes=VMEM_LIMIT)


def _mm(a, b):
    return jnp.dot(a.astype(BF16), b.astype(BF16), preferred_element_type=F32)


def _mm_nt(a, b):
    return lax.dot_general(a.astype(BF16), b.astype(BF16), (((1,), (1,)), ((), ())),
                           preferred_element_type=F32)


def _mm_tn(a, b):
    return lax.dot_general(a.astype(BF16), b.astype(BF16), (((0,), (0,)), ((), ())),
                           preferred_element_type=F32)


def _split2(x):
    hi = x.astype(BF16)
    lo = (x - hi.astype(F32)).astype(BF16)
    return hi, lo


def _split3(x):
    hi = x.astype(BF16)
    r = x - hi.astype(F32)
    mid = r.astype(BF16)
    lo = (r - mid.astype(F32)).astype(BF16)
    return hi, mid, lo


def _sigmoid(x):
    return 1.0 / (1.0 + jnp.exp(-x))


def _rms(x):
    return x * lax.rsqrt(jnp.mean(x * x, axis=-1, keepdims=True) + NORM_EPS)


def _ada_body(c_ref, w_ref, b_ref, o_ref):
    c = c_ref[...]
    ca = c * _sigmoid(c)
    ah, al = _split2(ca)
    wh, wl = _split2(w_ref[...])
    acc = jnp.dot(ah, wh, preferred_element_type=F32)
    acc += jnp.dot(ah, wl, preferred_element_type=F32)
    acc += jnp.dot(al, wh, preferred_element_type=F32)
    o_ref[...] = acc + b_ref[...]


def _ada_mod(c, ada_w, ada_b):
    bsz = c.shape[0]
    n = ada_w.shape[1]
    tn = 1024
    return pl.pallas_call(
        _ada_body,
        name="ada_mod",
        grid=(n // tn,),
        in_specs=[pl.BlockSpec((bsz, D_MODEL), lambda j: (0, 0)),
                  pl.BlockSpec((D_MODEL, tn), lambda j: (0, j)),
                  pl.BlockSpec((1, tn), lambda j: (0, j))],
        out_specs=pl.BlockSpec((bsz, tn), lambda j: (0, j)),
        out_shape=jax.ShapeDtypeStruct((bsz, n), F32),
        compiler_params=_cparams("parallel"),
    )(c, ada_w, ada_b.reshape(1, n))


def _group_sums(x, group):
    li = lax.broadcasted_iota(I32, (LANES, LANES), 0)
    lj = lax.broadcasted_iota(I32, (LANES, LANES), 1)
    same = jnp.where((li < group) == (lj < group), 1.0, 0.0).astype(BF16)
    hi, lo = _split2(x)
    cols = []
    for t in range(x.shape[1] // LANES):
        ts = slice(t * LANES, (t + 1) * LANES)
        cols.append(jnp.dot(hi[:, ts], same, preferred_element_type=F32)
                    + jnp.dot(lo[:, ts], same, preferred_element_type=F32))
    return jnp.concatenate(cols, axis=1)


def _inproj_body(x_ref, g_ref, sh_ref, sc_ref, w_ref, cw_ref, gn_ref, yc_ref, or_ref, carry_ref):
    @pl.when(pl.program_id(1) == 0)
    def _():
        carry_ref[...] = jnp.zeros_like(carry_ref)

    ub = _rms(x_ref[0]) * g_ref[...]
    ub = (ub * (1.0 + sc_ref[0, 0]) + sh_ref[0, 0]).astype(BF16)
    p = jnp.dot(ub, w_ref[:, :COLS_CONV], preferred_element_type=F32)
    or_ref[0] = jnp.dot(ub, w_ref[:, COLS_CONV:], preferred_element_type=F32)

    tc = p.shape[0]
    b_gate = p[:, :D_CONV]
    u = p[:, D_CONV:2 * D_CONV] * p[:, 2 * D_CONV:]
    ext = jnp.concatenate([carry_ref[...], u], axis=0)
    u1 = pltpu.roll(ext, 1, axis=0)[SUBLANES:]
    u2 = pltpu.roll(ext, 2, axis=0)[SUBLANES:]
    carry_ref[...] = u[tc - SUBLANES:, :]
    cw = cw_ref[...]
    y = b_gate * (cw[0:1] * u2 + cw[1:2] * u1 + cw[2:3] * u)
    ms = _group_sums(y * y, CONV_GROUP_DIM) * (1.0 / CONV_GROUP_DIM)
    yc_ref[0] = y * lax.rsqrt(ms + NORM_EPS) * gn_ref[...]


def _in_proj_conv(x, g1, mod6, w_in_bf, conv_w, conv_gn):
    bsz, seq, _ = x.shape
    tm = TOKEN_TILE
    n_in = w_in_bf.shape[1]
    const = lambda m, n: pl.BlockSpec((m, n), lambda b, i: (0, 0))
    return pl.pallas_call(
        _inproj_body,
        name="in_proj_conv",
        grid=(bsz, seq // tm),
        in_specs=[pl.BlockSpec((1, tm, D_MODEL), lambda b, i: (b, i, 0)),
                  const(1, D_MODEL),
                  pl.BlockSpec((1, 1, 1, D_MODEL), lambda b, i: (b, 0, 0, 0)),
                  pl.BlockSpec((1, 1, 1, D_MODEL), lambda b, i: (b, 1, 0, 0)),
                  const(D_MODEL, n_in), const(3, D_CONV), const(1, D_CONV)],
        out_specs=[pl.BlockSpec((1, tm, D_CONV), lambda b, i: (b, i, 0)),
                   pl.BlockSpec((1, tm, COLS_RWKV), lambda b, i: (b, i, 0))],
        out_shape=[jax.ShapeDtypeStruct((bsz, seq, D_CONV), F32),
                   jax.ShapeDtypeStruct((bsz, seq, COLS_RWKV), F32)],
        scratch_shapes=[pltpu.VMEM((SUBLANES, D_CONV), F32)],
        compiler_params=_cparams("parallel", "arbitrary"),
    )(x, g1, mod6, mod6, w_in_bf, conv_w, conv_gn)


def _rwkv_body(p_ref, mu_ref, w0_ref, wup_ref, a0_ref, aup_ref, gup_ref, kkw_ref, kaw_ref,
               rkw_ref, lnw_ref, lnb_ref, o_ref, state_ref, prev_ref):
    @pl.when(pl.program_id(1) == 0)
    def _():
        state_ref[...] = jnp.zeros_like(state_ref)
        prev_ref[...] = jnp.zeros_like(prev_ref)

    C = CHUNK
    N = HEAD_DIM
    nb = p_ref.shape[0]
    s1, s2, s3 = D_RWKV, 2 * D_RWKV, 3 * D_RWKV
    row1 = lax.broadcasted_iota(I32, (C, 1), 0)
    ri = lax.broadcasted_iota(I32, (C, C), 0)
    ci = lax.broadcasted_iota(I32, (C, C), 1)
    strict = ri > ci
    incl = ri >= ci
    ri2 = lax.broadcasted_iota(I32, (C, 2 * C), 0)
    ci2 = lax.broadcasted_iota(I32, (C, 2 * C), 1)
    ci2 = jnp.where(ci2 >= C, ci2 - C, ci2)
    strict2 = ri2 > ci2
    incl2 = ri2 >= ci2
    eye = jnp.where(ri == ci, 1.0, 0.0).astype(F32)
    tri = jnp.where(incl, 1.0, 0.0).astype(BF16)

    def stages(bb):
        units = []
        P = p_ref[bb]
        Pprev = jnp.where(row1 == 0, prev_ref[bb, 0:1, :], pltpu.roll(P, 1, axis=0))
        prev_ref[bb] = jnp.broadcast_to(P[C - 1:C, :], prev_ref.shape[1:])
        p = P + (Pprev - P) * mu_ref[...]
        r = p[:, :s1]
        k = p[:, s1:s2]
        v = p[:, s2:s3]
        dw = p[:, s3:s3 + DECAY_LORA]
        da = p[:, s3 + DECAY_LORA:s3 + DECAY_LORA + AAA_LORA]
        dg = p[:, s3 + DECAY_LORA + AAA_LORA:]
        z = -(w0_ref[...] + _mm(jnp.tanh(dw), wup_ref[...]))
        softplus = jnp.maximum(z, 0.0) + jnp.log(1.0 + jnp.exp(-jnp.abs(z)))
        logw = -jnp.exp(-softplus - 0.5)
        a = _sigmoid(a0_ref[...] + _mm(da, aup_ref[...]))
        g = _mm(_sigmoid(dg), gup_ref[...])
        kk = k * kkw_ref[...]
        kmod = k * (1.0 + (a - 1.0) * kaw_ref[...])
        cum = sum(jnp.dot(tri, part, preferred_element_type=F32) for part in _split3(logw))
        gam = jnp.exp(cum)
        gam_prev = jnp.exp(cum - logw)
        ginv = jnp.exp(-cum)
        def head_sum(x):
            return _group_sums(x, N)

        kkn = kk * lax.rsqrt(jnp.maximum(head_sum(kk * kk), 1e-24))
        bonus_v = head_sum(r * kmod * rkw_ref[...]) * v
        at_all = (-kkn * gam_prev).astype(BF16)
        rt_all = (r * gam).astype(BF16)
        bt_all = (kkn * a * ginv).astype(BF16)
        kt_all = (kmod * ginv).astype(BF16)
        vb_all = v.astype(BF16)
        for h in range(N_HEADS):
            sl = slice(h * N, (h + 1) * N)
            At, Rt = at_all[:, sl], rt_all[:, sl]
            units.append(dict(
                slot=bb * N_HEADS + h, At=At, Rt=Rt, vb=vb_all[:, sl],
                BK=jnp.concatenate([bt_all[:, sl], kt_all[:, sl]], axis=0),
                AR=jnp.concatenate([At, Rt], axis=0), glast=gam[C - 1:C, sl]))
        yield
        for u in units:
            M = _mm_nt(u["AR"], u["BK"])
            top = jnp.where(strict2, M[:C, :], 0.0)
            u["T"] = eye + top[:, :C]
            u["LL"] = top.astype(BF16)
            u["Lp"] = top[:, :C].astype(BF16)
            u["Tr"] = jnp.where(incl2, M[C:, :], 0.0).astype(BF16)
        yield
        for u in units:
            u["X2"] = _mm(u["LL"], jnp.concatenate([jnp.zeros_like(u["vb"]), u["vb"]], axis=0))
            u["Lp"] = _mm(u["Lp"], u["Lp"]).astype(BF16)
        yield
        for _ in range(4):
            for u in units:
                u["T"] = u["T"] + _mm(u["T"], u["Lp"])
                u["Lp"] = _mm(u["Lp"], u["Lp"]).astype(BF16)
            yield
        for u in units:
            u["T"] = u["T"] + _mm(u["T"], u["Lp"])
        yield
        for u in units:
            u["W"] = _mm(u["T"], jnp.concatenate([u["At"], u["X2"].astype(BF16)], axis=1))
        yield
        for u in units:
            u["S0"] = state_ref[u["slot"]]
            u["S0b"] = u["S0"].astype(BF16)
            U = _mm_nt(u["W"][:, :N], u["S0b"]) + u["W"][:, N:]
            u["UV"] = jnp.concatenate([U.astype(BF16), u["vb"]], axis=0)
        yield
        for u in units:
            u["O"] = _mm_nt(u["Rt"], u["S0b"]) + _mm(u["Tr"], u["UV"])
            state_ref[u["slot"]] = (u["S0"] + _mm_tn(u["UV"], u["BK"])) * u["glast"]
        yield
        o_all = jnp.concatenate([u["O"] for u in units], axis=-1)
        cen = o_all - head_sum(o_all) * (1.0 / N)
        var = head_sum(cen * cen) * (1.0 / N)
        on = cen * lax.rsqrt(var + GN_EPS) * lnw_ref[...] + lnb_ref[...]
        o_ref[bb] = (on + bonus_v) * g
        yield

    live = [(bb, stages(bb)) for bb in range(nb)]
    tick = 0
    while live:
        for item in list(live):
            if tick >= RWKV_STAGE_LAG * item[0] and next(item[1], "done") == "done":
                live.remove(item)
        tick += 1


def _rwkv_mixer(pr, mu, w0, w_up, a0, a_up, g_up, k_k, k_a, r_k, ln_w, ln_b, bsz, seq):
    C = CHUNK
    nb = RWKV_SEQS_PER_STEP if bsz % RWKV_SEQS_PER_STEP == 0 else 1
    row = lambda n: pl.BlockSpec((1, n), lambda b, i: (0, 0))
    full = lambda m, n: pl.BlockSpec((m, n), lambda b, i: (0, 0))
    return pl.pallas_call(
        _rwkv_body,
        name="rwkv_mixer",
        grid=(bsz // nb, seq // C),
        in_specs=[pl.BlockSpec((nb, C, COLS_RWKV), lambda b, i: (b, i, 0)),
                  row(COLS_RWKV), row(D_RWKV), full(DECAY_LORA, D_RWKV), row(D_RWKV),
                  full(AAA_LORA, D_RWKV), full(GATE_LORA, D_RWKV), row(D_RWKV), row(D_RWKV),
                  row(D_RWKV), row(D_RWKV), row(D_RWKV)],
        out_specs=pl.BlockSpec((nb, C, D_RWKV), lambda b, i: (b, i, 0)),
        out_shape=jax.ShapeDtypeStruct((bsz, seq, D_RWKV), F32),
        scratch_shapes=[pltpu.VMEM((nb * N_HEADS, HEAD_DIM, HEAD_DIM), F32),
                        pltpu.VMEM((nb, SUBLANES, COLS_RWKV), F32)],
        compiler_params=_cparams("parallel", "arbitrary"),
    )(pr.reshape(bsz, seq, COLS_RWKV), mu.reshape(1, -1), w0.reshape(1, -1), w_up,
      a0.reshape(1, -1), a_up, g_up, k_k.reshape(1, -1), k_a.reshape(1, -1), r_k.reshape(1, -1),
      ln_w.reshape(1, -1), ln_b.reshape(1, -1))


def _outproj_body(yc_ref, yr_ref, x_ref, wo_ref, gt_ref, g2_ref, sh_ref, sc_ref, rwh_ref, rwl_ref,
                  rb_ref, h_ref, route_ref, tab_ref, xl_ref):
    mix = _mm(yc_ref[...], wo_ref[:D_CONV, :]) + _mm(yr_ref[...], wo_ref[D_CONV:, :])
    h = x_ref[...] + gt_ref[0, 0] * mix
    h_ref[...] = h
    u2 = _rms(h) * g2_ref[...]
    u2 = u2 * (1.0 + sc_ref[0, 0]) + sh_ref[0, 0]

    uh, ul = _split2(u2)
    tm = u2.shape[0]
    logits = (_mm_nt(rwh_ref[...], uh) + _mm_nt(rwl_ref[...], uh) + _mm_nt(rwh_ref[...], ul)
              + jnp.concatenate([rb_ref[...]] * (tm // LANES), axis=1))
    eid = lax.broadcasted_iota(I32, (N_EXPERTS, tm), 0).astype(F32)
    neg = jnp.float32(-jnp.inf)
    l = logits
    tops, idxs = [], []
    for _ in range(TOP_K):
        m = jnp.max(l, axis=0, keepdims=True)
        ix = jnp.min(jnp.where(l == m, eid, float(N_EXPERTS)), axis=0, keepdims=True)
        tops.append(m)
        idxs.append(ix)
        l = jnp.where(eid == ix, neg, l)
    es = [jnp.exp(t - tops[0]) for t in tops]
    den = es[0] + es[1] + es[2] + es[3]
    gates = [e / den for e in es]
    lpos_rows = _sort_tiles(eid, idxs, uh, tab_ref, xl_ref)
    row8 = lax.broadcasted_iota(I32, (SUBLANES, tm), 0)
    packed = jnp.zeros((SUBLANES, tm), F32)
    for kq, val in enumerate(gates + lpos_rows):
        packed = jnp.where(row8 == kq, val, packed)
    route_ref[...] = jnp.transpose(
        jnp.concatenate([packed, jnp.zeros((LANES - SUBLANES, tm), F32)], axis=0))


def _out_proj_router(yc, yr, xf, w_out_bf, mod6, g2, rw_hi, rw_lo, rb, tiles_per_batch):
    n_tok = xf.shape[0]
    tm = TOKEN_TILE
    n_sub = tm // COMBINE_TILE
    n_seg = D_MODEL // LANES
    tok = lambda n: pl.BlockSpec((tm, n), lambda i: (i, 0))
    modspec = lambda which: pl.BlockSpec((1, 1, 1, D_MODEL),
                                         lambda i: (i // tiles_per_batch, which, 0, 0))
    const = lambda m, n: pl.BlockSpec((m, n), lambda i: (0, 0))
    return pl.pallas_call(
        _outproj_body,
        name="out_proj_router",
        grid=(n_tok // tm,),
        in_specs=[tok(D_CONV), tok(D_RWKV), tok(D_MODEL), const(D_MODEL, D_MODEL),
                  modspec(2), const(1, D_MODEL), modspec(3), modspec(4),
                  const(N_EXPERTS, D_MODEL), const(N_EXPERTS, D_MODEL), const(N_EXPERTS, LANES)],
        out_specs=[tok(D_MODEL), tok(LANES),
                   pl.BlockSpec((n_sub, N_EXPERTS, LANES), lambda i: (i, 0, 0)),
                   pl.BlockSpec((tm * TOP_K * n_seg, LANES), lambda i: (i, 0))],
        out_shape=[jax.ShapeDtypeStruct((n_tok, D_MODEL), F32),
                   jax.ShapeDtypeStruct((n_tok, LANES), F32),
                   jax.ShapeDtypeStruct((n_tok // COMBINE_TILE, N_EXPERTS, LANES), F32),
                   jax.ShapeDtypeStruct((n_tok * TOP_K * n_seg, LANES), F32)],
        compiler_params=_cparams("parallel"),
    )(yc, yr, xf, w_out_bf, mod6, g2, mod6, mod6, rw_hi, rw_lo, rb)


def _sort_tiles(eid, idxs, u_all, tab_ref, xl_ref):
    tm = COMBINE_TILE
    n_sub = u_all.shape[0] // tm
    n_loc = tm * TOP_K
    n_seg = D_MODEL // LANES
    rr = lax.broadcasted_iota(I32, (tm, tm), 0)
    cc = lax.broadcasted_iota(I32, (tm, tm), 1)
    before = jnp.where(rr < cc, 1.0, 0.0).astype(BF16)
    ones_col = jnp.ones((tm, LANES), BF16)
    er = lax.broadcasted_iota(I32, (N_EXPERTS, N_EXPERTS), 0)
    ec = lax.broadcasted_iota(I32, (N_EXPERTS, N_EXPERTS), 1)
    lower = jnp.where(er > ec, 1.0, 0.0).astype(BF16)
    lane = lax.broadcasted_iota(I32, (N_EXPERTS, LANES), 1)
    q = lax.broadcasted_iota(I32, (n_loc, tm), 0)
    subs = [dict(tok=slice(s * tm, (s + 1) * tm)) for s in range(n_sub)]
    chosen = [jnp.broadcast_to(ix, eid.shape) for ix in idxs]
    eid_tile = lax.broadcasted_iota(I32, (N_EXPERTS, tm), 0).astype(F32)

    for sb in subs:
        sb["masks"] = [eid_tile == ch[:, sb["tok"]] for ch in chosen]
        sel = jnp.zeros((N_EXPERTS, tm), F32)
        for mk in sb["masks"]:
            sel = sel + jnp.where(mk, 1.0, 0.0)
        sb["sel"] = sel.astype(BF16)
    for sb in subs:
        sb["prefix"] = jnp.dot(sb["sel"], before, preferred_element_type=F32)
        sb["cnt"] = jnp.dot(sb["sel"], ones_col, preferred_element_type=F32)
    for sb in subs:
        sb["loff"] = jnp.dot(lower, sb["cnt"].astype(BF16), preferred_element_type=F32)
    for s, sb in enumerate(subs):
        tab_ref[s] = jnp.where(lane == 0, sb["cnt"], jnp.where(lane == 1, sb["loff"], 0.0))
        local_row = jnp.concatenate([sb["loff"]] * (tm // LANES), axis=1) + sb["prefix"]
        sb["lpos"] = [jnp.sum(jnp.where(mk, local_row, 0.0), axis=0, keepdims=True)
                      for mk in sb["masks"]]
    for sb in subs:
        pick = jnp.zeros((n_loc, tm), F32)
        for lp in sb["lpos"]:
            pick = jnp.where(q == lp.astype(I32), 1.0, pick)
        sb["xl"] = jnp.dot(pick.astype(BF16), u_all[sb["tok"], :], preferred_element_type=F32)
    for s, sb in enumerate(subs):
        for j in range(n_seg):
            xl_ref[pl.ds(s * n_loc * n_seg + j, n_loc, stride=n_seg), :] = (
                sb["xl"][:, j * LANES:(j + 1) * LANES])
    return [jnp.concatenate([jnp.broadcast_to(sb["lpos"][kq], (SUBLANES, tm)) for sb in subs], axis=1)
            for kq in range(TOP_K)]


def _deinterleave_into(w_ref, o_ref):
    d_in, two_f = o_ref.shape
    half = two_f // 2
    src = lax.broadcasted_iota(I32, (2 * MXU_TILE, MXU_TILE), 0)
    dst = lax.broadcasted_iota(I32, (2 * MXU_TILE, MXU_TILE), 1)
    pick_even = jnp.where(src == 2 * dst, 1.0, 0.0).astype(BF16)
    pick_odd = jnp.where(src == 2 * dst + 1, 1.0, 0.0).astype(BF16)
    for r in range(d_in // MXU_TILE):
        rows = slice(r * MXU_TILE, (r + 1) * MXU_TILE)
        for j in range(half // MXU_TILE):
            slab = w_ref[0, rows, 2 * MXU_TILE * j:2 * MXU_TILE * (j + 1)].astype(BF16)
            lo = MXU_TILE * j
            o_ref[rows, lo:lo + MXU_TILE] = jnp.dot(
                slab, pick_even, preferred_element_type=F32).astype(BF16)
            o_ref[rows, half + lo:half + lo + MXU_TILE] = jnp.dot(
                slab, pick_odd, preferred_element_type=F32).astype(BF16)


def _expert_body(be_ref, nu_ref, ft_ref, lt_ref, nv_ref, rg_ref, rl_ref, rc_ref,
                 xl_ref, w1_ref, w2_ref, bg_ref, bl_ref, b2_ref, y_ref, xbuf_ref, w1d_ref, sem):
    i = pl.program_id(0)
    n_used = nu_ref[0]
    bm = ROW_BLOCK
    n_seg = D_MODEL // LANES
    n_loc = COMBINE_TILE * TOP_K

    def issue(blk, slot):
        e = be_ref[blk]
        r0 = blk * bm

        def piece(t, carry):
            g = rg_ref[t * N_EXPERTS + e]
            s = jnp.maximum(g, r0)
            n = jnp.minimum(g + rc_ref[t * N_EXPERTS + e], r0 + bm) - s

            @pl.when(n > 0)
            def _():
                src = pl.multiple_of((t * n_loc + rl_ref[t * N_EXPERTS + e] + s - g) * n_seg, n_seg)
                dst = pl.multiple_of((s - r0) * n_seg, n_seg)
                pltpu.make_async_copy(xl_ref.at[pl.ds(src, n * n_seg)],
                                      xbuf_ref.at[slot, pl.ds(dst, n * n_seg)], sem.at[slot]).start()
            return carry

        lax.fori_loop(ft_ref[blk], lt_ref[blk] + 1, piece, 0)

    @pl.when(i == 0)
    def _():
        xbuf_ref[...] = jnp.zeros_like(xbuf_ref)
        issue(0, 0)

    @pl.when(i + 1 < n_used)
    def _():
        issue(i + 1, (i + 1) % 2)

    @pl.when(i >= n_used)
    def _():
        y_ref[...] = jnp.zeros_like(y_ref)

    new_expert = jnp.logical_or(i == 0, be_ref[i] != be_ref[jnp.maximum(i - 1, 0)])

    @pl.when(jnp.logical_and(new_expert, i < n_used))
    def _():
        _deinterleave_into(w1_ref, w1d_ref)

    @pl.when(i < n_used)
    def _():
        slot = i % 2
        n_valid = nv_ref[i] * n_seg
        pltpu.make_async_copy(xl_ref.at[pl.ds(0, n_valid)], xbuf_ref.at[slot, pl.ds(0, n_valid)],
                              sem.at[slot]).wait()
        x = jnp.concatenate([xbuf_ref[slot, pl.ds(j, bm, stride=n_seg), :] for j in range(n_seg)],
                            axis=1).astype(BF16)
        hb = jnp.dot(x, w1d_ref[...], preferred_element_type=F32)
        hg = hb[:, :D_EXPERT] + bg_ref[0]
        hl = hb[:, D_EXPERT:] + bl_ref[0]
        xg = jnp.minimum(hg, SWIGLU_LIMIT)
        xl = jnp.clip(hl, -SWIGLU_LIMIT, SWIGLU_LIMIT)
        act = xg * _sigmoid(SWIGLU_ALPHA * xg) * (xl + 1.0)
        y = jnp.dot(act.astype(BF16), w2_ref[0].astype(BF16),
                    preferred_element_type=F32) + b2_ref[0]
        for j in range(n_seg):
            y_ref[pl.ds(j, bm, stride=n_seg), :] = y[:, j * LANES:(j + 1) * LANES]


def _experts(x_local, block_tables, run_tables, w1, w2, b1g, b1l, b2, m_pad):
    bm = ROW_BLOCK
    n_seg = D_MODEL // LANES
    wsel = lambda i, be, *_: (be[i], 0, 0)
    grid_spec = pltpu.PrefetchScalarGridSpec(
        num_scalar_prefetch=8,
        grid=(m_pad // bm,),
        in_specs=[pl.BlockSpec(memory_space=pl.ANY),
                  pl.BlockSpec((1, D_MODEL, 2 * D_EXPERT), wsel),
                  pl.BlockSpec((1, D_EXPERT, D_MODEL), wsel),
                  pl.BlockSpec((1, 1, D_EXPERT), wsel),
                  pl.BlockSpec((1, 1, D_EXPERT), wsel),
                  pl.BlockSpec((1, 1, D_MODEL), wsel)],
        out_specs=pl.BlockSpec((bm * n_seg, LANES), lambda i, *_: (i, 0)),
        scratch_shapes=[pltpu.VMEM((2, bm * n_seg, LANES), F32),
                        pltpu.VMEM((D_MODEL, 2 * D_EXPERT), BF16),
                        pltpu.SemaphoreType.DMA((2,))],
    )
    return pl.pallas_call(
        _expert_body,
        name="expert_mlp",
        grid_spec=grid_spec,
        out_shape=jax.ShapeDtypeStruct((m_pad * n_seg, LANES), F32),
        compiler_params=pltpu.CompilerParams(dimension_semantics=("arbitrary",),
                                             vmem_limit_bytes=EXPERT_VMEM_LIMIT),
    )(*block_tables, *run_tables, x_local, w1, w2, b1g, b1l, b2)


def _combine_body(rg_ref, rl_ref, rc_ref, y_ref, route_ref, h_ref, gt_ref, fg_ref, o_ref,
                  yloc_ref, sem):
    i = pl.program_id(0)
    tm = COMBINE_TILE
    n_sub = h_ref.shape[0] // tm
    n_seg = D_MODEL // LANES
    n_loc = tm * TOP_K

    def issue(step, slot):
        for s in range(n_sub):
            tile = step * n_sub + s
            for e in range(N_EXPERTS):
                c = rc_ref[tile * N_EXPERTS + e] * n_seg

                @pl.when(c > 0)
                def _():
                    src = pl.multiple_of(rg_ref[tile * N_EXPERTS + e] * n_seg, n_seg)
                    dst = pl.multiple_of((s * n_loc + rl_ref[tile * N_EXPERTS + e]) * n_seg, n_seg)
                    pltpu.make_async_copy(y_ref.at[pl.ds(src, c)], yloc_ref.at[slot, pl.ds(dst, c)],
                                          sem.at[slot]).start()

    @pl.when(i == 0)
    def _():
        issue(0, 0)

    @pl.when(i + 1 < pl.num_programs(0))
    def _():
        issue(i + 1, (i + 1) % 2)

    slot = i % 2
    pltpu.make_async_copy(y_ref.at[pl.ds(0, n_sub * n_loc * n_seg)], yloc_ref.at[slot],
                          sem.at[slot]).wait()

    q = lax.broadcasted_iota(I32, (tm, n_loc), 1)
    subs = [dict(tok=slice(s * tm, (s + 1) * tm)) for s in range(n_sub)]
    for s, sb in enumerate(subs):
        sb["y"] = jnp.concatenate(
            [yloc_ref[slot, pl.ds(s * n_loc * n_seg + j, n_loc, stride=n_seg), :]
             for j in range(n_seg)], axis=1).astype(BF16)
        route = route_ref[sb["tok"], :]
        lpos = route.astype(I32)
        gmat = jnp.zeros((tm, n_loc), F32)
        for kq in range(TOP_K):
            gmat = jnp.where(q == lpos[:, TOP_K + kq:TOP_K + kq + 1], route[:, kq:kq + 1], gmat)
        sb["ghi"], sb["glo"] = _split2(gmat)
    for sb in subs:
        sb["moe"] = (jnp.dot(sb["ghi"], sb["y"], preferred_element_type=F32)
                     + jnp.dot(sb["glo"], sb["y"], preferred_element_type=F32))
    for sb in subs:
        h = h_ref[sb["tok"], :] + gt_ref[0, 0] * sb["moe"]
        o_ref[sb["tok"], :] = _rms(h) * fg_ref[...]


def _combine(y_rows, run_start, run_local, run_count, route, h, mod6, final_g, tiles_per_batch):
    n_tok = h.shape[0]
    tm = COMBINE_TILE * COMBINE_TILES_PER_STEP
    grid_spec = pltpu.PrefetchScalarGridSpec(
        num_scalar_prefetch=3,
        grid=(n_tok // tm,),
        in_specs=[pl.BlockSpec(memory_space=pl.ANY),
                  pl.BlockSpec((tm, LANES), lambda i, *_: (i, 0)),
                  pl.BlockSpec((tm, D_MODEL), lambda i, *_: (i, 0)),
                  pl.BlockSpec((1, 1, 1, D_MODEL), lambda i, *_: (i // tiles_per_batch, 5, 0, 0)),
                  pl.BlockSpec((1, D_MODEL), lambda i, *_: (0, 0))],
        out_specs=pl.BlockSpec((tm, D_MODEL), lambda i, *_: (i, 0)),
        scratch_shapes=[pltpu.VMEM((2, tm * TOP_K * (D_MODEL // LANES), LANES), F32),
                        pltpu.SemaphoreType.DMA((2,))],
    )
    return pl.pallas_call(
        _combine_body,
        name="combine_rows",
        grid_spec=grid_spec,
        out_shape=jax.ShapeDtypeStruct((n_tok, D_MODEL), F32),
        compiler_params=_cparams("arbitrary"),
    )(run_start, run_local, run_count, y_rows, route, h, mod6, final_g)


def kernel(x, c, ada_w, ada_b, norm1_g, w_in, conv_w, conv_gn, rwkv_mu, rwkv_w0, rwkv_w_up,
           rwkv_a0, rwkv_a_up, rwkv_g_up, rwkv_k_k, rwkv_k_a, rwkv_r_k, rwkv_ln_w, rwkv_ln_b,
           w_out, norm2_g, router_w, router_b, exp_w1, exp_b1, exp_w2, exp_b2, final_g):
    bsz, seq, _ = x.shape
    depth = ada_w.shape[0]
    assert depth == 1, "the final norm is fused into the last layer's combine step"
    n_tok = bsz * seq
    assert seq % TOKEN_TILE == 0 and seq % CHUNK == 0 and seq % COMBINE_TILE == 0
    n_rows = n_tok * TOP_K
    assert n_rows % ROW_BLOCK == 0
    n_blocks = n_rows // ROW_BLOCK + N_EXPERTS
    m_pad = n_blocks * ROW_BLOCK

    h = x.reshape(n_tok, D_MODEL)
    out = h
    for l in range(depth):
        mod6 = _ada_mod(c, ada_w[l], ada_b[l]).reshape(bsz, 6, 1, D_MODEL)
        yc, pr = _in_proj_conv(h.reshape(bsz, seq, D_MODEL), norm1_g[l].reshape(1, -1), mod6,
                               w_in[l].astype(BF16), conv_w[l], conv_gn[l].reshape(1, -1))
        yr = _rwkv_mixer(pr, rwkv_mu[l], rwkv_w0[l], rwkv_w_up[l], rwkv_a0[l], rwkv_a_up[l],
                         rwkv_g_up[l], rwkv_k_k[l], rwkv_k_a[l], rwkv_r_k[l], rwkv_ln_w[l],
                         rwkv_ln_b[l], bsz, seq)
        rw = router_w[l].T
        rw_hi = rw.astype(BF16)
        rw_lo = (rw - rw_hi.astype(F32)).astype(BF16)
        rb = jnp.broadcast_to(router_b[l][:, None], (N_EXPERTS, LANES))
        h, route, tab, x_local = _out_proj_router(
            yc.reshape(n_tok, D_CONV), yr.reshape(n_tok, D_RWKV), h, w_out[l].astype(BF16), mod6,
            norm2_g[l].reshape(1, -1), rw_hi, rw_lo, rb, seq // TOKEN_TILE)
        run_count = tab[:, :, 0].astype(I32)
        run_local = tab[:, :, 1].astype(I32)
        counts = jnp.sum(run_count, axis=0)
        padded = (counts + ROW_BLOCK - 1) // ROW_BLOCK * ROW_BLOCK
        pad_end = jnp.cumsum(padded)
        pad_start = pad_end - padded
        run_start = pad_start[None, :] + jnp.cumsum(run_count, axis=0) - run_count
        n_used = (pad_end[-1] // ROW_BLOCK).astype(I32).reshape(1)
        blk_row = jnp.minimum(jnp.arange(n_blocks, dtype=I32), n_used[0] - 1) * ROW_BLOCK
        block_expert = jnp.minimum(
            jnp.sum((pad_end[None, :] <= blk_row[:, None]).astype(I32), axis=1), N_EXPERTS - 1)
        blk_start = run_start.T[block_expert]
        blk_end = blk_start + run_count.T[block_expert]
        first_tile = jnp.sum((blk_end <= blk_row[:, None]).astype(I32), axis=1)
        last_tile = jnp.sum((blk_start < blk_row[:, None] + ROW_BLOCK).astype(I32), axis=1) - 1
        n_valid = jnp.clip((pad_start + counts)[block_expert] - blk_row, 0, ROW_BLOCK).astype(I32)
        run_tables = (run_start.reshape(-1), run_local.reshape(-1), run_count.reshape(-1))
        y_rows = _experts(x_local, (block_expert, n_used, first_tile, last_tile, n_valid), run_tables,
                          exp_w1[l], exp_w2[l],
                          exp_b1[l][:, None, 0::2], exp_b1[l][:, None, 1::2], exp_b2[l][:, None, :],
                          m_pad)
        out = _combine(y_rows, *run_tables, route, h, mod6, final_g.reshape(1, -1),
                       seq // (COMBINE_TILE * COMBINE_TILES_PER_STEP))
    return out.reshape(bsz, seq, D_MODEL)
```

```python
import jax
import jax.numpy as jnp
from jax import lax
from jax.experimental import pallas as pl
from jax.experimental.pallas import tpu as pltpu

F32 = jnp.float32
BF16 = jnp.bfloat16
I32 = jnp.int32

D_MODEL = 1024
D_CONV = 512
CONV_GROUP_DIM = 64
D_RWKV = 512
HEAD_DIM = 64
N_HEADS = D_RWKV // HEAD_DIM
DECAY_LORA = 64
AAA_LORA = 64
GATE_LORA = 128
COLS_CONV = 3 * D_CONV
COLS_RWKV = 3 * D_RWKV + DECAY_LORA + AAA_LORA + GATE_LORA
N_EXPERTS = 32
TOP_K = 4
D_EXPERT = D_MODEL
SWIGLU_LIMIT = 7.0
SWIGLU_ALPHA = 1.702
NORM_EPS = 1e-5
GN_EPS = 64e-5

LANES = 128
SUBLANES = 8
MXU_TILE = 256
CHUNK = 64
RWKV_SEQS_PER_STEP = 16
RWKV_STAGE_LAG = 3
ROW_BLOCK = 512
TOKEN_TILE = 512
COMBINE_TILE = 256
COMBINE_TILES_PER_STEP = 2
VMEM_LIMIT = 48 * 1024 * 1024
EXPERT_VMEM_LIMIT = 56 * 1024 * 1024


def _cparams(*sem):
    return pltpu.CompilerParams(dimension_semantics=sem, vmem_limit_bytes=VMEM_LIMIT)


def _mm(a, b):
    return jnp.dot(a.astype(BF16), b.astype(BF16), preferred_element_type=F32)


def _mm_nt(a, b):
    return lax.dot_general(a.astype(BF16), b.astype(BF16), (((1,), (1,)), ((), ())),
                           preferred_element_type=F32)


def _mm_tn(a, b):
    return lax.dot_general(a.astype(BF16), b.astype(BF16), (((0,), (0,)), ((), ())),
                           preferred_element_type=F32)


def _split2(x):
    hi = x.astype(BF16)
    lo = (x - hi.astype(F32)).astype(BF16)
    return hi, lo


def _split3(x):
    hi = x.astype(BF16)
    r = x - hi.astype(F32)
    mid = r.astype(BF16)
    lo = (r - mid.astype(F32)).astype(BF16)
    return hi, mid, lo


def _sigmoid(x):
    return 1.0 / (1.0 + jnp.exp(-x))


def _rms(x):
    return x * lax.rsqrt(jnp.mean(x * x, axis=-1, keepdims=True) + NORM_EPS)


def _ada_body(c_ref, w_ref, b_ref, o_ref):
    c = c_ref[...]
    ca = c * _sigmoid(c)
    ah, al = _split2(ca)
    wh, wl = _split2(w_ref[...])
    acc = jnp.dot(ah, wh, preferred_element_type=F32)
    acc += jnp.dot(ah, wl, preferred_element_type=F32)
    acc += jnp.dot(al, wh, preferred_element_type=F32)
    o_ref[...] = acc + b_ref[...]


def _ada_mod(c, ada_w, ada_b):
    bsz = c.shape[0]
    n = ada_w.shape[1]
    tn = 1024
    return pl.pallas_call(
        _ada_body,
        name="ada_mod",
        grid=(n // tn,),
        in_specs=[pl.BlockSpec((bsz, D_MODEL), lambda j: (0, 0)),
                  pl.BlockSpec((D_MODEL, tn), lambda j: (0, j)),
                  pl.BlockSpec((1, tn), lambda j: (0, j))],
        out_specs=pl.BlockSpec((bsz, tn), lambda j: (0, j)),
        out_shape=jax.ShapeDtypeStruct((bsz, n), F32),
        compiler_params=_cparams("parallel"),
    )(c, ada_w, ada_b.reshape(1, n))


def _group_sums(x, group):
    li = lax.broadcasted_iota(I32, (LANES, LANES), 0)
    lj = lax.broadcasted_iota(I32, (LANES, LANES), 1)
    same = jnp.where((li < group) == (lj < group), 1.0, 0.0).astype(BF16)
    hi, lo = _split2(x)
    cols = []
    for t in range(x.shape[1] // LANES):
        ts = slice(t * LANES, (t + 1) * LANES)
        cols.append(jnp.dot(hi[:, ts], same, preferred_element_type=F32)
                    + jnp.dot(lo[:, ts], same, preferred_element_type=F32))
    return jnp.concatenate(cols, axis=1)


def _inproj_body(x_ref, g_ref, sh_ref, sc_ref, w_ref, cw_ref, gn_ref, yc_ref, or_ref, carry_ref):
    @pl.when(pl.program_id(1) == 0)
    def _():
        carry_ref[...] = jnp.zeros_like(carry_ref)

    ub = _rms(x_ref[0]) * g_ref[...]
    ub = (ub * (1.0 + sc_ref[0, 0]) + sh_ref[0, 0]).astype(BF16)
    p = jnp.dot(ub, w_ref[:, :COLS_CONV], preferred_element_type=F32)
    or_ref[0] = jnp.dot(ub, w_ref[:, COLS_CONV:], preferred_element_type=F32)

    tc = p.shape[0]
    b_gate = p[:, :D_CONV]
    u = p[:, D_CONV:2 * D_CONV] * p[:, 2 * D_CONV:]
    ext = jnp.concatenate([carry_ref[...], u], axis=0)
    u1 = pltpu.roll(ext, 1, axis=0)[SUBLANES:]
    u2 = pltpu.roll(ext, 2, axis=0)[SUBLANES:]
    carry_ref[...] = u[tc - SUBLANES:, :]
    cw = cw_ref[...]
    y = b_gate * (cw[0:1] * u2 + cw[1:2] * u1 + cw[2:3] * u)
    ms = _group_sums(y * y, CONV_GROUP_DIM) * (1.0 / CONV_GROUP_DIM)
    yc_ref[0] = y * lax.rsqrt(ms + NORM_EPS) * gn_ref[...]


def _in_proj_conv(x, g1, mod6, w_in_bf, conv_w, conv_gn):
    bsz, seq, _ = x.shape
    tm = TOKEN_TILE
    n_in = w_in_bf.shape[1]
    const = lambda m, n: pl.BlockSpec((m, n), lambda b, i: (0, 0))
    return pl.pallas_call(
        _inproj_body,
        name="in_proj_conv",
        grid=(bsz, seq // tm),
        in_specs=[pl.BlockSpec((1, tm, D_MODEL), lambda b, i: (b, i, 0)),
                  const(1, D_MODEL),
                  pl.BlockSpec((1, 1, 1, D_MODEL), lambda b, i: (b, 0, 0, 0)),
                  pl.BlockSpec((1, 1, 1, D_MODEL), lambda b, i: (b, 1, 0, 0)),
                  const(D_MODEL, n_in), const(3, D_CONV), const(1, D_CONV)],
        out_specs=[pl.BlockSpec((1, tm, D_CONV), lambda b, i: (b, i, 0)),
                   pl.BlockSpec((1, tm, COLS_RWKV), lambda b, i: (b, i, 0))],
        out_shape=[jax.ShapeDtypeStruct((bsz, seq, D_CONV), F32),
                   jax.ShapeDtypeStruct((bsz, seq, COLS_RWKV), F32)],
        scratch_shapes=[pltpu.VMEM((SUBLANES, D_CONV), F32)],
        compiler_params=_cparams("parallel", "arbitrary"),
    )(x, g1, mod6, mod6, w_in_bf, conv_w, conv_gn)


def _rwkv_body(p_ref, mu_ref, w0_ref, wup_ref, a0_ref, aup_ref, gup_ref, kkw_ref, kaw_ref,
               rkw_ref, lnw_ref, lnb_ref, o_ref, state_ref, prev_ref):
    @pl.when(pl.program_id(1) == 0)
    def _():
        state_ref[...] = jnp.zeros_like(state_ref)
        prev_ref[...] = jnp.zeros_like(prev_ref)

    C = CHUNK
    N = HEAD_DIM
    nb = p_ref.shape[0]
    s1, s2, s3 = D_RWKV, 2 * D_RWKV, 3 * D_RWKV
    row1 = lax.broadcasted_iota(I32, (C, 1), 0)
    ri = lax.broadcasted_iota(I32, (C, C), 0)
    ci = lax.broadcasted_iota(I32, (C, C), 1)
    strict = ri > ci
    incl = ri >= ci
    ri2 = lax.broadcasted_iota(I32, (C, 2 * C), 0)
    ci2 = lax.broadcasted_iota(I32, (C, 2 * C), 1)
    ci2 = jnp.where(ci2 >= C, ci2 - C, ci2)
    strict2 = ri2 > ci2
    incl2 = ri2 >= ci2
    eye = jnp.where(ri == ci, 1.0, 0.0).astype(F32)
    tri = jnp.where(incl, 1.0, 0.0).astype(BF16)

    def stages(bb):
        units = []
        P = p_ref[bb]
        Pprev = jnp.where(row1 == 0, prev_ref[bb, 0:1, :], pltpu.roll(P, 1, axis=0))
        prev_ref[bb] = jnp.broadcast_to(P[C - 1:C, :], prev_ref.shape[1:])
        p = P + (Pprev - P) * mu_ref[...]
        r = p[:, :s1]
        k = p[:, s1:s2]
        v = p[:, s2:s3]
        dw = p[:, s3:s3 + DECAY_LORA]
        da = p[:, s3 + DECAY_LORA:s3 + DECAY_LORA + AAA_LORA]
        dg = p[:, s3 + DECAY_LORA + AAA_LORA:]
        z = -(w0_ref[...] + _mm(jnp.tanh(dw), wup_ref[...]))
        softplus = jnp.maximum(z, 0.0) + jnp.log(1.0 + jnp.exp(-jnp.abs(z)))
        logw = -jnp.exp(-softplus - 0.5)
        a = _sigmoid(a0_ref[...] + _mm(da, aup_ref[...]))
        g = _mm(_sigmoid(dg), gup_ref[...])
        kk = k * kkw_ref[...]
        kmod = k * (1.0 + (a - 1.0) * kaw_ref[...])
        cum = sum(jnp.dot(tri, part, preferred_element_type=F32) for part in _split3(logw))
        gam = jnp.exp(cum)
        gam_prev = jnp.exp(cum - logw)
        ginv = jnp.exp(-cum)
        def head_sum(x):
            return _group_sums(x, N)

        kkn = kk * lax.rsqrt(jnp.maximum(head_sum(kk * kk), 1e-24))
        bonus_v = head_sum(r * kmod * rkw_ref[...]) * v
        at_all = (-kkn * gam_prev).astype(BF16)
        rt_all = (r * gam).astype(BF16)
        bt_all = (kkn * a * ginv).astype(BF16)
        kt_all = (kmod * ginv).astype(BF16)
        vb_all = v.astype(BF16)
        for h in range(N_HEADS):
            sl = slice(h * N, (h + 1) * N)
            At, Rt = at_all[:, sl], rt_all[:, sl]
            units.append(dict(
                slot=bb * N_HEADS + h, At=At, Rt=Rt, vb=vb_all[:, sl],
                BK=jnp.concatenate([bt_all[:, sl], kt_all[:, sl]], axis=0),
                AR=jnp.concatenate([At, Rt], axis=0), glast=gam[C - 1:C, sl]))
        yield
        for u in units:
            M = _mm_nt(u["AR"], u["BK"])
            top = jnp.where(strict2, M[:C, :], 0.0)
            u["T"] = eye + top[:, :C]
            u["LL"] = top.astype(BF16)
            u["Lp"] = top[:, :C].astype(BF16)
            u["Tr"] = jnp.where(incl2, M[C:, :], 0.0).astype(BF16)
        yield
        for u in units:
            u["X2"] = _mm(u["LL"], jnp.concatenate([jnp.zeros_like(u["vb"]), u["vb"]], axis=0))
            u["Lp"] = _mm(u["Lp"], u["Lp"]).astype(BF16)
        yield
        for _ in range(4):
            for u in units:
                u["T"] = u["T"] + _mm(u["T"], u["Lp"])
                u["Lp"] = _mm(u["Lp"], u["Lp"]).astype(BF16)
            yield
        for u in units:
            u["T"] = u["T"] + _mm(u["T"], u["Lp"])
        yield
        for u in units:
            u["W"] = _mm(u["T"], jnp.concatenate([u["At"], u["X2"].astype(BF16)], axis=1))
        yield
        for u in units:
            u["S0"] = state_ref[u["slot"]]
            u["S0b"] = u["S0"].astype(BF16)
            U = _mm_nt(u["W"][:, :N], u["S0b"]) + u["W"][:, N:]
            u["UV"] = jnp.concatenate([U.astype(BF16), u["vb"]], axis=0)
        yield
        for u in units:
            u["O"] = _mm_nt(u["Rt"], u["S0b"]) + _mm(u["Tr"], u["UV"])
            state_ref[u["slot"]] = (u["S0"] + _mm_tn(u["UV"], u["BK"])) * u["glast"]
        yield
        o_all = jnp.concatenate([u["O"] for u in units], axis=-1)
        cen = o_all - head_sum(o_all) * (1.0 / N)
        var = head_sum(cen * cen) * (1.0 / N)
        on = cen * lax.rsqrt(var + GN_EPS) * lnw_ref[...] + lnb_ref[...]
        o_ref[bb] = (on + bonus_v) * g
        yield

    live = [(bb, stages(bb)) for bb in range(nb)]
    tick = 0
    while live:
        for item in list(live):
            if tick >= RWKV_STAGE_LAG * item[0] and next(item[1], "done") == "done":
                live.remove(item)
        tick += 1


def _rwkv_mixer(pr, mu, w0, w_up, a0, a_up, g_up, k_k, k_a, r_k, ln_w, ln_b, bsz, seq):
    C = CHUNK
    nb = RWKV_SEQS_PER_STEP if bsz % RWKV_SEQS_PER_STEP == 0 else 1
    row = lambda n: pl.BlockSpec((1, n), lambda b, i: (0, 0))
    full = lambda m, n: pl.BlockSpec((m, n), lambda b, i: (0, 0))
    return pl.pallas_call(
        _rwkv_body,
        name="rwkv_mixer",
        grid=(bsz // nb, seq // C),
        in_specs=[pl.BlockSpec((nb, C, COLS_RWKV), lambda b, i: (b, i, 0)),
                  row(COLS_RWKV), row(D_RWKV), full(DECAY_LORA, D_RWKV), row(D_RWKV),
                  full(AAA_LORA, D_RWKV), full(GATE_LORA, D_RWKV), row(D_RWKV), row(D_RWKV),
                  row(D_RWKV), row(D_RWKV), row(D_RWKV)],
        out_specs=pl.BlockSpec((nb, C, D_RWKV), lambda b, i: (b, i, 0)),
        out_shape=jax.ShapeDtypeStruct((bsz, seq, D_RWKV), F32),
        scratch_shapes=[pltpu.VMEM((nb * N_HEADS, HEAD_DIM, HEAD_DIM), F32),
                        pltpu.VMEM((nb, SUBLANES, COLS_RWKV), F32)],
        compiler_params=_cparams("parallel", "arbitrary"),
    )(pr.reshape(bsz, seq, COLS_RWKV), mu.reshape(1, -1), w0.reshape(1, -1), w_up,
      a0.reshape(1, -1), a_up, g_up, k_k.reshape(1, -1), k_a.reshape(1, -1), r_k.reshape(1, -1),
      ln_w.reshape(1, -1), ln_b.reshape(1, -1))


def _outproj_body(yc_ref, yr_ref, x_ref, wo_ref, gt_ref, g2_ref, sh_ref, sc_ref, rwh_ref, rwl_ref,
                  rb_ref, h_ref, route_ref, tab_ref, xl_ref):
    mix = _mm(yc_ref[...], wo_ref[:D_CONV, :]) + _mm(yr_ref[...], wo_ref[D_CONV:, :])
    h = x_ref[...] + gt_ref[0, 0] * mix
    h_ref[...] = h
    u2 = _rms(h) * g2_ref[...]
    u2 = u2 * (1.0 + sc_ref[0, 0]) + sh_ref[0, 0]

    uh, ul = _split2(u2)
    tm = u2.shape[0]
    logits = (_mm_nt(rwh_ref[...], uh) + _mm_nt(rwl_ref[...], uh) + _mm_nt(rwh_ref[...], ul)
              + jnp.concatenate([rb_ref[...]] * (tm // LANES), axis=1))
    eid = lax.broadcasted_iota(I32, (N_EXPERTS, tm), 0).astype(F32)
    neg = jnp.float32(-jnp.inf)
    l = logits
    tops, idxs = [], []
    for _ in range(TOP_K):
        m = jnp.max(l, axis=0, keepdims=True)
        ix = jnp.min(jnp.where(l == m, eid, float(N_EXPERTS)), axis=0, keepdims=True)
        tops.append(m)
        idxs.append(ix)
        l = jnp.where(eid == ix, neg, l)
    es = [jnp.exp(t - tops[0]) for t in tops]
    den = es[0] + es[1] + es[2] + es[3]
    gates = [e / den for e in es]
    lpos_rows = _sort_tiles(eid, idxs, uh, tab_ref, xl_ref)
    row8 = lax.broadcasted_iota(I32, (SUBLANES, tm), 0)
    packed = jnp.zeros((SUBLANES, tm), F32)
    for kq, val in enumerate(gates + lpos_rows):
        packed = jnp.where(row8 == kq, val, packed)
    route_ref[...] = jnp.transpose(
        jnp.concatenate([packed, jnp.zeros((LANES - SUBLANES, tm), F32)], axis=0))


def _out_proj_router(yc, yr, xf, w_out_bf, mod6, g2, rw_hi, rw_lo, rb, tiles_per_batch):
    n_tok = xf.shape[0]
    tm = TOKEN_TILE
    n_sub = tm // COMBINE_TILE
    n_seg = D_MODEL // LANES
    tok = lambda n: pl.BlockSpec((tm, n), lambda i: (i, 0))
    modspec = lambda which: pl.BlockSpec((1, 1, 1, D_MODEL),
                                         lambda i: (i // tiles_per_batch, which, 0, 0))
    const = lambda m, n: pl.BlockSpec((m, n), lambda i: (0, 0))
    return pl.pallas_call(
        _outproj_body,
        name="out_proj_router",
        grid=(n_tok // tm,),
        in_specs=[tok(D_CONV), tok(D_RWKV), tok(D_MODEL), const(D_MODEL, D_MODEL),
                  modspec(2), const(1, D_MODEL), modspec(3), modspec(4),
                  const(N_EXPERTS, D_MODEL), const(N_EXPERTS, D_MODEL), const(N_EXPERTS, LANES)],
        out_specs=[tok(D_MODEL), tok(LANES),
                   pl.BlockSpec((n_sub, N_EXPERTS, LANES), lambda i: (i, 0, 0)),
                   pl.BlockSpec((tm * TOP_K * n_seg, LANES), lambda i: (i, 0))],
        out_shape=[jax.ShapeDtypeStruct((n_tok, D_MODEL), F32),
                   jax.ShapeDtypeStruct((n_tok, LANES), F32),
                   jax.ShapeDtypeStruct((n_tok // COMBINE_TILE, N_EXPERTS, LANES), F32),
                   jax.ShapeDtypeStruct((n_tok * TOP_K * n_seg, LANES), F32)],
        compiler_params=_cparams("parallel"),
    )(yc, yr, xf, w_out_bf, mod6, g2, mod6, mod6, rw_hi, rw_lo, rb)


def _sort_tiles(eid, idxs, u_all, tab_ref, xl_ref):
    tm = COMBINE_TILE
    n_sub = u_all.shape[0] // tm
    n_loc = tm * TOP_K
    n_seg = D_MODEL // LANES
    rr = lax.broadcasted_iota(I32, (tm, tm), 0)
    cc = lax.broadcasted_iota(I32, (tm, tm), 1)
    before = jnp.where(rr < cc, 1.0, 0.0).astype(BF16)
    ones_col = jnp.ones((tm, LANES), BF16)
    er = lax.broadcasted_iota(I32, (N_EXPERTS, N_EXPERTS), 0)
    ec = lax.broadcasted_iota(I32, (N_EXPERTS, N_EXPERTS), 1)
    lower = jnp.where(er > ec, 1.0, 0.0).astype(BF16)
    lane = lax.broadcasted_iota(I32, (N_EXPERTS, LANES), 1)
    q = lax.broadcasted_iota(I32, (n_loc, tm), 0)
    subs = [dict(tok=slice(s * tm, (s + 1) * tm)) for s in range(n_sub)]
    chosen = [jnp.broadcast_to(ix, eid.shape) for ix in idxs]
    eid_tile = lax.broadcasted_iota(I32, (N_EXPERTS, tm), 0).astype(F32)

    for sb in subs:
        sb["masks"] = [eid_tile == ch[:, sb["tok"]] for ch in chosen]
        sel = jnp.zeros((N_EXPERTS, tm), F32)
        for mk in sb["masks"]:
            sel = sel + jnp.where(mk, 1.0, 0.0)
        sb["sel"] = sel.astype(BF16)
    for sb in subs:
        sb["prefix"] = jnp.dot(sb["sel"], before, preferred_element_type=F32)
        sb["cnt"] = jnp.dot(sb["sel"], ones_col, preferred_element_type=F32)
    for sb in subs:
        sb["loff"] = jnp.dot(lower, sb["cnt"].astype(BF16), preferred_element_type=F32)
    for s, sb in enumerate(subs):
        tab_ref[s] = jnp.where(lane == 0, sb["cnt"], jnp.where(lane == 1, sb["loff"], 0.0))
        local_row = jnp.concatenate([sb["loff"]] * (tm // LANES), axis=1) + sb["prefix"]
        sb["lpos"] = [jnp.sum(jnp.where(mk, local_row, 0.0), axis=0, keepdims=True)
                      for mk in sb["masks"]]
    for sb in subs:
        pick = jnp.zeros((n_loc, tm), F32)
        for lp in sb["lpos"]:
            pick = jnp.where(q == lp.astype(I32), 1.0, pick)
        sb["xl"] = jnp.dot(pick.astype(BF16), u_all[sb["tok"], :], preferred_element_type=F32)
    for s, sb in enumerate(subs):
        for j in range(n_seg):
            xl_ref[pl.ds(s * n_loc * n_seg + j, n_loc, stride=n_seg), :] = (
                sb["xl"][:, j * LANES:(j + 1) * LANES])
    return [jnp.concatenate([jnp.broadcast_to(sb["lpos"][kq], (SUBLANES, tm)) for sb in subs], axis=1)
            for kq in range(TOP_K)]


def _deinterleave_into(w_ref, o_ref):
    d_in, two_f = o_ref.shape
    half = two_f // 2
    src = lax.broadcasted_iota(I32, (2 * MXU_TILE, MXU_TILE), 0)
    dst = lax.broadcasted_iota(I32, (2 * MXU_TILE, MXU_TILE), 1)
    pick_even = jnp.where(src == 2 * dst, 1.0, 0.0).astype(BF16)
    pick_odd = jnp.where(src == 2 * dst + 1, 1.0, 0.0).astype(BF16)
    for r in range(d_in // MXU_TILE):
        rows = slice(r * MXU_TILE, (r + 1) * MXU_TILE)
        for j in range(half // MXU_TILE):
            slab = w_ref[0, rows, 2 * MXU_TILE * j:2 * MXU_TILE * (j + 1)].astype(BF16)
            lo = MXU_TILE * j
            o_ref[rows, lo:lo + MXU_TILE] = jnp.dot(
                slab, pick_even, preferred_element_type=F32).astype(BF16)
            o_ref[rows, half + lo:half + lo + MXU_TILE] = jnp.dot(
                slab, pick_odd, preferred_element_type=F32).astype(BF16)


def _expert_body(be_ref, nu_ref, ft_ref, lt_ref, nv_ref, rg_ref, rl_ref, rc_ref,
                 xl_ref, w1_ref, w2_ref, bg_ref, bl_ref, b2_ref, y_ref, xbuf_ref, w1d_ref, sem):
    i = pl.program_id(0)
    n_used = nu_ref[0]
    bm = ROW_BLOCK
    n_seg = D_MODEL // LANES
    n_loc = COMBINE_TILE * TOP_K

    def issue(blk, slot):
        e = be_ref[blk]
        r0 = blk * bm

        def piece(t, carry):
            g = rg_ref[t * N_EXPERTS + e]
            s = jnp.maximum(g, r0)
            n = jnp.minimum(g + rc_ref[t * N_EXPERTS + e], r0 + bm) - s

            @pl.when(n > 0)
            def _():
                src = pl.multiple_of((t * n_loc + rl_ref[t * N_EXPERTS + e] + s - g) * n_seg, n_seg)
                dst = pl.multiple_of((s - r0) * n_seg, n_seg)
                pltpu.make_async_copy(xl_ref.at[pl.ds(src, n * n_seg)],
                                      xbuf_ref.at[slot, pl.ds(dst, n * n_seg)], sem.at[slot]).start()
            return carry

        lax.fori_loop(ft_ref[blk], lt_ref[blk] + 1, piece, 0)

    @pl.when(i == 0)
    def _():
        xbuf_ref[...] = jnp.zeros_like(xbuf_ref)
        issue(0, 0)

    @pl.when(i + 1 < n_used)
    def _():
        issue(i + 1, (i + 1) % 2)

    @pl.when(i >= n_used)
    def _():
        y_ref[...] = jnp.zeros_like(y_ref)

    new_expert = jnp.logical_or(i == 0, be_ref[i] != be_ref[jnp.maximum(i - 1, 0)])

    @pl.when(jnp.logical_and(new_expert, i < n_used))
    def _():
        _deinterleave_into(w1_ref, w1d_ref)

    @pl.when(i < n_used)
    def _():
        slot = i % 2
        n_valid = nv_ref[i] * n_seg
        pltpu.make_async_copy(xl_ref.at[pl.ds(0, n_valid)], xbuf_ref.at[slot, pl.ds(0, n_valid)],
                              sem.at[slot]).wait()
        x = jnp.concatenate([xbuf_ref[slot, pl.ds(j, bm, stride=n_seg), :] for j in range(n_seg)],
                            axis=1).astype(BF16)
        hb = jnp.dot(x, w1d_ref[...], preferred_element_type=F32)
        hg = hb[:, :D_EXPERT] + bg_ref[0]
        hl = hb[:, D_EXPERT:] + bl_ref[0]
        xg = jnp.minimum(hg, SWIGLU_LIMIT)
        xl = jnp.clip(hl, -SWIGLU_LIMIT, SWIGLU_LIMIT)
        act = xg * _sigmoid(SWIGLU_ALPHA * xg) * (xl + 1.0)
        y = jnp.dot(act.astype(BF16), w2_ref[0].astype(BF16),
                    preferred_element_type=F32) + b2_ref[0]
        for j in range(n_seg):
            y_ref[pl.ds(j, bm, stride=n_seg), :] = y[:, j * LANES:(j + 1) * LANES]


def _experts(x_local, block_tables, run_tables, w1, w2, b1g, b1l, b2, m_pad):
    bm = ROW_BLOCK
    n_seg = D_MODEL // LANES
    wsel = lambda i, be, *_: (be[i], 0, 0)
    grid_spec = pltpu.PrefetchScalarGridSpec(
        num_scalar_prefetch=8,
        grid=(m_pad // bm,),
        in_specs=[pl.BlockSpec(memory_space=pl.ANY),
                  pl.BlockSpec((1, D_MODEL, 2 * D_EXPERT), wsel),
                  pl.BlockSpec((1, D_EXPERT, D_MODEL), wsel),
                  pl.BlockSpec((1, 1, D_EXPERT), wsel),
                  pl.BlockSpec((1, 1, D_EXPERT), wsel),
                  pl.BlockSpec((1, 1, D_MODEL), wsel)],
        out_specs=pl.BlockSpec((bm * n_seg, LANES), lambda i, *_: (i, 0)),
        scratch_shapes=[pltpu.VMEM((2, bm * n_seg, LANES), F32),
                        pltpu.VMEM((D_MODEL, 2 * D_EXPERT), BF16),
                        pltpu.SemaphoreType.DMA((2,))],
    )
    return pl.pallas_call(
        _expert_body,
        name="expert_mlp",
        grid_spec=grid_spec,
        out_shape=jax.ShapeDtypeStruct((m_pad * n_seg, LANES), F32),
        compiler_params=pltpu.CompilerParams(dimension_semantics=("arbitrary",),
                                             vmem_limit_bytes=EXPERT_VMEM_LIMIT),
    )(*block_tables, *run_tables, x_local, w1, w2, b1g, b1l, b2)


def _combine_body(rg_ref, rl_ref, rc_ref, y_ref, route_ref, h_ref, gt_ref, fg_ref, o_ref,
                  yloc_ref, sem):
    i = pl.program_id(0)
    tm = COMBINE_TILE
    n_sub = h_ref.shape[0] // tm
    n_seg = D_MODEL // LANES
    n_loc = tm * TOP_K

    def issue(step, slot):
        for s in range(n_sub):
            tile = step * n_sub + s
            for e in range(N_EXPERTS):
                c = rc_ref[tile * N_EXPERTS + e] * n_seg

                @pl.when(c > 0)
                def _():
                    src = pl.multiple_of(rg_ref[tile * N_EXPERTS + e] * n_seg, n_seg)
                    dst = pl.multiple_of((s * n_loc + rl_ref[tile * N_EXPERTS + e]) * n_seg, n_seg)
                    pltpu.make_async_copy(y_ref.at[pl.ds(src, c)], yloc_ref.at[slot, pl.ds(dst, c)],
                                          sem.at[slot]).start()

    @pl.when(i == 0)
    def _():
        issue(0, 0)

    @pl.when(i + 1 < pl.num_programs(0))
    def _():
        issue(i + 1, (i + 1) % 2)

    slot = i % 2
    pltpu.make_async_copy(y_ref.at[pl.ds(0, n_sub * n_loc * n_seg)], yloc_ref.at[slot],
                          sem.at[slot]).wait()

    q = lax.broadcasted_iota(I32, (tm, n_loc), 1)
    subs = [dict(tok=slice(s * tm, (s + 1) * tm)) for s in range(n_sub)]
    for s, sb in enumerate(subs):
        sb["y"] = jnp.concatenate(
            [yloc_ref[slot, pl.ds(s * n_loc * n_seg + j, n_loc, stride=n_seg), :]
             for j in range(n_seg)], axis=1).astype(BF16)
        route = route_ref[sb["tok"], :]
        lpos = route.astype(I32)
        gmat = jnp.zeros((tm, n_loc), F32)
        for kq in range(TOP_K):
            gmat = jnp.where(q == lpos[:, TOP_K + kq:TOP_K + kq + 1], route[:, kq:kq + 1], gmat)
        sb["ghi"], sb["glo"] = _split2(gmat)
    for sb in subs:
        sb["moe"] = (jnp.dot(sb["ghi"], sb["y"], preferred_element_type=F32)
                     + jnp.dot(sb["glo"], sb["y"], preferred_element_type=F32))
    for sb in subs:
        h = h_ref[sb["tok"], :] + gt_ref[0, 0] * sb["moe"]
        o_ref[sb["tok"], :] = _rms(h) * fg_ref[...]


def _combine(y_rows, run_start, run_local, run_count, route, h, mod6, final_g, tiles_per_batch):
    n_tok = h.shape[0]
    tm = COMBINE_TILE * COMBINE_TILES_PER_STEP
    grid_spec = pltpu.PrefetchScalarGridSpec(
        num_scalar_prefetch=3,
        grid=(n_tok // tm,),
        in_specs=[pl.BlockSpec(memory_space=pl.ANY),
                  pl.BlockSpec((tm, LANES), lambda i, *_: (i, 0)),
                  pl.BlockSpec((tm, D_MODEL), lambda i, *_: (i, 0)),
                  pl.BlockSpec((1, 1, 1, D_MODEL), lambda i, *_: (i // tiles_per_batch, 5, 0, 0)),
                  pl.BlockSpec((1, D_MODEL), lambda i, *_: (0, 0))],
        out_specs=pl.BlockSpec((tm, D_MODEL), lambda i, *_: (i, 0)),
        scratch_shapes=[pltpu.VMEM((2, tm * TOP_K * (D_MODEL // LANES), LANES), F32),
                        pltpu.SemaphoreType.DMA((2,))],
    )
    return pl.pallas_call(
        _combine_body,
        name="combine_rows",
        grid_spec=grid_spec,
        out_shape=jax.ShapeDtypeStruct((n_tok, D_MODEL), F32),
        compiler_params=_cparams("arbitrary"),
    )(run_start, run_local, run_count, y_rows, route, h, mod6, final_g)


def kernel(x, c, ada_w, ada_b, norm1_g, w_in, conv_w, conv_gn, rwkv_mu, rwkv_w0, rwkv_w_up,
           rwkv_a0, rwkv_a_up, rwkv_g_up, rwkv_k_k, rwkv_k_a, rwkv_r_k, rwkv_ln_w, rwkv_ln_b,
           w_out, norm2_g, router_w, router_b, exp_w1, exp_b1, exp_w2, exp_b2, final_g):
    bsz, seq, _ = x.shape
    depth = ada_w.shape[0]
    assert depth == 1, "the final norm is fused into the last layer's combine step"
    n_tok = bsz * seq
    assert seq % TOKEN_TILE == 0 and seq % CHUNK == 0 and seq % COMBINE_TILE == 0
    n_rows = n_tok * TOP_K
    assert n_rows % ROW_BLOCK == 0
    n_blocks = n_rows // ROW_BLOCK + N_EXPERTS
    m_pad = n_blocks * ROW_BLOCK

    h = x.reshape(n_tok, D_MODEL)
    out = h
    for l in range(depth):
        mod6 = _ada_mod(c, ada_w[l], ada_b[l]).reshape(bsz, 6, 1, D_MODEL)
        yc, pr = _in_proj_conv(h.reshape(bsz, seq, D_MODEL), norm1_g[l].reshape(1, -1), mod6,
                               w_in[l].astype(BF16), conv_w[l], conv_gn[l].reshape(1, -1))
        yr = _rwkv_mixer(pr, rwkv_mu[l], rwkv_w0[l], rwkv_w_up[l], rwkv_a0[l], rwkv_a_up[l],
                         rwkv_g_up[l], rwkv_k_k[l], rwkv_k_a[l], rwkv_r_k[l], rwkv_ln_w[l],
                         rwkv_ln_b[l], bsz, seq)
        rw = router_w[l].T
        rw_hi = rw.astype(BF16)
        rw_lo = (rw - rw_hi.astype(F32)).astype(BF16)
        rb = jnp.broadcast_to(router_b[l][:, None], (N_EXPERTS, LANES))
        h, route, tab, x_local = _out_proj_router(
            yc.reshape(n_tok, D_CONV), yr.reshape(n_tok, D_RWKV), h, w_out[l].astype(BF16), mod6,
            norm2_g[l].reshape(1, -1), rw_hi, rw_lo, rb, seq // TOKEN_TILE)
        run_count = tab[:, :, 0].astype(I32)
        run_local = tab[:, :, 1].astype(I32)
        counts = jnp.sum(run_count, axis=0)
        padded = (counts + ROW_BLOCK - 1) // ROW_BLOCK * ROW_BLOCK
        pad_end = jnp.cumsum(padded)
        pad_start = pad_end - padded
        run_start = pad_start[None, :] + jnp.cumsum(run_count, axis=0) - run_count
        n_used = (pad_end[-1] // ROW_BLOCK).astype(I32).reshape(1)
        blk_row = jnp.minimum(jnp.arange(n_blocks, dtype=I32), n_used[0] - 1) * ROW_BLOCK
        block_expert = jnp.minimum(
            jnp.sum((pad_end[None, :] <= blk_row[:, None]).astype(I32), axis=1), N_EXPERTS - 1)
        blk_start = run_start.T[block_expert]
        blk_end = blk_start + run_count.T[block_expert]
        first_tile = jnp.sum((blk_end <= blk_row[:, None]).astype(I32), axis=1)
        last_tile = jnp.sum((blk_start < blk_row[:, None] + ROW_BLOCK).astype(I32), axis=1) - 1
        n_valid = jnp.clip((pad_start + counts)[block_expert] - blk_row, 0, ROW_BLOCK).astype(I32)
        run_tables = (run_start.reshape(-1), run_local.reshape(-1), run_count.reshape(-1))
        y_rows = _experts(x_local, (block_expert, n_used, first_tile, last_tile, n_valid), run_tables,
                          exp_w1[l], exp_w2[l],
                          exp_b1[l][:, None, 0::2], exp_b1[l][:, None, 1::2], exp_b2[l][:, None, :],
                          m_pad)
        out = _combine(y_rows, *run_tables, route, h, mod6, final_g.reshape(1, -1),
                       seq // (COMBINE_TILE * COMBINE_TILES_PER_STEP))
    return out.reshape(bsz, seq, D_MODEL)
```

```python
import jax
import jax.numpy as jnp
from jax import lax
from jax.experimental import pallas as pl
from jax.experimental.pallas import tpu as pltpu

F32 = jnp.float32
BF16 = jnp.bfloat16
I32 = jnp.int32

D_MODEL = 1024
D_CONV = 512
CONV_GROUP_DIM = 64
D_RWKV = 512
HEAD_DIM = 64
N_HEADS = D_RWKV // HEAD_DIM
DECAY_LORA = 64
AAA_LORA = 64
GATE_LORA = 128
COLS_CONV = 3 * D_CONV
COLS_RWKV = 3 * D_RWKV + DECAY_LORA + AAA_LORA + GATE_LORA
N_EXPERTS = 32
TOP_K = 4
D_EXPERT = D_MODEL
SWIGLU_LIMIT = 7.0
SWIGLU_ALPHA = 1.702
NORM_EPS = 1e-5
GN_EPS = 64e-5

LANES = 128
SUBLANES = 8
MXU_TILE = 256
CHUNK = 64
RWKV_SEQS_PER_STEP = 16
RWKV_STAGE_LAG = 3
ROW_BLOCK = 512
TOKEN_TILE = 512
COMBINE_TILE = 256
COMBINE_TILES_PER_STEP = 2
VMEM_LIMIT = 48 * 1024 * 1024
EXPERT_VMEM_LIMIT = 56 * 1024 * 1024


def _cparams(*sem):
    return pltpu.CompilerParams(dimension_semantics=sem, vmem_limit_bytes=VMEM_LIMIT)


def _mm(a, b):
    return jnp.dot(a.astype(BF16), b.astype(BF16), preferred_element_type=F32)


def _mm_nt(a, b):
    return lax.dot_general(a.astype(BF16), b.astype(BF16), (((1,), (1,)), ((), ())),
                           preferred_element_type=F32)


def _mm_tn(a, b):
    return lax.dot_general(a.astype(BF16), b.astype(BF16), (((0,), (0,)), ((), ())),
                           preferred_element_type=F32)


def _split2(x):
    hi = x.astype(BF16)
    lo = (x - hi.astype(F32)).astype(BF16)
    return hi, lo


def _split3(x):
    hi = x.astype(BF16)
    r = x - hi.astype(F32)
    mid = r.astype(BF16)
    lo = (r - mid.astype(F32)).astype(BF16)
    return hi, mid, lo


def _sigmoid(x):
    return 1.0 / (1.0 + jnp.exp(-x))


def _rms(x):
    return x * lax.rsqrt(jnp.mean(x * x, axis=-1, keepdims=True) + NORM_EPS)


def _ada_body(c_ref, w_ref, b_ref, o_ref):
    c = c_ref[...]
    ca = c * _sigmoid(c)
    ah, al = _split2(ca)
    wh, wl = _split2(w_ref[...])
    acc = jnp.dot(ah, wh, preferred_element_type=F32)
    acc += jnp.dot(ah, wl, preferred_element_type=F32)
    acc += jnp.dot(al, wh, preferred_element_type=F32)
    o_ref[...] = acc + b_ref[...]


def _ada_mod(c, ada_w, ada_b):
    bsz = c.shape[0]
    n = ada_w.shape[1]
    tn = 1024
    return pl.pallas_call(
        _ada_body,
        name="ada_mod",
        grid=(n // tn,),
        in_specs=[pl.BlockSpec((bsz, D_MODEL), lambda j: (0, 0)),
                  pl.BlockSpec((D_MODEL, tn), lambda j: (0, j)),
                  pl.BlockSpec((1, tn), lambda j: (0, j))],
        out_specs=pl.BlockSpec((bsz, tn), lambda j: (0, j)),
        out_shape=jax.ShapeDtypeStruct((bsz, n), F32),
        compiler_params=_cparams("parallel"),
    )(c, ada_w, ada_b.reshape(1, n))


def _group_sums(x, group):
    li = lax.broadcasted_iota(I32, (LANES, LANES), 0)
    lj = lax.broadcasted_iota(I32, (LANES, LANES), 1)
    same = jnp.where((li < group) == (lj < group), 1.0, 0.0).astype(BF16)
    hi, lo = _split2(x)
    cols = []
    for t in range(x.shape[1] // LANES):
        ts = slice(t * LANES, (t + 1) * LANES)
        cols.append(jnp.dot(hi[:, ts], same, preferred_element_type=F32)
                    + jnp.dot(lo[:, ts], same, preferred_element_type=F32))
    return jnp.concatenate(cols, axis=1)


def _inproj_body(x_ref, g_ref, sh_ref, sc_ref, w_ref, cw_ref, gn_ref, yc_ref, or_ref, carry_ref):
    @pl.when(pl.program_id(1) == 0)
    def _():
        carry_ref[...] = jnp.zeros_like(carry_ref)

    ub = _rms(x_ref[0]) * g_ref[...]
    ub = (ub * (1.0 + sc_ref[0, 0]) + sh_ref[0, 0]).astype(BF16)
    p = jnp.dot(ub, w_ref[:, :COLS_CONV], preferred_element_type=F32)
    or_ref[0] = jnp.dot(ub, w_ref[:, COLS_CONV:], preferred_element_type=F32)

    tc = p.shape[0]
    b_gate = p[:, :D_CONV]
    u = p[:, D_CONV:2 * D_CONV] * p[:, 2 * D_CONV:]
    ext = jnp.concatenate([carry_ref[...], u], axis=0)
    u1 = pltpu.roll(ext, 1, axis=0)[SUBLANES:]
    u2 = pltpu.roll(ext, 2, axis=0)[SUBLANES:]
    carry_ref[...] = u[tc - SUBLANES:, :]
    cw = cw_ref[...]
    y = b_gate * (cw[0:1] * u2 + cw[1:2] * u1 + cw[2:3] * u)
    ms = _group_sums(y * y, CONV_GROUP_DIM) * (1.0 / CONV_GROUP_DIM)
    yc_ref[0] = (y * lax.rsqrt(ms + NORM_EPS) * gn_ref[...]).astype(BF16)


def _in_proj_conv(x, g1, mod6, w_in_bf, conv_w, conv_gn):
    bsz, seq, _ = x.shape
    tm = TOKEN_TILE
    n_in = w_in_bf.shape[1]
    const = lambda m, n: pl.BlockSpec((m, n), lambda b, i: (0, 0))
    return pl.pallas_call(
        _inproj_body,
        name="in_proj_conv",
        grid=(bsz, seq // tm),
        in_specs=[pl.BlockSpec((1, tm, D_MODEL), lambda b, i: (b, i, 0)),
                  const(1, D_MODEL),
                  pl.BlockSpec((1, 1, 1, D_MODEL), lambda b, i: (b, 0, 0, 0)),
                  pl.BlockSpec((1, 1, 1, D_MODEL), lambda b, i: (b, 1, 0, 0)),
                  const(D_MODEL, n_in), const(3, D_CONV), const(1, D_CONV)],
        out_specs=[pl.BlockSpec((1, tm, D_CONV), lambda b, i: (b, i, 0)),
                   pl.BlockSpec((1, tm, COLS_RWKV), lambda b, i: (b, i, 0))],
        out_shape=[jax.ShapeDtypeStruct((bsz, seq, D_CONV), BF16),
                   jax.ShapeDtypeStruct((bsz, seq, COLS_RWKV), F32)],
        scratch_shapes=[pltpu.VMEM((SUBLANES, D_CONV), F32)],
        compiler_params=_cparams("parallel", "arbitrary"),
    )(x, g1, mod6, mod6, w_in_bf, conv_w, conv_gn)


def _rwkv_body(p_ref, mu_ref, w0_ref, wup_ref, a0_ref, aup_ref, gup_ref, kkw_ref, kaw_ref,
               rkw_ref, lnw_ref, lnb_ref, o_ref, state_ref, prev_ref):
    @pl.when(pl.program_id(1) == 0)
    def _():
        state_ref[...] = jnp.zeros_like(state_ref)
        prev_ref[...] = jnp.zeros_like(prev_ref)

    C = CHUNK
    N = HEAD_DIM
    nb = p_ref.shape[0]
    s1, s2, s3 = D_RWKV, 2 * D_RWKV, 3 * D_RWKV
    row1 = lax.broadcasted_iota(I32, (C, 1), 0)
    ri = lax.broadcasted_iota(I32, (C, C), 0)
    ci = lax.broadcasted_iota(I32, (C, C), 1)
    strict = ri > ci
    incl = ri >= ci
    ri2 = lax.broadcasted_iota(I32, (C, 2 * C), 0)
    ci2 = lax.broadcasted_iota(I32, (C, 2 * C), 1)
    ci2 = jnp.where(ci2 >= C, ci2 - C, ci2)
    strict2 = ri2 > ci2
    incl2 = ri2 >= ci2
    eye = jnp.where(ri == ci, 1.0, 0.0).astype(F32)
    tri = jnp.where(incl, 1.0, 0.0).astype(BF16)

    def stages(bb):
        units = []
        P = p_ref[bb]
        Pprev = jnp.where(row1 == 0, prev_ref[bb, 0:1, :], pltpu.roll(P, 1, axis=0))
        prev_ref[bb] = jnp.broadcast_to(P[C - 1:C, :], prev_ref.shape[1:])
        p = P + (Pprev - P) * mu_ref[...]
        r = p[:, :s1]
        k = p[:, s1:s2]
        v = p[:, s2:s3]
        dw = p[:, s3:s3 + DECAY_LORA]
        da = p[:, s3 + DECAY_LORA:s3 + DECAY_LORA + AAA_LORA]
        dg = p[:, s3 + DECAY_LORA + AAA_LORA:]
        z = -(w0_ref[...] + _mm(jnp.tanh(dw), wup_ref[...]))
        softplus = jnp.maximum(z, 0.0) + jnp.log(1.0 + jnp.exp(-jnp.abs(z)))
        logw = -jnp.exp(-softplus - 0.5)
        a = _sigmoid(a0_ref[...] + _mm(da, aup_ref[...]))
        g = _mm(_sigmoid(dg), gup_ref[...])
        kk = k * kkw_ref[...]
        kmod = k * (1.0 + (a - 1.0) * kaw_ref[...])
        cum = sum(jnp.dot(tri, part, preferred_element_type=F32) for part in _split3(logw))
        gam = jnp.exp(cum)
        gam_prev = jnp.exp(cum - logw)
        ginv = jnp.exp(-cum)
        def head_sum(x):
            return _group_sums(x, N)

        kkn = kk * lax.rsqrt(jnp.maximum(head_sum(kk * kk), 1e-24))
        bonus_v = head_sum(r * kmod * rkw_ref[...]) * v
        at_all = (-kkn * gam_prev).astype(BF16)
        rt_all = (r * gam).astype(BF16)
        bt_all = (kkn * a * ginv).astype(BF16)
        kt_all = (kmod * ginv).astype(BF16)
        vb_all = v.astype(BF16)
        for h in range(N_HEADS):
            sl = slice(h * N, (h + 1) * N)
            At, Rt = at_all[:, sl], rt_all[:, sl]
            units.append(dict(
                slot=bb * N_HEADS + h, At=At, Rt=Rt, vb=vb_all[:, sl],
                BK=jnp.concatenate([bt_all[:, sl], kt_all[:, sl]], axis=0),
                AR=jnp.concatenate([At, Rt], axis=0), glast=gam[C - 1:C, sl]))
        yield
        for u in units:
            M = _mm_nt(u["AR"], u["BK"])
            top = jnp.where(strict2, M[:C, :], 0.0)
            u["T"] = eye + top[:, :C]
            u["LL"] = top.astype(BF16)
            u["Lp"] = top[:, :C].astype(BF16)
            u["Tr"] = jnp.where(incl2, M[C:, :], 0.0).astype(BF16)
        yield
        for u in units:
            u["X2"] = _mm(u["LL"], jnp.concatenate([jnp.zeros_like(u["vb"]), u["vb"]], axis=0))
            u["Lp"] = _mm(u["Lp"], u["Lp"]).astype(BF16)
        yield
        for _ in range(4):
            for u in units:
                u["T"] = u["T"] + _mm(u["T"], u["Lp"])
                u["Lp"] = _mm(u["Lp"], u["Lp"]).astype(BF16)
            yield
        for u in units:
            u["T"] = u["T"] + _mm(u["T"], u["Lp"])
        yield
        for u in units:
            u["W"] = _mm(u["T"], jnp.concatenate([u["At"], u["X2"].astype(BF16)], axis=1))
        yield
        for u in units:
            u["S0"] = state_ref[u["slot"]]
            u["S0b"] = u["S0"].astype(BF16)
            U = _mm_nt(u["W"][:, :N], u["S0b"]) + u["W"][:, N:]
            u["UV"] = jnp.concatenate([U.astype(BF16), u["vb"]], axis=0)
        yield
        for u in units:
            u["O"] = _mm_nt(u["Rt"], u["S0b"]) + _mm(u["Tr"], u["UV"])
            state_ref[u["slot"]] = (u["S0"] + _mm_tn(u["UV"], u["BK"])) * u["glast"]
        yield
        o_all = jnp.concatenate([u["O"] for u in units], axis=-1)
        cen = o_all - head_sum(o_all) * (1.0 / N)
        var = head_sum(cen * cen) * (1.0 / N)
        on = cen * lax.rsqrt(var + GN_EPS) * lnw_ref[...] + lnb_ref[...]
        o_ref[bb] = ((on + bonus_v) * g).astype(BF16)
        yield

    live = [(bb, stages(bb)) for bb in range(nb)]
    tick = 0
    while live:
        for item in list(live):
            if tick >= RWKV_STAGE_LAG * item[0] and next(item[1], "done") == "done":
                live.remove(item)
        tick += 1


def _rwkv_mixer(pr, mu, w0, w_up, a0, a_up, g_up, k_k, k_a, r_k, ln_w, ln_b, bsz, seq):
    C = CHUNK
    nb = RWKV_SEQS_PER_STEP if bsz % RWKV_SEQS_PER_STEP == 0 else 1
    row = lambda n: pl.BlockSpec((1, n), lambda b, i: (0, 0))
    full = lambda m, n: pl.BlockSpec((m, n), lambda b, i: (0, 0))
    return pl.pallas_call(
        _rwkv_body,
        name="rwkv_mixer",
        grid=(bsz // nb, seq // C),
        in_specs=[pl.BlockSpec((nb, C, COLS_RWKV), lambda b, i: (b, i, 0)),
                  row(COLS_RWKV), row(D_RWKV), full(DECAY_LORA, D_RWKV), row(D_RWKV),
                  full(AAA_LORA, D_RWKV), full(GATE_LORA, D_RWKV), row(D_RWKV), row(D_RWKV),
                  row(D_RWKV), row(D_RWKV), row(D_RWKV)],
        out_specs=pl.BlockSpec((nb, C, D_RWKV), lambda b, i: (b, i, 0)),
        out_shape=jax.ShapeDtypeStruct((bsz, seq, D_RWKV), BF16),
        scratch_shapes=[pltpu.VMEM((nb * N_HEADS, HEAD_DIM, HEAD_DIM), F32),
                        pltpu.VMEM((nb, SUBLANES, COLS_RWKV), F32)],
        compiler_params=_cparams("parallel", "arbitrary"),
    )(pr.reshape(bsz, seq, COLS_RWKV), mu.reshape(1, -1), w0.reshape(1, -1), w_up,
      a0.reshape(1, -1), a_up, g_up, k_k.reshape(1, -1), k_a.reshape(1, -1), r_k.reshape(1, -1),
      ln_w.reshape(1, -1), ln_b.reshape(1, -1))


def _outproj_body(yc_ref, yr_ref, x_ref, wo_ref, gt_ref, g2_ref, sh_ref, sc_ref, rwh_ref, rwl_ref,
                  rb_ref, h_ref, route_ref, tab_ref, xl_ref):
    mix = _mm(yc_ref[...], wo_ref[:D_CONV, :]) + _mm(yr_ref[...], wo_ref[D_CONV:, :])
    h = x_ref[...] + gt_ref[0, 0] * mix
    h_ref[...] = h
    u2 = _rms(h) * g2_ref[...]
    u2 = u2 * (1.0 + sc_ref[0, 0]) + sh_ref[0, 0]

    uh, ul = _split2(u2)
    tm = u2.shape[0]
    logits = (_mm_nt(rwh_ref[...], uh) + _mm_nt(rwl_ref[...], uh) + _mm_nt(rwh_ref[...], ul)
              + jnp.concatenate([rb_ref[...]] * (tm // LANES), axis=1))
    eid = lax.broadcasted_iota(I32, (N_EXPERTS, tm), 0).astype(F32)
    neg = jnp.float32(-jnp.inf)
    l = logits
    tops, idxs = [], []
    for _ in range(TOP_K):
        m = jnp.max(l, axis=0, keepdims=True)
        ix = jnp.min(jnp.where(l == m, eid, float(N_EXPERTS)), axis=0, keepdims=True)
        tops.append(m)
        idxs.append(ix)
        l = jnp.where(eid == ix, neg, l)
    es = [jnp.exp(t - tops[0]) for t in tops]
    den = es[0] + es[1] + es[2] + es[3]
    gates = [e / den for e in es]
    lpos_rows = _sort_tiles(eid, idxs, uh, tab_ref, xl_ref)
    row8 = lax.broadcasted_iota(I32, (SUBLANES, tm), 0)
    packed = jnp.zeros((SUBLANES, tm), F32)
    for kq, val in enumerate(gates + lpos_rows):
        packed = jnp.where(row8 == kq, val, packed)
    route_ref[...] = jnp.transpose(
        jnp.concatenate([packed, jnp.zeros((LANES - SUBLANES, tm), F32)], axis=0))


def _out_proj_router(yc, yr, xf, w_out_bf, mod6, g2, rw_hi, rw_lo, rb, tiles_per_batch):
    n_tok = xf.shape[0]
    tm = TOKEN_TILE
    n_sub = tm // COMBINE_TILE
    n_seg = D_MODEL // LANES
    tok = lambda n: pl.BlockSpec((tm, n), lambda i: (i, 0))
    modspec = lambda which: pl.BlockSpec((1, 1, 1, D_MODEL),
                                         lambda i: (i // tiles_per_batch, which, 0, 0))
    const = lambda m, n: pl.BlockSpec((m, n), lambda i: (0, 0))
    return pl.pallas_call(
        _outproj_body,
        name="out_proj_router",
        grid=(n_tok // tm,),
        in_specs=[tok(D_CONV), tok(D_RWKV), tok(D_MODEL), const(D_MODEL, D_MODEL),
                  modspec(2), const(1, D_MODEL), modspec(3), modspec(4),
                  const(N_EXPERTS, D_MODEL), const(N_EXPERTS, D_MODEL), const(N_EXPERTS, LANES)],
        out_specs=[tok(D_MODEL), tok(LANES),
                   pl.BlockSpec((n_sub, N_EXPERTS, LANES), lambda i: (i, 0, 0)),
                   pl.BlockSpec((tm * TOP_K * n_seg, LANES), lambda i: (i, 0))],
        out_shape=[jax.ShapeDtypeStruct((n_tok, D_MODEL), F32),
                   jax.ShapeDtypeStruct((n_tok, LANES), F32),
                   jax.ShapeDtypeStruct((n_tok // COMBINE_TILE, N_EXPERTS, LANES), F32),
                   jax.ShapeDtypeStruct((n_tok * TOP_K * n_seg, LANES), F32)],
        compiler_params=_cparams("parallel"),
    )(yc, yr, xf, w_out_bf, mod6, g2, mod6, mod6, rw_hi, rw_lo, rb)


def _sort_tiles(eid, idxs, u_all, tab_ref, xl_ref):
    tm = COMBINE_TILE
    n_sub = u_all.shape[0] // tm
    n_loc = tm * TOP_K
    n_seg = D_MODEL // LANES
    rr = lax.broadcasted_iota(I32, (tm, tm), 0)
    cc = lax.broadcasted_iota(I32, (tm, tm), 1)
    before = jnp.where(rr < cc, 1.0, 0.0).astype(BF16)
    ones_col = jnp.ones((tm, LANES), BF16)
    er = lax.broadcasted_iota(I32, (N_EXPERTS, N_EXPERTS), 0)
    ec = lax.broadcasted_iota(I32, (N_EXPERTS, N_EXPERTS), 1)
    lower = jnp.where(er > ec, 1.0, 0.0).astype(BF16)
    lane = lax.broadcasted_iota(I32, (N_EXPERTS, LANES), 1)
    q = lax.broadcasted_iota(I32, (n_loc, tm), 0)
    subs = [dict(tok=slice(s * tm, (s + 1) * tm)) for s in range(n_sub)]
    chosen = [jnp.broadcast_to(ix, eid.shape) for ix in idxs]
    eid_tile = lax.broadcasted_iota(I32, (N_EXPERTS, tm), 0).astype(F32)

    for sb in subs:
        sb["masks"] = [eid_tile == ch[:, sb["tok"]] for ch in chosen]
        sel = jnp.zeros((N_EXPERTS, tm), F32)
        for mk in sb["masks"]:
            sel = sel + jnp.where(mk, 1.0, 0.0)
        sb["sel"] = sel.astype(BF16)
    for sb in subs:
        sb["prefix"] = jnp.dot(sb["sel"], before, preferred_element_type=F32)
        sb["cnt"] = jnp.dot(sb["sel"], ones_col, preferred_element_type=F32)
    for sb in subs:
        sb["loff"] = jnp.dot(lower, sb["cnt"].astype(BF16), preferred_element_type=F32)
    for s, sb in enumerate(subs):
        tab_ref[s] = jnp.where(lane == 0, sb["cnt"], jnp.where(lane == 1, sb["loff"], 0.0))
        local_row = jnp.concatenate([sb["loff"]] * (tm // LANES), axis=1) + sb["prefix"]
        sb["lpos"] = [jnp.sum(jnp.where(mk, local_row, 0.0), axis=0, keepdims=True)
                      for mk in sb["masks"]]
    for sb in subs:
        pick = jnp.zeros((n_loc, tm), F32)
        for lp in sb["lpos"]:
            pick = jnp.where(q == lp.astype(I32), 1.0, pick)
        sb["xl"] = jnp.dot(pick.astype(BF16), u_all[sb["tok"], :], preferred_element_type=F32)
    for s, sb in enumerate(subs):
        for j in range(n_seg):
            xl_ref[pl.ds(s * n_loc * n_seg + j, n_loc, stride=n_seg), :] = (
                sb["xl"][:, j * LANES:(j + 1) * LANES])
    return [jnp.concatenate([jnp.broadcast_to(sb["lpos"][kq], (SUBLANES, tm)) for sb in subs], axis=1)
            for kq in range(TOP_K)]


def _deinterleave_into(w_ref, o_ref):
    d_in, two_f = o_ref.shape
    half = two_f // 2
    src = lax.broadcasted_iota(I32, (2 * MXU_TILE, MXU_TILE), 0)
    dst = lax.broadcasted_iota(I32, (2 * MXU_TILE, MXU_TILE), 1)
    pick_even = jnp.where(src == 2 * dst, 1.0, 0.0).astype(BF16)
    pick_odd = jnp.where(src == 2 * dst + 1, 1.0, 0.0).astype(BF16)
    for r in range(d_in // MXU_TILE):
        rows = slice(r * MXU_TILE, (r + 1) * MXU_TILE)
        for j in range(half // MXU_TILE):
            slab = w_ref[0, rows, 2 * MXU_TILE * j:2 * MXU_TILE * (j + 1)].astype(BF16)
            lo = MXU_TILE * j
            o_ref[rows, lo:lo + MXU_TILE] = jnp.dot(
                slab, pick_even, preferred_element_type=F32).astype(BF16)
            o_ref[rows, half + lo:half + lo + MXU_TILE] = jnp.dot(
                slab, pick_odd, preferred_element_type=F32).astype(BF16)


def _expert_body(be_ref, nu_ref, ft_ref, lt_ref, nv_ref, rg_ref, rl_ref, rc_ref,
                 xl_ref, w1_ref, w2_ref, bg_ref, bl_ref, b2_ref, y_ref, xbuf_ref, w1d_ref, sem):
    i = pl.program_id(0)
    n_used = nu_ref[0]
    bm = ROW_BLOCK
    n_seg = D_MODEL // LANES
    n_loc = COMBINE_TILE * TOP_K

    def issue(blk, slot):
        e = be_ref[blk]
        r0 = blk * bm

        def piece(t, carry):
            g = rg_ref[t * N_EXPERTS + e]
            s = jnp.maximum(g, r0)
            n = jnp.minimum(g + rc_ref[t * N_EXPERTS + e], r0 + bm) - s

            @pl.when(n > 0)
            def _():
                src = pl.multiple_of((t * n_loc + rl_ref[t * N_EXPERTS + e] + s - g) * n_seg, n_seg)
                dst = pl.multiple_of((s - r0) * n_seg, n_seg)
                pltpu.make_async_copy(xl_ref.at[pl.ds(src, n * n_seg)],
                                      xbuf_ref.at[slot, pl.ds(dst, n * n_seg)], sem.at[slot]).start()
            return carry

        lax.fori_loop(ft_ref[blk], lt_ref[blk] + 1, piece, 0)

    @pl.when(i == 0)
    def _():
        xbuf_ref[...] = jnp.zeros_like(xbuf_ref)
        issue(0, 0)

    @pl.when(i + 1 < n_used)
    def _():
        issue(i + 1, (i + 1) % 2)

    @pl.when(i >= n_used)
    def _():
        y_ref[...] = jnp.zeros_like(y_ref)

    new_expert = jnp.logical_or(i == 0, be_ref[i] != be_ref[jnp.maximum(i - 1, 0)])

    @pl.when(jnp.logical_and(new_expert, i < n_used))
    def _():
        _deinterleave_into(w1_ref, w1d_ref)

    @pl.when(i < n_used)
    def _():
        slot = i % 2
        n_valid = nv_ref[i] * n_seg
        pltpu.make_async_copy(xl_ref.at[pl.ds(0, n_valid)], xbuf_ref.at[slot, pl.ds(0, n_valid)],
                              sem.at[slot]).wait()
        x = jnp.concatenate([xbuf_ref[slot, pl.ds(j, bm, stride=n_seg), :] for j in range(n_seg)],
                            axis=1).astype(BF16)
        hb = jnp.dot(x, w1d_ref[...], preferred_element_type=F32)
        hg = hb[:, :D_EXPERT] + bg_ref[0]
        hl = hb[:, D_EXPERT:] + bl_ref[0]
        xg = jnp.minimum(hg, SWIGLU_LIMIT)
        xl = jnp.clip(hl, -SWIGLU_LIMIT, SWIGLU_LIMIT)
        act = xg * _sigmoid(SWIGLU_ALPHA * xg) * (xl + 1.0)
        y = jnp.dot(act.astype(BF16), w2_ref[0].astype(BF16),
                    preferred_element_type=F32) + b2_ref[0]
        for j in range(n_seg):
            y_ref[pl.ds(j, bm, stride=n_seg), :] = y[:, j * LANES:(j + 1) * LANES]


def _experts(x_local, block_tables, run_tables, w1, w2, b1g, b1l, b2, m_pad):
    bm = ROW_BLOCK
    n_seg = D_MODEL // LANES
    wsel = lambda i, be, *_: (be[i], 0, 0)
    grid_spec = pltpu.PrefetchScalarGridSpec(
        num_scalar_prefetch=8,
        grid=(m_pad // bm,),
        in_specs=[pl.BlockSpec(memory_space=pl.ANY),
                  pl.BlockSpec((1, D_MODEL, 2 * D_EXPERT), wsel),
                  pl.BlockSpec((1, D_EXPERT, D_MODEL), wsel),
                  pl.BlockSpec((1, 1, D_EXPERT), wsel),
                  pl.BlockSpec((1, 1, D_EXPERT), wsel),
                  pl.BlockSpec((1, 1, D_MODEL), wsel)],
        out_specs=pl.BlockSpec((bm * n_seg, LANES), lambda i, *_: (i, 0)),
        scratch_shapes=[pltpu.VMEM((2, bm * n_seg, LANES), F32),
                        pltpu.VMEM((D_MODEL, 2 * D_EXPERT), BF16),
                        pltpu.SemaphoreType.DMA((2,))],
    )
    return pl.pallas_call(
        _expert_body,
        name="expert_mlp",
        grid_spec=grid_spec,
        out_shape=jax.ShapeDtypeStruct((m_pad * n_seg, LANES), F32),
        compiler_params=pltpu.CompilerParams(dimension_semantics=("arbitrary",),
                                             vmem_limit_bytes=EXPERT_VMEM_LIMIT),
    )(*block_tables, *run_tables, x_local, w1, w2, b1g, b1l, b2)


def _combine_body(rg_ref, rl_ref, rc_ref, y_ref, route_ref, h_ref, gt_ref, fg_ref, o_ref,
                  yloc_ref, sem):
    i = pl.program_id(0)
    tm = COMBINE_TILE
    n_sub = h_ref.shape[0] // tm
    n_seg = D_MODEL // LANES
    n_loc = tm * TOP_K

    def issue(step, slot):
        for s in range(n_sub):
            tile = step * n_sub + s
            for e in range(N_EXPERTS):
                c = rc_ref[tile * N_EXPERTS + e] * n_seg

                @pl.when(c > 0)
                def _():
                    src = pl.multiple_of(rg_ref[tile * N_EXPERTS + e] * n_seg, n_seg)
                    dst = pl.multiple_of((s * n_loc + rl_ref[tile * N_EXPERTS + e]) * n_seg, n_seg)
                    pltpu.make_async_copy(y_ref.at[pl.ds(src, c)], yloc_ref.at[slot, pl.ds(dst, c)],
                                          sem.at[slot]).start()

    @pl.when(i == 0)
    def _():
        issue(0, 0)

    @pl.when(i + 1 < pl.num_programs(0))
    def _():
        issue(i + 1, (i + 1) % 2)

    slot = i % 2
    pltpu.make_async_copy(y_ref.at[pl.ds(0, n_sub * n_loc * n_seg)], yloc_ref.at[slot],
                          sem.at[slot]).wait()

    q = lax.broadcasted_iota(I32, (tm, n_loc), 1)
    subs = [dict(tok=slice(s * tm, (s + 1) * tm)) for s in range(n_sub)]
    for s, sb in enumerate(subs):
        sb["y"] = jnp.concatenate(
            [yloc_ref[slot, pl.ds(s * n_loc * n_seg + j, n_loc, stride=n_seg), :]
             for j in range(n_seg)], axis=1).astype(BF16)
        route = route_ref[sb["tok"], :]
        lpos = route.astype(I32)
        gmat = jnp.zeros((tm, n_loc), F32)
        for kq in range(TOP_K):
            gmat = jnp.where(q == lpos[:, TOP_K + kq:TOP_K + kq + 1], route[:, kq:kq + 1], gmat)
        sb["ghi"], sb["glo"] = _split2(gmat)
    for sb in subs:
        sb["moe"] = (jnp.dot(sb["ghi"], sb["y"], preferred_element_type=F32)
                     + jnp.dot(sb["glo"], sb["y"], preferred_element_type=F32))
    for sb in subs:
        h = h_ref[sb["tok"], :] + gt_ref[0, 0] * sb["moe"]
        o_ref[sb["tok"], :] = _rms(h) * fg_ref[...]


def _combine(y_rows, run_start, run_local, run_count, route, h, mod6, final_g, tiles_per_batch):
    n_tok = h.shape[0]
    tm = COMBINE_TILE * COMBINE_TILES_PER_STEP
    grid_spec = pltpu.PrefetchScalarGridSpec(
        num_scalar_prefetch=3,
        grid=(n_tok // tm,),
        in_specs=[pl.BlockSpec(memory_space=pl.ANY),
                  pl.BlockSpec((tm, LANES), lambda i, *_: (i, 0)),
                  pl.BlockSpec((tm, D_MODEL), lambda i, *_: (i, 0)),
                  pl.BlockSpec((1, 1, 1, D_MODEL), lambda i, *_: (i // tiles_per_batch, 5, 0, 0)),
                  pl.BlockSpec((1, D_MODEL), lambda i, *_: (0, 0))],
        out_specs=pl.BlockSpec((tm, D_MODEL), lambda i, *_: (i, 0)),
        scratch_shapes=[pltpu.VMEM((2, tm * TOP_K * (D_MODEL // LANES), LANES), F32),
                        pltpu.SemaphoreType.DMA((2,))],
    )
    return pl.pallas_call(
        _combine_body,
        name="combine_rows",
        grid_spec=grid_spec,
        out_shape=jax.ShapeDtypeStruct((n_tok, D_MODEL), F32),
        compiler_params=_cparams("arbitrary"),
    )(run_start, run_local, run_count, y_rows, route, h, mod6, final_g)


def kernel(x, c, ada_w, ada_b, norm1_g, w_in, conv_w, conv_gn, rwkv_mu, rwkv_w0, rwkv_w_up,
           rwkv_a0, rwkv_a_up, rwkv_g_up, rwkv_k_k, rwkv_k_a, rwkv_r_k, rwkv_ln_w, rwkv_ln_b,
           w_out, norm2_g, router_w, router_b, exp_w1, exp_b1, exp_w2, exp_b2, final_g):
    bsz, seq, _ = x.shape
    depth = ada_w.shape[0]
    assert depth == 1, "the final norm is fused into the last layer's combine step"
    n_tok = bsz * seq
    assert seq % TOKEN_TILE == 0 and seq % CHUNK == 0 and seq % COMBINE_TILE == 0
    n_rows = n_tok * TOP_K
    assert n_rows % ROW_BLOCK == 0
    n_blocks = n_rows // ROW_BLOCK + N_EXPERTS
    m_pad = n_blocks * ROW_BLOCK

    h = x.reshape(n_tok, D_MODEL)
    out = h
    for l in range(depth):
        mod6 = _ada_mod(c, ada_w[l], ada_b[l]).reshape(bsz, 6, 1, D_MODEL)
        yc, pr = _in_proj_conv(h.reshape(bsz, seq, D_MODEL), norm1_g[l].reshape(1, -1), mod6,
                               w_in[l].astype(BF16), conv_w[l], conv_gn[l].reshape(1, -1))
        yr = _rwkv_mixer(pr, rwkv_mu[l], rwkv_w0[l], rwkv_w_up[l], rwkv_a0[l], rwkv_a_up[l],
                         rwkv_g_up[l], rwkv_k_k[l], rwkv_k_a[l], rwkv_r_k[l], rwkv_ln_w[l],
                         rwkv_ln_b[l], bsz, seq)
        rw = router_w[l].T
        rw_hi = rw.astype(BF16)
        rw_lo = (rw - rw_hi.astype(F32)).astype(BF16)
        rb = jnp.broadcast_to(router_b[l][:, None], (N_EXPERTS, LANES))
        h, route, tab, x_local = _out_proj_router(
            yc.reshape(n_tok, D_CONV), yr.reshape(n_tok, D_RWKV), h, w_out[l].astype(BF16), mod6,
            norm2_g[l].reshape(1, -1), rw_hi, rw_lo, rb, seq // TOKEN_TILE)
        run_count = tab[:, :, 0].astype(I32)
        run_local = tab[:, :, 1].astype(I32)
        counts = jnp.sum(run_count, axis=0)
        padded = (counts + ROW_BLOCK - 1) // ROW_BLOCK * ROW_BLOCK
        pad_end = jnp.cumsum(padded)
        pad_start = pad_end - padded
        run_start = pad_start[None, :] + jnp.cumsum(run_count, axis=0) - run_count
        n_used = (pad_end[-1] // ROW_BLOCK).astype(I32).reshape(1)
        blk_row = jnp.minimum(jnp.arange(n_blocks, dtype=I32), n_used[0] - 1) * ROW_BLOCK
        block_expert = jnp.minimum(
            jnp.sum((pad_end[None, :] <= blk_row[:, None]).astype(I32), axis=1), N_EXPERTS - 1)
        blk_start = run_start.T[block_expert]
        blk_end = blk_start + run_count.T[block_expert]
        first_tile = jnp.sum((blk_end <= blk_row[:, None]).astype(I32), axis=1)
        last_tile = jnp.sum((blk_start < blk_row[:, None] + ROW_BLOCK).astype(I32), axis=1) - 1
        n_valid = jnp.clip((pad_start + counts)[block_expert] - blk_row, 0, ROW_BLOCK).astype(I32)
        run_tables = (run_start.reshape(-1), run_local.reshape(-1), run_count.reshape(-1))
        y_rows = _experts(x_local, (block_expert, n_used, first_tile, last_tile, n_valid), run_tables,
                          exp_w1[l], exp_w2[l],
                          exp_b1[l][:, None, 0::2], exp_b1[l][:, None, 1::2], exp_b2[l][:, None, :],
                          m_pad)
        out = _combine(y_rows, *run_tables, route, h, mod6, final_g.reshape(1, -1),
                       seq // (COMBINE_TILE * COMBINE_TILES_PER_STEP))
    return out.reshape(bsz, seq, D_MODEL)
```
